```python
import jax, jax.numpy as jnp
from jax import lax
import numpy as np

D_MODEL = 1024
BATCH = 4
SEQ = 8192
DEPTH = 1

PLE_DIM = 256
HEAD_DIM = 64
N_ATTN_HEADS = 8
ATTN_WIDTH = N_ATTN_HEADS * HEAD_DIM
CONV_WIDTH = D_MODEL - ATTN_WIDTH
N_CONV_GROUPS = 8
CONV_GROUP_DIM = CONV_WIDTH // N_CONV_GROUPS
MIX_WIDTH = ATTN_WIDTH + CONV_WIDTH
CONV_K = 3
DILATED_GROUPS = ((128, 1), (512, 4), (2048, 16))
ATTN_BLOCK = 128
ROPE_THETA = 500000.0
ROPE_DIM = HEAD_DIM // 4
N_GROUPS = 4
EXPERTS_PER_GROUP = 4
N_EXPERTS = N_GROUPS * EXPERTS_PER_GROUP
TOP_K = 2
D_FF_EXPERT = 512
EPS = 1e-6

kernel_name = "hymba_conv_dilated_attn_hmoe_ple"


def rmsnorm(x, g):
    x32 = x.astype(jnp.float32)
    y = x32 * lax.rsqrt(jnp.mean(x32 * x32, axis=-1, keepdims=True) + EPS)
    return (y * g.astype(jnp.float32)).astype(x.dtype)


def partial_rope(x, pos):
    half = ROPE_DIM // 2
    inv = ROPE_THETA ** (-jnp.arange(0, ROPE_DIM, 2, dtype=jnp.float32) / ROPE_DIM)
    ang = pos.astype(jnp.float32)[:, None] * inv[None, :]
    cos = jnp.cos(ang)[None, :, None, :]
    sin = jnp.sin(ang)[None, :, None, :]
    x32 = x.astype(jnp.float32)
    x1 = x32[..., :half]
    x2 = x32[..., half:ROPE_DIM]
    rot = jnp.concatenate([x1 * cos - x2 * sin, x2 * cos + x1 * sin], axis=-1).astype(x.dtype)
    return jnp.concatenate([rot, x[..., ROPE_DIM:]], axis=-1)


def dilated_window_branch(q, k, v, window, dilation):
    B, S, H, Dh = q.shape
    n_back = window // dilation
    L = S // dilation
    nb = -(-L // ATTN_BLOCK)
    Lp = nb * ATTN_BLOCK

    def split(t):
        t = t.reshape(B, L, dilation, H, Dh)
        t = jnp.pad(t, ((0, 0), (0, Lp - L), (0, 0), (0, 0), (0, 0)))
        return t.reshape(B, nb, ATTN_BLOCK, dilation, H, Dh)

    def with_prev(t):
        prev = jnp.pad(t, ((0, 0), (1, 0), (0, 0), (0, 0), (0, 0), (0, 0)))[:, :-1]
        return jnp.concatenate([prev, t], axis=2)

    qb = split(q)
    kk = with_prev(split(k))
    vv = with_prev(split(v))
    s = jnp.einsum('bnqrhd,bnkrhd->bnrhqk', qb, kk,
                   preferred_element_type=jnp.float32)
    qi = jnp.arange(ATTN_BLOCK)[:, None]
    ki = jnp.arange(2 * ATTN_BLOCK)[None, :]
    dist = qi + ATTN_BLOCK - ki
    band = (dist >= 0) & (dist <= n_back)
    exists = (jnp.arange(nb)[:, None, None] > 0) | (ki[None] >= ATTN_BLOCK)
    valid = band[None] & exists
    s = jnp.where(valid[None, :, None, None], s, -jnp.inf)
    m = jnp.max(s, axis=-1)
    e = jnp.exp(s - m[..., None])
    den = jnp.sum(e, axis=-1)
    num = jnp.einsum('bnrhqk,bnkrhd->bnqrhd', e, vv.astype(jnp.float32))
    num = num.reshape(B, Lp, dilation, H, Dh)[:, :L].reshape(B, S, H, Dh)

    def unblock(t):
        t = t.transpose(0, 1, 4, 2, 3).reshape(B, Lp, dilation, H)
        return t[:, :L].reshape(B, S, H)

    return num, unblock(m), unblock(den)


def dilated_mixture_attention(q, k, v):
    parts = [dilated_window_branch(q, k, v, w, d) for (w, d) in DILATED_GROUPS]
    m_all = jnp.max(jnp.stack([pm for (_, pm, _) in parts], axis=0), axis=0)
    num = 0.0
    den = 0.0
    for (pn, pm, pd) in parts:
        w = jnp.exp(pm - m_all)
        num = num + w[..., None] * pn
        den = den + w * pd
    return num / den[..., None]


def short_gated_conv(b_gate, c_gate, hx, w):
    u = c_gate * hx
    y = lax.conv_general_dilated(
        u, w[:, None, :].astype(u.dtype), window_strides=(1,),
        padding=((CONV_K - 1, 0),), dimension_numbers=('NWC', 'WIO', 'NWC'),
        feature_group_count=u.shape[-1])
    return b_gate * y


def hierarchical_moe(x, w_rg, b_rg, w_re, b_re, w1, w3, w2):
    B, S, _ = x.shape
    lg = jnp.einsum('bsd,dg->bsg', x, w_rg, preferred_element_type=jnp.float32) + b_rg.astype(jnp.float32)
    pg = jax.nn.softmax(lg, axis=-1)
    g_idx = jnp.argmax(lg, axis=-1)
    pg_top = jnp.take_along_axis(pg, g_idx[..., None], axis=-1)
    le = jnp.einsum('bsd,de->bse', x, w_re, preferred_element_type=jnp.float32) + b_re.astype(jnp.float32)
    le = le.reshape(B, S, N_GROUPS, EXPERTS_PER_GROUP)
    le_sel = jnp.take_along_axis(le, g_idx[..., None, None], axis=-2)[..., 0, :]
    pe = jax.nn.softmax(le_sel, axis=-1)
    tv, ti = lax.top_k(pe, TOP_K)
    tv = tv / jnp.sum(tv, axis=-1, keepdims=True)
    within = jnp.sum(jax.nn.one_hot(ti, EXPERTS_PER_GROUP, dtype=jnp.float32) * tv[..., None], axis=-2)
    gates = (pg_top[..., None]
             * jax.nn.one_hot(g_idx, N_GROUPS, dtype=jnp.float32)[..., None]
             * within[..., None, :]).reshape(B, S, N_EXPERTS).astype(x.dtype)
    y = jnp.zeros_like(x)
    for e in range(N_EXPERTS):
        hdn = jax.nn.silu(x @ w1[e]) * (x @ w3[e])
        y = y + gates[..., e:e + 1] * (hdn @ w2[e])
    return y


def setup_inputs(seed: int = 0) -> dict:
    key = jax.random.key(seed)
    ks = jax.random.split(key, 24)
    f32 = jnp.float32

    def nrm(k, shape, fan_in):
        return jax.random.normal(k, shape, f32) * (fan_in ** -0.5)

    def gain(k, shape):
        return 1.0 + 0.05 * jax.random.normal(k, shape, f32)

    n_in = 3 * ATTN_WIDTH + 3 * CONV_WIDTH
    return {
        "x": jax.random.normal(ks[0], (BATCH, SEQ, D_MODEL), f32),
        "p": jax.random.normal(ks[1], (DEPTH, BATCH, SEQ, PLE_DIM), f32),
        "g_mix": gain(ks[2], (DEPTH, D_MODEL)),
        "w_in": nrm(ks[3], (DEPTH, D_MODEL, n_in), D_MODEL),
        "q_norm": gain(ks[4], (DEPTH, HEAD_DIM)),
        "k_norm": gain(ks[5], (DEPTH, HEAD_DIM)),
        "conv_w": nrm(ks[6], (DEPTH, CONV_K, CONV_WIDTH), CONV_K),
        "g_attn_out": gain(ks[7], (DEPTH, ATTN_WIDTH)),
        "g_conv_out": gain(ks[8], (DEPTH, CONV_WIDTH)),
        "w_out": nrm(ks[9], (DEPTH, MIX_WIDTH, D_MODEL), MIX_WIDTH),
        "g_ffn": gain(ks[10], (DEPTH, D_MODEL)),
        "w_router_group": nrm(ks[11], (DEPTH, D_MODEL, N_GROUPS), D_MODEL),
        "b_router_group": 0.01 * jax.random.normal(ks[12], (DEPTH, N_GROUPS), f32),
        "w_router_expert": nrm(ks[13], (DEPTH, D_MODEL, N_EXPERTS), D_MODEL),
        "b_router_expert": 0.01 * jax.random.normal(ks[14], (DEPTH, N_EXPERTS), f32),
        "w1": nrm(ks[15], (DEPTH, N_EXPERTS, D_MODEL, D_FF_EXPERT), D_MODEL),
        "w3": nrm(ks[16], (DEPTH, N_EXPERTS, D_MODEL, D_FF_EXPERT), D_MODEL),
        "w2": nrm(ks[17], (DEPTH, N_EXPERTS, D_FF_EXPERT, D_MODEL), D_FF_EXPERT),
        "g_ple": gain(ks[18], (DEPTH, D_MODEL)),
        "w_ple_gate": nrm(ks[19], (DEPTH, D_MODEL, D_MODEL), D_MODEL),
        "w_ple_proj": nrm(ks[20], (DEPTH, PLE_DIM, D_MODEL), PLE_DIM),
        "g_ple_post": gain(ks[21], (DEPTH, D_MODEL)),
    }


def reference(x, p, g_mix, w_in, q_norm, k_norm, conv_w, g_attn_out, g_conv_out, w_out,
              g_ffn, w_router_group, b_router_group, w_router_expert, b_router_expert,
              w1, w3, w2, g_ple, w_ple_gate, w_ple_proj, g_ple_post):
    B, S, _ = x.shape
    pos = jnp.arange(S)
    A, C = ATTN_WIDTH, CONV_WIDTH
    h = x
    for i in range(DEPTH):
        xn = rmsnorm(h, g_mix[i])
        z = xn @ w_in[i]
        q, k, v, cb, cc, ch = jnp.split(z, [A, 2 * A, 3 * A, 3 * A + C, 3 * A + 2 * C], axis=-1)
        q = q.reshape(B, S, N_ATTN_HEADS, HEAD_DIM)
        k = k.reshape(B, S, N_ATTN_HEADS, HEAD_DIM)
        v = v.reshape(B, S, N_ATTN_HEADS, HEAD_DIM)
        q = partial_rope(rmsnorm(q, q_norm[i]), pos) * (HEAD_DIM ** -0.5)
        k = partial_rope(rmsnorm(k, k_norm[i]), pos)
        attn = dilated_mixture_attention(q, k, v).astype(h.dtype)
        attn = rmsnorm(attn, g_attn_out[i].reshape(N_ATTN_HEADS, HEAD_DIM)).reshape(B, S, A)
        conv = short_gated_conv(cb, cc, ch, conv_w[i])
        conv = rmsnorm(conv.reshape(B, S, N_CONV_GROUPS, CONV_GROUP_DIM),
                       g_conv_out[i].reshape(N_CONV_GROUPS, CONV_GROUP_DIM)).reshape(B, S, C)
        h = h + jnp.concatenate([attn, conv], axis=-1) @ w_out[i]
        h = h + hierarchical_moe(rmsnorm(h, g_ffn[i]), w_router_group[i], b_router_group[i],
                                 w_router_expert[i], b_router_expert[i], w1[i], w3[i], w2[i])
        gate = jax.nn.sigmoid(rmsnorm(h, g_ple[i]) @ w_ple_gate[i])
        h = h + gate * rmsnorm(p[i] @ w_ple_proj[i], g_ple_post[i])
    return h
```

```python
import functools

import numpy as np
import jax
import jax.numpy as jnp
from jax import lax
from jax.experimental import pallas as pl
from jax.experimental.pallas import tpu as pltpu

F32 = jnp.float32
BF16 = jnp.bfloat16

D_MODEL = 1024
PLE_DIM = 256
HEAD_DIM = 64
N_HEADS = 8
ATTN_WIDTH = N_HEADS * HEAD_DIM
CONV_WIDTH = D_MODEL - ATTN_WIDTH
CONV_K = 3
DILATIONS = (1, 4, 16)
N_BACK = 128
ATTN_BLOCK = 128
ROPE_THETA = 500000.0
ROPE_DIM = HEAD_DIM // 4
ROPE_HALF = ROPE_DIM // 2
N_GROUPS = 4
EXPERTS_PER_GROUP = 4
N_EXPERTS = N_GROUPS * EXPERTS_PER_GROUP
D_FF = 512
EPS = 1e-6

LANES = 128
MXU_DIM = 256
NEG_BIG = -1e30

ROW_TILE = 512
ATTN_TILE = 2048
MOE_TILE = 1024
GATE_LANES = LANES
XG_WIDTH = D_MODEL + GATE_LANES
ROUTER_GROUP_COL = 0
ROUTER_EXPERT_COL = 8
VMEM_LIMIT = 48 * 1024 * 1024


def _block_diag_ones(n, seg):
    idx = np.arange(n) // seg
    return jnp.asarray((idx[:, None] == idx[None, :]).astype(np.float32), dtype=BF16)


def _segment_mean_sq(t, bd):
    t2 = (t * t).astype(BF16)
    parts = [jnp.dot(t2[:, i:i + MXU_DIM], bd, preferred_element_type=F32)
             for i in range(0, t.shape[1], MXU_DIM)]
    return jnp.concatenate(parts, axis=1) * (1.0 / HEAD_DIM)


def _in_proj_body(x_ref, gmix_ref, win_ref, gq_ref, gk_ref, cos_ref, sa_ref, sb_ref, bd_ref, cw_ref, gconv_ref,
                  q_ref, k_ref, v_ref, conv_ref, ubuf):
    ts = x_ref.shape[0]
    x = x_ref[...]
    ms = jnp.mean(x * x, axis=-1, keepdims=True)
    xn = (x * lax.rsqrt(ms + EPS) * gmix_ref[...]).astype(BF16)
    bd = bd_ref[...]

    def proj(c):
        return jnp.dot(xn, win_ref[:, c * ATTN_WIDTH:(c + 1) * ATTN_WIDTH], preferred_element_type=F32)

    reps = ATTN_WIDTH // LANES
    cos = jnp.concatenate([cos_ref[...]] * reps, axis=1)
    sa = jnp.concatenate([sa_ref[...]] * reps, axis=1)
    sb = jnp.concatenate([sb_ref[...]] * reps, axis=1)

    def norm_rope(t, g):
        tn = t * lax.rsqrt(_segment_mean_sq(t, bd) + EPS) * g
        up = pltpu.roll(tn, ATTN_WIDTH - ROPE_HALF, 1)
        dn = pltpu.roll(tn, ROPE_HALF, 1)
        return tn * cos + up * sa + dn * sb

    q_ref[...] = norm_rope(proj(0), gq_ref[...]).astype(BF16)
    k_ref[...] = norm_rope(proj(1), gk_ref[...]).astype(BF16)
    v_ref[...] = proj(2).astype(BF16)

    cb = proj(3)
    u = proj(4) * proj(5)

    @pl.when(pl.program_id(1) == 0)
    def _():
        ubuf[0:8, :] = jnp.zeros((8, CONV_WIDTH), F32)

    ubuf[8:8 + ts, :] = u
    u1 = ubuf[7:7 + ts, :]
    u2 = ubuf[6:6 + ts, :]
    y = cw_ref[0:1, :] * u2 + cw_ref[1:2, :] * u1 + cw_ref[2:3, :] * u
    ubuf[0:8, :] = ubuf[ts:ts + 8, :]
    conv = cb * y
    convn = conv * lax.rsqrt(_segment_mean_sq(conv, bd) + EPS) * gconv_ref[...]
    conv_ref[...] = convn.astype(BF16)


def _in_proj(x, g_mix, w_in, gq, gk, cos_t, sa_t, sb_t, bd, cw, g_conv):
    B, S, _ = x.shape
    ts = ROW_TILE
    row = lambda b, j: (b, j, 0)
    const2 = lambda b, j: (0, 0)
    tab = lambda b, j: (j, 0)
    out_sds = jax.ShapeDtypeStruct((B, S, ATTN_WIDTH), BF16)
    return pl.pallas_call(
        _in_proj_body,
        grid=(B, S // ts),
        in_specs=[
            pl.BlockSpec((None, ts, D_MODEL), row),
            pl.BlockSpec((1, D_MODEL), const2),
            pl.BlockSpec((D_MODEL, 6 * ATTN_WIDTH), const2),
            pl.BlockSpec((1, ATTN_WIDTH), const2),
            pl.BlockSpec((1, ATTN_WIDTH), const2),
            pl.BlockSpec((ts, LANES), tab),
            pl.BlockSpec((ts, LANES), tab),
            pl.BlockSpec((ts, LANES), tab),
            pl.BlockSpec((MXU_DIM, MXU_DIM), const2),
            pl.BlockSpec((8, CONV_WIDTH), const2),
            pl.BlockSpec((1, CONV_WIDTH), const2),
        ],
        out_specs=[pl.BlockSpec((None, ts, ATTN_WIDTH), row)] * 4,
        out_shape=[out_sds] * 4,
        scratch_shapes=[pltpu.VMEM((ts + 8, CONV_WIDTH), F32)],
        compiler_params=pltpu.CompilerParams(
            dimension_semantics=("arbitrary", "arbitrary"), vmem_limit_bytes=VMEM_LIMIT),
        name="in_proj",
    )(x, g_mix, w_in, gq, gk, cos_t, sa_t, sb_t, bd, cw, g_conv)


def _attn_body(q_ref, kp_ref, kc_ref, vp_ref, vc_ref, gat_ref, bd_ref, bias_ref, o_ref,
               qa, qb, kf, vf, anum, am, aden):
    tq = q_ref.shape[0]
    blk = ATTN_BLOCK
    first_tile = pl.program_id(2) == 0

    lane_t = lax.broadcasted_iota(jnp.int32, (tq, LANES), 1)
    qf = q_ref[...].astype(F32)
    qa[...] = jnp.where(lane_t < HEAD_DIM, qf, 0.0)
    qb[...] = jnp.where(lane_t < HEAD_DIM, 0.0, qf)
    kf[0:tq, :] = kp_ref[...].astype(F32)
    kf[tq:2 * tq, :] = kc_ref[...].astype(F32)
    vf[0:tq, :] = vp_ref[...].astype(F32)
    vf[tq:2 * tq, :] = vc_ref[...].astype(F32)

    head_a = lax.broadcasted_iota(jnp.int32, (blk, LANES), 1) < HEAD_DIM

    for d in DILATIONS:
        nblk = tq // (blk * d)
        shift = nblk.bit_length() - 1

        def block(idx, carry, d=d, nblk=nblk, shift=shift):
            r = idx >> shift
            n = idx & (nblk - 1)
            qstart = r + d * blk * n
            kstart = tq + d * blk * (n - 1) + r
            rows_q = pl.ds(qstart, blk, stride=d)
            rows_k = pl.ds(kstart, 2 * blk, stride=d)
            kw = kf[rows_k, :].astype(BF16)
            vw = vf[rows_k, :].astype(BF16)
            bias = bias_ref[jnp.where(jnp.logical_and(first_tile, n == 0), 1, 0)]

            def one_head(qsrc):
                qh = qsrc[rows_q, :].astype(BF16)
                s = lax.dot_general(qh, kw, (((1,), (1,)), ((), ())), preferred_element_type=F32) + bias
                m = jnp.max(s, axis=-1, keepdims=True)
                e = jnp.exp(s - m)
                den = jnp.sum(e, axis=-1, keepdims=True)
                pv = jnp.dot(e.astype(BF16), vw, preferred_element_type=F32)
                return pv, m, den

            pva, ma, dena = one_head(qa)
            pvb, mb, denb = one_head(qb)
            num_b = jnp.where(head_a, pva, pvb)
            m_b = jnp.where(head_a, ma, mb)
            den_b = jnp.where(head_a, dena, denb)
            if d == DILATIONS[0]:
                anum[rows_q, :] = num_b
                am[rows_q, :] = m_b
                aden[rows_q, :] = den_b
            else:
                m_old = am[rows_q, :]
                m_new = jnp.maximum(m_old, m_b)
                wa = jnp.exp(m_old - m_new)
                wb = jnp.exp(m_b - m_new)
                anum[rows_q, :] = wa * anum[rows_q, :] + wb * num_b
                aden[rows_q, :] = wa * aden[rows_q, :] + wb * den_b
                am[rows_q, :] = m_new
            return carry

        lax.fori_loop(0, tq // blk, block, 0)

    o = anum[...] / aden[...]
    o2 = (o * o).astype(BF16)
    msq = jnp.dot(o2, bd_ref[...], preferred_element_type=F32) * (1.0 / HEAD_DIM)
    o_ref[...] = (o * lax.rsqrt(msq + EPS) * gat_ref[...]).astype(BF16)


def _attention(q, k, v, g_attn, bd, bias):
    B, S, _ = q.shape
    tq = ATTN_TILE
    n_pairs = ATTN_WIDTH // LANES
    cur = lambda b, hp, j: (b, j, hp)
    prev = lambda b, hp, j: (b, jnp.maximum(j - 1, 0), hp)
    blk = (None, tq, LANES)
    return pl.pallas_call(
        _attn_body,
        grid=(B, n_pairs, S // tq),
        in_specs=[
            pl.BlockSpec(blk, cur),
            pl.BlockSpec(blk, prev),
            pl.BlockSpec(blk, cur),
            pl.BlockSpec(blk, prev),
            pl.BlockSpec(blk, cur),
            pl.BlockSpec((1, LANES), lambda b, hp, j: (0, hp)),
            pl.BlockSpec((LANES, LANES), lambda b, hp, j: (0, 0)),
            pl.BlockSpec((2, ATTN_BLOCK, 2 * ATTN_BLOCK), lambda b, hp, j: (0, 0, 0)),
        ],
        out_specs=pl.BlockSpec(blk, cur),
        out_shape=jax.ShapeDtypeStruct((B, S, ATTN_WIDTH), BF16),
        scratch_shapes=[
            pltpu.VMEM((tq, LANES), F32), pltpu.VMEM((tq, LANES), F32),
            pltpu.VMEM((2 * tq, LANES), F32), pltpu.VMEM((2 * tq, LANES), F32),
            pltpu.VMEM((tq, LANES), F32), pltpu.VMEM((tq, LANES), F32), pltpu.VMEM((tq, LANES), F32),
        ],
        compiler_params=pltpu.CompilerParams(
            dimension_semantics=("arbitrary", "arbitrary", "arbitrary"), vmem_limit_bytes=VMEM_LIMIT),
        name="attention",
    )(q, k, k, v, v, g_attn, bd, bias)


def _band_bias():
    qi = np.arange(ATTN_BLOCK)[:, None]
    ki = np.arange(2 * ATTN_BLOCK)[None, :]
    dist = qi + ATTN_BLOCK - ki
    band = (dist >= 0) & (dist <= N_BACK)
    no_prev = band & (ki >= ATTN_BLOCK)
    tab = np.stack([band, no_prev]).astype(np.float32)
    return jnp.asarray((1.0 - tab) * NEG_BIG, dtype=F32)


def _first_argmax(vals):
    best = vals[0]
    idx = jnp.zeros(best.shape, jnp.int32)
    for i in range(1, len(vals)):
        upd = vals[i] > best
        idx = jnp.where(upd, i, idx)
        best = jnp.where(upd, vals[i], best)
    return best, idx


def _out_route_body(x_ref, attn_ref, conv_ref, wo_ref, gffn_ref, wrh_ref, wrl_ref, br_ref, h1_ref, xg_ref):
    ts = x_ref.shape[0]
    h1 = (x_ref[...]
          + jnp.dot(attn_ref[...], wo_ref[0:ATTN_WIDTH, :], preferred_element_type=F32)
          + jnp.dot(conv_ref[...], wo_ref[ATTN_WIDTH:, :], preferred_element_type=F32))
    h1_ref[...] = h1
    ms = jnp.mean(h1 * h1, axis=-1, keepdims=True)
    xn = h1 * lax.rsqrt(ms + EPS) * gffn_ref[...]
    xg_ref[:, 0:D_MODEL] = xn

    xh = xn.astype(BF16)
    xl = (xn - xh.astype(F32)).astype(BF16)
    logits = (jnp.dot(xh, wrh_ref[...], preferred_element_type=F32)
              + jnp.dot(xl, wrh_ref[...], preferred_element_type=F32)
              + jnp.dot(xh, wrl_ref[...], preferred_element_type=F32)) + br_ref[...]
    lt = logits.T

    lg = [lt[ROUTER_GROUP_COL + i:ROUTER_GROUP_COL + i + 1, :] for i in range(N_GROUPS)]
    gbest, gi = _first_argmax(lg)
    sumexp = lg[0] * 0.0
    for i in range(N_GROUPS):
        sumexp = sumexp + jnp.exp(lg[i] - gbest)
    pg_top = 1.0 / sumexp

    sel = []
    for jx in range(EXPERTS_PER_GROUP):
        cand = [lt[ROUTER_EXPERT_COL + EXPERTS_PER_GROUP * g + jx:ROUTER_EXPERT_COL + EXPERTS_PER_GROUP * g + jx + 1, :]
                for g in range(N_GROUPS)]
        vj = cand[N_GROUPS - 1]
        for g in range(N_GROUPS - 2, -1, -1):
            vj = jnp.where(gi == g, cand[g], vj)
        sel.append(vj)
    b1, i1 = _first_argmax(sel)
    masked = [jnp.where(i1 == jx, -jnp.inf, sel[jx]) for jx in range(EXPERTS_PER_GROUP)]
    b2, i2 = _first_argmax(masked)
    e2 = jnp.exp(b2 - b1)
    t1 = 1.0 / (1.0 + e2)
    ga = pg_top * t1
    gb = pg_top * (e2 * t1)
    ea = EXPERTS_PER_GROUP * gi + i1
    eb = EXPERTS_PER_GROUP * gi + i2

    erow = lax.broadcasted_iota(jnp.int32, (GATE_LANES, ts), 0)
    gt = jnp.where(erow == ea, ga, 0.0) + jnp.where(erow == eb, gb, 0.0)
    xg_ref[:, D_MODEL:XG_WIDTH] = gt.T


def _out_route(x, attn, conv, w_out, g_ffn, wr_hi, wr_lo, b_r):
    B, S, _ = x.shape
    ts = ROW_TILE
    row = lambda b, j: (b, j, 0)
    const2 = lambda b, j: (0, 0)
    return pl.pallas_call(
        _out_route_body,
        grid=(B, S // ts),
        in_specs=[
            pl.BlockSpec((None, ts, D_MODEL), row),
            pl.BlockSpec((None, ts, ATTN_WIDTH), row),
            pl.BlockSpec((None, ts, CONV_WIDTH), row),
            pl.BlockSpec((D_MODEL, D_MODEL), const2),
            pl.BlockSpec((1, D_MODEL), const2),
            pl.BlockSpec((D_MODEL, GATE_LANES), const2),
            pl.BlockSpec((D_MODEL, GATE_LANES), const2),
            pl.BlockSpec((1, GATE_LANES), const2),
        ],
        out_specs=[pl.BlockSpec((None, ts, D_MODEL), row), pl.BlockSpec((None, ts, XG_WIDTH), row)],
        out_shape=[jax.ShapeDtypeStruct((B, S, D_MODEL), F32), jax.ShapeDtypeStruct((B, S, XG_WIDTH), F32)],
        compiler_params=pltpu.CompilerParams(
            dimension_semantics=("arbitrary", "arbitrary"), vmem_limit_bytes=VMEM_LIMIT),
        name="out_route",
    )(x, attn, conv, w_out, g_ffn, wr_hi, wr_lo, b_r)


def _moe_body(xg_ref, w1_ref, w3_ref, w2_ref, y_ref):
    e = pl.program_id(1)
    x = xg_ref[:, 0:D_MODEL].astype(BF16)
    gates = xg_ref[:, D_MODEL:XG_WIDTH]
    lane = lax.broadcasted_iota(jnp.int32, gates.shape, 1)
    gcol = jnp.sum(jnp.where(lane == e, gates, 0.0), axis=-1, keepdims=True)
    a = jnp.dot(x, w1_ref[...], preferred_element_type=F32)
    b = jnp.dot(x, w3_ref[...], preferred_element_type=F32)
    hdn = (a * (1.0 / (1.0 + jnp.exp(-a))) * b).astype(BF16)
    y = gcol * jnp.dot(hdn, w2_ref[...], preferred_element_type=F32)

    @pl.when(e == 0)
    def _():
        y_ref[...] = y

    @pl.when(e != 0)
    def _():
        y_ref[...] += y


def _moe(xg, w1, w3, w2):
    T = xg.shape[0]
    tm = MOE_TILE
    return pl.pallas_call(
        _moe_body,
        grid=(T // tm, N_EXPERTS),
        in_specs=[
            pl.BlockSpec((tm, XG_WIDTH), lambda i, e: (i, 0)),
            pl.BlockSpec((None, D_MODEL, D_FF), lambda i, e: (e, 0, 0)),
            pl.BlockSpec((None, D_MODEL, D_FF), lambda i, e: (e, 0, 0)),
            pl.BlockSpec((None, D_FF, D_MODEL), lambda i, e: (e, 0, 0)),
        ],
        out_specs=pl.BlockSpec((tm, D_MODEL), lambda i, e: (i, 0)),
        out_shape=jax.ShapeDtypeStruct((T, D_MODEL), F32),
        compiler_params=pltpu.CompilerParams(
            dimension_semantics=("arbitrary", "arbitrary"), vmem_limit_bytes=VMEM_LIMIT),
        name="moe",
    )(xg, w1, w3, w2)


def _ple_body(h1_ref, y_ref, p_ref, gple_ref, wg_ref, wp_ref, gpost_ref, o_ref):
    h2 = h1_ref[...] + y_ref[...]
    ms = jnp.mean(h2 * h2, axis=-1, keepdims=True)
    hn = (h2 * lax.rsqrt(ms + EPS) * gple_ref[...]).astype(BF16)
    z = jnp.dot(hn, wg_ref[...], preferred_element_type=F32)
    gate = 1.0 / (1.0 + jnp.exp(-z))
    pp = jnp.dot(p_ref[...].astype(BF16), wp_ref[...], preferred_element_type=F32)
    pms = jnp.mean(pp * pp, axis=-1, keepdims=True)
    pn = pp * lax.rsqrt(pms + EPS) * gpost_ref[...]
    o_ref[...] = h2 + gate * pn


def _ple(h1, y, p, g_ple, wg, wp, g_post):
    T = h1.shape[0]
    ts = ROW_TILE
    row = lambda i: (i, 0)
    const = lambda i: (0, 0)
    return pl.pallas_call(
        _ple_body,
        grid=(T // ts,),
        in_specs=[
            pl.BlockSpec((ts, D_MODEL), row),
            pl.BlockSpec((ts, D_MODEL), row),
            pl.BlockSpec((ts, PLE_DIM), row),
            pl.BlockSpec((1, D_MODEL), const),
            pl.BlockSpec((D_MODEL, D_MODEL), const),
            pl.BlockSpec((PLE_DIM, D_MODEL), const),
            pl.BlockSpec((1, D_MODEL), const),
        ],
        out_specs=pl.BlockSpec((ts, D_MODEL), row),
        out_shape=jax.ShapeDtypeStruct((T, D_MODEL), F32),
        compiler_params=pltpu.CompilerParams(
            dimension_semantics=("arbitrary",), vmem_limit_bytes=VMEM_LIMIT),
        name="ple",
    )(h1, y, p, g_ple, wg, wp, g_post)


def _rope_tables(S):
    pos = jnp.arange(S)
    inv = ROPE_THETA ** (-jnp.arange(0, ROPE_DIM, 2, dtype=F32) / ROPE_DIM)
    ang = pos.astype(F32)[:, None] * inv[None, :]
    cos, sin = jnp.cos(ang), jnp.sin(ang)
    ones = jnp.ones((S, HEAD_DIM - ROPE_DIM), F32)
    zeros_h = jnp.zeros((S, ROPE_HALF), F32)
    zeros_r = jnp.zeros((S, HEAD_DIM - ROPE_DIM), F32)
    cos_h = jnp.concatenate([cos, cos, ones], axis=1)
    sa_h = jnp.concatenate([-sin, zeros_h, zeros_r], axis=1)
    sb_h = jnp.concatenate([zeros_h, sin, zeros_r], axis=1)
    pair = lambda t: jnp.concatenate([t, t], axis=1)
    return pair(cos_h), pair(sa_h), pair(sb_h)


def _layer(h, p_i, g_mix, w_in, q_norm, k_norm, conv_w, g_attn_out, g_conv_out, w_out, g_ffn,
           w_rg, b_rg, w_re, b_re, w1, w3, w2, g_ple, w_ple_gate, w_ple_proj, g_ple_post):
    B, S, _ = h.shape
    T = B * S
    row = lambda g: g.reshape(1, -1).astype(F32)

    cos_t, sa_t, sb_t = _rope_tables(S)
    bd256 = _block_diag_ones(MXU_DIM, HEAD_DIM)
    bd128 = _block_diag_ones(LANES, HEAD_DIM)
    gq = row(jnp.tile(q_norm, N_HEADS) * (HEAD_DIM ** -0.5))
    gk = row(jnp.tile(k_norm, N_HEADS))
    cw = jnp.zeros((8, CONV_WIDTH), F32).at[0:CONV_K].set(conv_w)

    q, k, v, convn = _in_proj(h, row(g_mix), w_in.astype(BF16), gq, gk, cos_t, sa_t, sb_t, bd256, cw,
                              row(g_conv_out))
    attn = _attention(q, k, v, row(g_attn_out), bd128, _band_bias())

    w_r = jnp.zeros((D_MODEL, GATE_LANES), F32)
    w_r = w_r.at[:, ROUTER_GROUP_COL:ROUTER_GROUP_COL + N_GROUPS].set(w_rg)
    w_r = w_r.at[:, ROUTER_EXPERT_COL:ROUTER_EXPERT_COL + N_EXPERTS].set(w_re)
    b_r = jnp.zeros((1, GATE_LANES), F32)
    b_r = b_r.at[0, ROUTER_GROUP_COL:ROUTER_GROUP_COL + N_GROUPS].set(b_rg)
    b_r = b_r.at[0, ROUTER_EXPERT_COL:ROUTER_EXPERT_COL + N_EXPERTS].set(b_re)
    wr_hi = w_r.astype(BF16)
    wr_lo = (w_r - wr_hi.astype(F32)).astype(BF16)

    h1, xg = _out_route(h, attn, convn, w_out.astype(BF16), row(g_ffn), wr_hi, wr_lo, b_r)
    y = _moe(xg.reshape(T, XG_WIDTH), w1.astype(BF16), w3.astype(BF16), w2.astype(BF16))
    out = _ple(h1.reshape(T, D_MODEL), y, p_i.reshape(T, PLE_DIM), row(g_ple),
               w_ple_gate.astype(BF16), w_ple_proj.astype(BF16), row(g_ple_post))
    return out.reshape(B, S, D_MODEL)


def kernel(x, p, g_mix, w_in, q_norm, k_norm, conv_w, g_attn_out, g_conv_out, w_out, g_ffn, w_router_group, b_router_group, w_router_expert, b_router_expert, w1, w3, w2, g_ple, w_ple_gate, w_ple_proj, g_ple_post):
    h = x
    for i in range(p.shape[0]):
        h = _layer(h, p[i], g_mix[i], w_in[i], q_norm[i], k_norm[i], conv_w[i], g_attn_out[i], g_conv_out[i],
                   w_out[i], g_ffn[i], w_router_group[i], b_router_group[i], w_router_expert[i],
                   b_router_expert[i], w1[i], w3[i], w2[i], g_ple[i], w_ple_gate[i], w_ple_proj[i],
                   g_ple_post[i])
    return h
```

```python
import functools

import numpy as np
import jax
import jax.numpy as jnp
from jax import lax
from jax.experimental import pallas as pl
from jax.experimental.pallas import tpu as pltpu

F32 = jnp.float32
BF16 = jnp.bfloat16

D_MODEL = 1024
PLE_DIM = 256
HEAD_DIM = 64
N_HEADS = 8
ATTN_WIDTH = N_HEADS * HEAD_DIM
CONV_WIDTH = D_MODEL - ATTN_WIDTH
CONV_K = 3
DILATIONS = (1, 4, 16)
N_BACK = 128
ATTN_BLOCK = 128
ROPE_THETA = 500000.0
ROPE_DIM = HEAD_DIM // 4
ROPE_HALF = ROPE_DIM // 2
N_GROUPS = 4
EXPERTS_PER_GROUP = 4
N_EXPERTS = N_GROUPS * EXPERTS_PER_GROUP
D_FF = 512
EPS = 1e-6

LANES = 128
MXU_DIM = 256
NEG_BIG = -1e30

ROW_TILE = 512
ATTN_TILE = 2048
ATTN_UNROLL = 16
MOE_TILE = 1024
GATE_LANES = LANES
XG_WIDTH = D_MODEL + GATE_LANES
ROUTER_GROUP_COL = 0
ROUTER_EXPERT_COL = 8
VMEM_LIMIT = 48 * 1024 * 1024


def _block_diag_ones(n, seg):
    idx = np.arange(n) // seg
    return jnp.asarray((idx[:, None] == idx[None, :]).astype(np.float32), dtype=BF16)


def _segment_mean_sq(t, bd):
    t2 = (t * t).astype(BF16)
    parts = [jnp.dot(t2[:, i:i + MXU_DIM], bd, preferred_element_type=F32)
             for i in range(0, t.shape[1], MXU_DIM)]
    return jnp.concatenate(parts, axis=1) * (1.0 / HEAD_DIM)


def _in_proj_body(x_ref, gmix_ref, win_ref, gq_ref, gk_ref, cos_ref, sa_ref, sb_ref, bd_ref, cw_ref, gconv_ref,
                  q_ref, k_ref, v_ref, conv_ref, ubuf):
    ts = x_ref.shape[0]
    x = x_ref[...]
    ms = jnp.mean(x * x, axis=-1, keepdims=True)
    xn = (x * lax.rsqrt(ms + EPS) * gmix_ref[...]).astype(BF16)
    bd = bd_ref[...]

    def proj(c):
        return jnp.dot(xn, win_ref[:, c * ATTN_WIDTH:(c + 1) * ATTN_WIDTH], preferred_element_type=F32)

    reps = ATTN_WIDTH // LANES
    cos = jnp.concatenate([cos_ref[...]] * reps, axis=1)
    sa = jnp.concatenate([sa_ref[...]] * reps, axis=1)
    sb = jnp.concatenate([sb_ref[...]] * reps, axis=1)

    def norm_rope(t, g):
        tn = t * lax.rsqrt(_segment_mean_sq(t, bd) + EPS) * g
        up = pltpu.roll(tn, ATTN_WIDTH - ROPE_HALF, 1)
        dn = pltpu.roll(tn, ROPE_HALF, 1)
        return tn * cos + up * sa + dn * sb

    q_ref[...] = norm_rope(proj(0), gq_ref[...]).astype(BF16)
    k_ref[...] = norm_rope(proj(1), gk_ref[...]).astype(BF16)
    v_ref[...] = proj(2).astype(BF16)

    cb = proj(3)
    u = proj(4) * proj(5)

    @pl.when(pl.program_id(1) == 0)
    def _():
        ubuf[0:8, :] = jnp.zeros((8, CONV_WIDTH), F32)

    ubuf[8:8 + ts, :] = u
    u1 = ubuf[7:7 + ts, :]
    u2 = ubuf[6:6 + ts, :]
    y = cw_ref[0:1, :] * u2 + cw_ref[1:2, :] * u1 + cw_ref[2:3, :] * u
    ubuf[0:8, :] = ubuf[ts:ts + 8, :]
    conv = cb * y
    convn = conv * lax.rsqrt(_segment_mean_sq(conv, bd) + EPS) * gconv_ref[...]
    conv_ref[...] = convn.astype(BF16)


def _in_proj(x, g_mix, w_in, gq, gk, cos_t, sa_t, sb_t, bd, cw, g_conv):
    B, S, _ = x.shape
    ts = ROW_TILE
    row = lambda b, j: (b, j, 0)
    const2 = lambda b, j: (0, 0)
    tab = lambda b, j: (j, 0)
    out_sds = jax.ShapeDtypeStruct((B, S, ATTN_WIDTH), BF16)
    return pl.pallas_call(
        _in_proj_body,
        grid=(B, S // ts),
        in_specs=[
            pl.BlockSpec((None, ts, D_MODEL), row),
            pl.BlockSpec((1, D_MODEL), const2),
            pl.BlockSpec((D_MODEL, 6 * ATTN_WIDTH), const2),
            pl.BlockSpec((1, ATTN_WIDTH), const2),
            pl.BlockSpec((1, ATTN_WIDTH), const2),
            pl.BlockSpec((ts, LANES), tab),
            pl.BlockSpec((ts, LANES), tab),
            pl.BlockSpec((ts, LANES), tab),
            pl.BlockSpec((MXU_DIM, MXU_DIM), const2),
            pl.BlockSpec((8, CONV_WIDTH), const2),
            pl.BlockSpec((1, CONV_WIDTH), const2),
        ],
        out_specs=[pl.BlockSpec((None, ts, ATTN_WIDTH), row)] * 4,
        out_shape=[out_sds] * 4,
        scratch_shapes=[pltpu.VMEM((ts + 8, CONV_WIDTH), F32)],
        compiler_params=pltpu.CompilerParams(
            dimension_semantics=("arbitrary", "arbitrary"), vmem_limit_bytes=VMEM_LIMIT),
        name="in_proj",
    )(x, g_mix, w_in, gq, gk, cos_t, sa_t, sb_t, bd, cw, g_conv)


def _attn_body(q_ref, kp_ref, kc_ref, vp_ref, vc_ref, gat_ref, bd_ref, bias_ref, o_ref,
               qf, kf, vf, acc_a, acc_b, m_a, m_b):
    tq = q_ref.shape[0]
    blk = ATTN_BLOCK
    first_tile = pl.program_id(2) == 0

    qf[...] = q_ref[...].astype(F32)
    kf[0:tq, :] = kp_ref[...].astype(F32)
    kf[tq:2 * tq, :] = kc_ref[...].astype(F32)
    vf[0:tq, :] = vp_ref[...].astype(F32)
    vf[tq:2 * tq, :] = vc_ref[...].astype(F32)

    lane = lax.broadcasted_iota(jnp.int32, (1, LANES), 1)
    sel_a = (lane < HEAD_DIM).astype(F32).astype(BF16)
    sel_b = (lane >= HEAD_DIM).astype(F32).astype(BF16)

    order = DILATIONS[::-1]
    for d in order:
        nblk = tq // (blk * d)
        shift = nblk.bit_length() - 1
        first_visit = d == order[0]

        def block(idx, carry, d=d, nblk=nblk, shift=shift, first_visit=first_visit):
            r = idx >> shift
            n = idx & (nblk - 1)
            rows_q = pl.ds(r + d * blk * n, blk, stride=d)
            rows_k = pl.ds(tq + d * blk * (n - 1) + r, 2 * blk, stride=d)
            qw = qf[rows_q, :].astype(BF16)
            kw = kf[rows_k, :].astype(BF16)
            vw = vf[rows_k, :].astype(BF16)
            bias = bias_ref[jnp.where(jnp.logical_and(first_tile, n == 0), 1, 0)]

            def one_head(sel_q, sel_one, acc, mst):
                s = lax.dot_general(qw * sel_q, kw, (((1,), (1,)), ((), ())), preferred_element_type=F32) + bias
                mb = jnp.max(s, axis=-1, keepdims=True)
                if first_visit:
                    m_new = jnp.broadcast_to(mb, (blk, LANES))
                else:
                    m_old = mst[rows_q, :]
                    m_new = jnp.maximum(m_old, mb)
                e = jnp.exp(s - jnp.concatenate([m_new, m_new], axis=1))
                pv = jnp.dot(e.astype(BF16), vw * sel_q + sel_one, preferred_element_type=F32)
                if first_visit:
                    acc[rows_q, :] = pv
                else:
                    acc[rows_q, :] = acc[rows_q, :] * jnp.exp(m_old - m_new) + pv
                mst[rows_q, :] = m_new

            one_head(sel_a, sel_b, acc_a, m_a)
            one_head(sel_b, sel_a, acc_b, m_b)
            return carry

        lax.fori_loop(0, tq // blk, block, 0, unroll=ATTN_UNROLL)

    head_a = lax.broadcasted_iota(jnp.int32, (tq, LANES), 1) < HEAD_DIM
    aa = acc_a[...]
    ab = acc_b[...]
    num = jnp.where(head_a, aa, ab)
    den = pltpu.roll(jnp.where(head_a, ab, aa), HEAD_DIM, 1)
    o = num / den
    o2 = (o * o).astype(BF16)
    msq = jnp.dot(o2, bd_ref[...], preferred_element_type=F32) * (1.0 / HEAD_DIM)
    o_ref[...] = (o * lax.rsqrt(msq + EPS) * gat_ref[...]).astype(BF16)


def _attention(q, k, v, g_attn, bd, bias):
    B, S, _ = q.shape
    tq = ATTN_TILE
    n_pairs = ATTN_WIDTH // LANES
    cur = lambda b, hp, j: (b, j, hp)
    prev = lambda b, hp, j: (b, jnp.maximum(j - 1, 0), hp)
    blk = (None, tq, LANES)
    return pl.pallas_call(
        _attn_body,
        grid=(B, n_pairs, S // tq),
        in_specs=[
            pl.BlockSpec(blk, cur),
            pl.BlockSpec(blk, prev),
            pl.BlockSpec(blk, cur),
            pl.BlockSpec(blk, prev),
            pl.BlockSpec(blk, cur),
            pl.BlockSpec((1, LANES), lambda b, hp, j: (0, hp)),
            pl.BlockSpec((LANES, LANES), lambda b, hp, j: (0, 0)),
            pl.BlockSpec((2, ATTN_BLOCK, 2 * ATTN_BLOCK), lambda b, hp, j: (0, 0, 0)),
        ],
        out_specs=pl.BlockSpec(blk, cur),
        out_shape=jax.ShapeDtypeStruct((B, S, ATTN_WIDTH), BF16),
        scratch_shapes=[
            pltpu.VMEM((tq, LANES), F32),
            pltpu.VMEM((2 * tq, LANES), F32), pltpu.VMEM((2 * tq, LANES), F32),
            pltpu.VMEM((tq, LANES), F32), pltpu.VMEM((tq, LANES), F32),
            pltpu.VMEM((tq, LANES), F32), pltpu.VMEM((tq, LANES), F32),
        ],
        compiler_params=pltpu.CompilerParams(
            dimension_semantics=("arbitrary", "arbitrary", "arbitrary"), vmem_limit_bytes=VMEM_LIMIT),
        name="attention",
    )(q, k, k, v, v, g_attn, bd, bias)


def _band_bias():
    qi = np.arange(ATTN_BLOCK)[:, None]
    ki = np.arange(2 * ATTN_BLOCK)[None, :]
    dist = qi + ATTN_BLOCK - ki
    band = (dist >= 0) & (dist <= N_BACK)
    no_prev = band & (ki >= ATTN_BLOCK)
    tab = np.stack([band, no_prev]).astype(np.float32)
    return jnp.asarray((1.0 - tab) * NEG_BIG, dtype=F32)


def _first_argmax(vals):
    best = vals[0]
    idx = jnp.zeros(best.shape, jnp.int32)
    for i in range(1, len(vals)):
        upd = vals[i] > best
        idx = jnp.where(upd, i, idx)
        best = jnp.where(upd, vals[i], best)
    return best, idx


def _out_route_body(x_ref, attn_ref, conv_ref, wo_ref, gffn_ref, wrh_ref, wrl_ref, br_ref, h1_ref, xg_ref):
    ts = x_ref.shape[0]
    h1 = (x_ref[...]
          + jnp.dot(attn_ref[...], wo_ref[0:ATTN_WIDTH, :], preferred_element_type=F32)
          + jnp.dot(conv_ref[...], wo_ref[ATTN_WIDTH:, :], preferred_element_type=F32))
    h1_ref[...] = h1
    ms = jnp.mean(h1 * h1, axis=-1, keepdims=True)
    xn = h1 * lax.rsqrt(ms + EPS) * gffn_ref[...]
    xg_ref[:, 0:D_MODEL] = xn

    xh = xn.astype(BF16)
    xl = (xn - xh.astype(F32)).astype(BF16)
    logits = (jnp.dot(xh, wrh_ref[...], preferred_element_type=F32)
              + jnp.dot(xl, wrh_ref[...], preferred_element_type=F32)
              + jnp.dot(xh, wrl_ref[...], preferred_element_type=F32)) + br_ref[...]
    lt = logits.T

    lg = [lt[ROUTER_GROUP_COL + i:ROUTER_GROUP_COL + i + 1, :] for i in range(N_GROUPS)]
    gbest, gi = _first_argmax(lg)
    sumexp = lg[0] * 0.0
    for i in range(N_GROUPS):
        sumexp = sumexp + jnp.exp(lg[i] - gbest)
    pg_top = 1.0 / sumexp

    sel = []
    for jx in range(EXPERTS_PER_GROUP):
        cand = [lt[ROUTER_EXPERT_COL + EXPERTS_PER_GROUP * g + jx:ROUTER_EXPERT_COL + EXPERTS_PER_GROUP * g + jx + 1, :]
                for g in range(N_GROUPS)]
        vj = cand[N_GROUPS - 1]
        for g in range(N_GROUPS - 2, -1, -1):
            vj = jnp.where(gi == g, cand[g], vj)
        sel.append(vj)
    b1, i1 = _first_argmax(sel)
    masked = [jnp.where(i1 == jx, -jnp.inf, sel[jx]) for jx in range(EXPERTS_PER_GROUP)]
    b2, i2 = _first_argmax(masked)
    e2 = jnp.exp(b2 - b1)
    t1 = 1.0 / (1.0 + e2)
    ga = pg_top * t1
    gb = pg_top * (e2 * t1)
    ea = EXPERTS_PER_GROUP * gi + i1
    eb = EXPERTS_PER_GROUP * gi + i2

    erow = lax.broadcasted_iota(jnp.int32, (GATE_LANES, ts), 0)
    gt = jnp.where(erow == ea, ga, 0.0) + jnp.where(erow == eb, gb, 0.0)
    xg_ref[:, D_MODEL:XG_WIDTH] = gt.T


def _out_route(x, attn, conv, w_out, g_ffn, wr_hi, wr_lo, b_r):
    B, S, _ = x.shape
    ts = ROW_TILE
    row = lambda b, j: (b, j, 0)
    const2 = lambda b, j: (0, 0)
    return pl.pallas_call(
        _out_route_body,
        grid=(B, S // ts),
        in_specs=[
            pl.BlockSpec((None, ts, D_MODEL), row),
            pl.BlockSpec((None, ts, ATTN_WIDTH), row),
            pl.BlockSpec((None, ts, CONV_WIDTH), row),
            pl.BlockSpec((D_MODEL, D_MODEL), const2),
            pl.BlockSpec((1, D_MODEL), const2),
            pl.BlockSpec((D_MODEL, GATE_LANES), const2),
            pl.BlockSpec((D_MODEL, GATE_LANES), const2),
            pl.BlockSpec((1, GATE_LANES), const2),
        ],
        out_specs=[pl.BlockSpec((None, ts, D_MODEL), row), pl.BlockSpec((None, ts, XG_WIDTH), row)],
        out_shape=[jax.ShapeDtypeStruct((B, S, D_MODEL), F32), jax.ShapeDtypeStruct((B, S, XG_WIDTH), F32)],
        compiler_params=pltpu.CompilerParams(
            dimension_semantics=("arbitrary", "arbitrary"), vmem_limit_bytes=VMEM_LIMIT),
        name="out_route",
    )(x, attn, conv, w_out, g_ffn, wr_hi, wr_lo, b_r)


def _moe_body(xg_ref, w1_ref, w3_ref, w2_ref, y_ref):
    e = pl.program_id(1)
    x = xg_ref[:, 0:D_MODEL].astype(BF16)
    gates = xg_ref[:, D_MODEL:XG_WIDTH]
    lane = lax.broadcasted_iota(jnp.int32, gates.shape, 1)
    gcol = jnp.sum(jnp.where(lane == e, gates, 0.0), axis=-1, keepdims=True)
    a = jnp.dot(x, w1_ref[...], preferred_element_type=F32)
    b = jnp.dot(x, w3_ref[...], preferred_element_type=F32)
    hdn = (a * (1.0 / (1.0 + jnp.exp(-a))) * b).astype(BF16)
    y = gcol * jnp.dot(hdn, w2_ref[...], preferred_element_type=F32)

    @pl.when(e == 0)
    def _():
        y_ref[...] = y

    @pl.when(e != 0)
    def _():
        y_ref[...] += y


def _moe(xg, w1, w3, w2):
    T = xg.shape[0]
    tm = MOE_TILE
    return pl.pallas_call(
        _moe_body,
        grid=(T // tm, N_EXPERTS),
        in_specs=[
            pl.BlockSpec((tm, XG_WIDTH), lambda i, e: (i, 0)),
            pl.BlockSpec((None, D_MODEL, D_FF), lambda i, e: (e, 0, 0)),
            pl.BlockSpec((None, D_MODEL, D_FF), lambda i, e: (e, 0, 0)),
            pl.BlockSpec((None, D_FF, D_MODEL), lambda i, e: (e, 0, 0)),
        ],
        out_specs=pl.BlockSpec((tm, D_MODEL), lambda i, e: (i, 0)),
        out_shape=jax.ShapeDtypeStruct((T, D_MODEL), F32),
        compiler_params=pltpu.CompilerParams(
            dimension_semantics=("arbitrary", "arbitrary"), vmem_limit_bytes=VMEM_LIMIT),
        name="moe",
    )(xg, w1, w3, w2)


def _ple_body(h1_ref, y_ref, p_ref, gple_ref, wg_ref, wp_ref, gpost_ref, o_ref):
    h2 = h1_ref[...] + y_ref[...]
    ms = jnp.mean(h2 * h2, axis=-1, keepdims=True)
    hn = (h2 * lax.rsqrt(ms + EPS) * gple_ref[...]).astype(BF16)
    z = jnp.dot(hn, wg_ref[...], preferred_element_type=F32)
    gate = 1.0 / (1.0 + jnp.exp(-z))
    pp = jnp.dot(p_ref[...].astype(BF16), wp_ref[...], preferred_element_type=F32)
    pms = jnp.mean(pp * pp, axis=-1, keepdims=True)
    pn = pp * lax.rsqrt(pms + EPS) * gpost_ref[...]
    o_ref[...] = h2 + gate * pn


def _ple(h1, y, p, g_ple, wg, wp, g_post):
    T = h1.shape[0]
    ts = ROW_TILE
    row = lambda i: (i, 0)
    const = lambda i: (0, 0)
    return pl.pallas_call(
        _ple_body,
        grid=(T // ts,),
        in_specs=[
            pl.BlockSpec((ts, D_MODEL), row),
            pl.BlockSpec((ts, D_MODEL), row),
            pl.BlockSpec((ts, PLE_DIM), row),
            pl.BlockSpec((1, D_MODEL), const),
            pl.BlockSpec((D_MODEL, D_MODEL), const),
            pl.BlockSpec((PLE_DIM, D_MODEL), const),
            pl.BlockSpec((1, D_MODEL), const),
        ],
        out_specs=pl.BlockSpec((ts, D_MODEL), row),
        out_shape=jax.ShapeDtypeStruct((T, D_MODEL), F32),
        compiler_params=pltpu.CompilerParams(
            dimension_semantics=("arbitrary",), vmem_limit_bytes=VMEM_LIMIT),
        name="ple",
    )(h1, y, p, g_ple, wg, wp, g_post)


def _rope_tables(S):
    pos = jnp.arange(S)
    inv = ROPE_THETA ** (-jnp.arange(0, ROPE_DIM, 2, dtype=F32) / ROPE_DIM)
    ang = pos.astype(F32)[:, None] * inv[None, :]
    cos, sin = jnp.cos(ang), jnp.sin(ang)
    ones = jnp.ones((S, HEAD_DIM - ROPE_DIM), F32)
    zeros_h = jnp.zeros((S, ROPE_HALF), F32)
    zeros_r = jnp.zeros((S, HEAD_DIM - ROPE_DIM), F32)
    cos_h = jnp.concatenate([cos, cos, ones], axis=1)
    sa_h = jnp.concatenate([-sin, zeros_h, zeros_r], axis=1)
    sb_h = jnp.concatenate([zeros_h, sin, zeros_r], axis=1)
    pair = lambda t: jnp.concatenate([t, t], axis=1)
    return pair(cos_h), pair(sa_h), pair(sb_h)


def _layer(h, p_i, g_mix, w_in, q_norm, k_norm, conv_w, g_attn_out, g_conv_out, w_out, g_ffn,
           w_rg, b_rg, w_re, b_re, w1, w3, w2, g_ple, w_ple_gate, w_ple_proj, g_ple_post):
    B, S, _ = h.shape
    T = B * S
    row = lambda g: g.reshape(1, -1).astype(F32)

    cos_t, sa_t, sb_t = _rope_tables(S)
    bd256 = _block_diag_ones(MXU_DIM, HEAD_DIM)
    bd128 = _block_diag_ones(LANES, HEAD_DIM)
    gq = row(jnp.tile(q_norm, N_HEADS) * (HEAD_DIM ** -0.5))
    gk = row(jnp.tile(k_norm, N_HEADS))
    cw = jnp.zeros((8, CONV_WIDTH), F32).at[0:CONV_K].set(conv_w)

    q, k, v, convn = _in_proj(h, row(g_mix), w_in.astype(BF16), gq, gk, cos_t, sa_t, sb_t, bd256, cw,
                              row(g_conv_out))
    attn = _attention(q, k, v, row(g_attn_out), bd128, _band_bias())

    w_r = jnp.zeros((D_MODEL, GATE_LANES), F32)
    w_r = w_r.at[:, ROUTER_GROUP_COL:ROUTER_GROUP_COL + N_GROUPS].set(w_rg)
    w_r = w_r.at[:, ROUTER_EXPERT_COL:ROUTER_EXPERT_COL + N_EXPERTS].set(w_re)
    b_r = jnp.zeros((1, GATE_LANES), F32)
    b_r = b_r.at[0, ROUTER_GROUP_COL:ROUTER_GROUP_COL + N_GROUPS].set(b_rg)
    b_r = b_r.at[0, ROUTER_EXPERT_COL:ROUTER_EXPERT_COL + N_EXPERTS].set(b_re)
    wr_hi = w_r.astype(BF16)
    wr_lo = (w_r - wr_hi.astype(F32)).astype(BF16)

    h1, xg = _out_route(h, attn, convn, w_out.astype(BF16), row(g_ffn), wr_hi, wr_lo, b_r)
    y = _moe(xg.reshape(T, XG_WIDTH), w1.astype(BF16), w3.astype(BF16), w2.astype(BF16))
    out = _ple(h1.reshape(T, D_MODEL), y, p_i.reshape(T, PLE_DIM), row(g_ple),
               w_ple_gate.astype(BF16), w_ple_proj.astype(BF16), row(g_ple_post))
    return out.reshape(B, S, D_MODEL)


def kernel(x, p, g_mix, w_in, q_norm, k_norm, conv_w, g_attn_out, g_conv_out, w_out, g_ffn, w_router_group, b_router_group, w_router_expert, b_router_expert, w1, w3, w2, g_ple, w_ple_gate, w_ple_proj, g_ple_post):
    h = x
    for i in range(p.shape[0]):
        h = _layer(h, p[i], g_mix[i], w_in[i], q_norm[i], k_norm[i], conv_w[i], g_attn_out[i], g_conv_out[i],
                   w_out[i], g_ffn[i], w_router_group[i], b_router_group[i], w_router_expert[i],
                   b_router_expert[i], w1[i], w3[i], w2[i], g_ple[i], w_ple_gate[i], w_ple_proj[i],
                   g_ple_post[i])
    return h
```

```python
import functools

import numpy as np
import jax
import jax.numpy as jnp
from jax import lax
from jax.experimental import pallas as pl
from jax.experimental.pallas import tpu as pltpu

F32 = jnp.float32
BF16 = jnp.bfloat16

D_MODEL = 1024
PLE_DIM = 256
HEAD_DIM = 64
N_HEADS = 8
ATTN_WIDTH = N_HEADS * HEAD_DIM
CONV_WIDTH = D_MODEL - ATTN_WIDTH
CONV_K = 3
DILATIONS = (1, 4, 16)
N_BACK = 128
ATTN_BLOCK = 128
ROPE_THETA = 500000.0
ROPE_DIM = HEAD_DIM // 4
ROPE_HALF = ROPE_DIM // 2
N_GROUPS = 4
EXPERTS_PER_GROUP = 4
N_EXPERTS = N_GROUPS * EXPERTS_PER_GROUP
D_FF = 512
EPS = 1e-6

LANES = 128
MXU_DIM = 256
NEG_BIG = -1e30

ROW_TILE = 512
ATTN_TILE = 2048
ATTN_UNROLL = 16
MOE_TILE = 256
GATE_LANES = LANES
XG_WIDTH = D_MODEL + GATE_LANES
GATE_LO_LANE = 0
GATE_HI_LANE = 1
EXPERT_PAIRS = tuple((a, b) for a in range(EXPERTS_PER_GROUP) for b in range(a + 1, EXPERTS_PER_GROUP))
N_CLASSES = N_GROUPS * len(EXPERT_PAIRS)
CLASS_ROWS = 32
SCHED_LANES = 256
ROUTER_GROUP_COL = 0
ROUTER_EXPERT_COL = 8
VMEM_LIMIT = 48 * 1024 * 1024


def _block_diag_ones(n, seg):
    idx = np.arange(n) // seg
    return jnp.asarray((idx[:, None] == idx[None, :]).astype(np.float32), dtype=BF16)


def _segment_mean_sq(t, bd):
    t2 = (t * t).astype(BF16)
    parts = [jnp.dot(t2[:, i:i + MXU_DIM], bd, preferred_element_type=F32)
             for i in range(0, t.shape[1], MXU_DIM)]
    return jnp.concatenate(parts, axis=1) * (1.0 / HEAD_DIM)


def _in_proj_body(x_ref, gmix_ref, win_ref, gq_ref, gk_ref, cos_ref, sa_ref, sb_ref, bd_ref, cw_ref, gconv_ref,
                  q_ref, k_ref, v_ref, conv_ref, ubuf):
    ts = x_ref.shape[0]
    x = x_ref[...]
    ms = jnp.mean(x * x, axis=-1, keepdims=True)
    xn = (x * lax.rsqrt(ms + EPS) * gmix_ref[...]).astype(BF16)
    bd = bd_ref[...]

    def proj(c):
        return jnp.dot(xn, win_ref[:, c * ATTN_WIDTH:(c + 1) * ATTN_WIDTH], preferred_element_type=F32)

    reps = ATTN_WIDTH // LANES
    cos = jnp.concatenate([cos_ref[...]] * reps, axis=1)
    sa = jnp.concatenate([sa_ref[...]] * reps, axis=1)
    sb = jnp.concatenate([sb_ref[...]] * reps, axis=1)

    def norm_rope(t, g):
        tn = t * lax.rsqrt(_segment_mean_sq(t, bd) + EPS) * g
        up = pltpu.roll(tn, ATTN_WIDTH - ROPE_HALF, 1)
        dn = pltpu.roll(tn, ROPE_HALF, 1)
        return tn * cos + up * sa + dn * sb

    q_ref[...] = norm_rope(proj(0), gq_ref[...]).astype(BF16)
    k_ref[...] = norm_rope(proj(1), gk_ref[...]).astype(BF16)
    v_ref[...] = proj(2).astype(BF16)

    cb = proj(3)
    u = proj(4) * proj(5)

    @pl.when(pl.program_id(1) == 0)
    def _():
        ubuf[0:8, :] = jnp.zeros((8, CONV_WIDTH), F32)

    ubuf[8:8 + ts, :] = u
    u1 = ubuf[7:7 + ts, :]
    u2 = ubuf[6:6 + ts, :]
    y = cw_ref[0:1, :] * u2 + cw_ref[1:2, :] * u1 + cw_ref[2:3, :] * u
    ubuf[0:8, :] = ubuf[ts:ts + 8, :]
    conv = cb * y
    convn = conv * lax.rsqrt(_segment_mean_sq(conv, bd) + EPS) * gconv_ref[...]
    conv_ref[...] = convn.astype(BF16)


def _in_proj(x, g_mix, w_in, gq, gk, cos_t, sa_t, sb_t, bd, cw, g_conv):
    B, S, _ = x.shape
    ts = ROW_TILE
    row = lambda b, j: (b, j, 0)
    const2 = lambda b, j: (0, 0)
    tab = lambda b, j: (j, 0)
    out_sds = jax.ShapeDtypeStruct((B, S, ATTN_WIDTH), BF16)
    return pl.pallas_call(
        _in_proj_body,
        grid=(B, S // ts),
        in_specs=[
            pl.BlockSpec((None, ts, D_MODEL), row),
            pl.BlockSpec((1, D_MODEL), const2),
            pl.BlockSpec((D_MODEL, 6 * ATTN_WIDTH), const2),
            pl.BlockSpec((1, ATTN_WIDTH), const2),
            pl.BlockSpec((1, ATTN_WIDTH), const2),
            pl.BlockSpec((ts, LANES), tab),
            pl.BlockSpec((ts, LANES), tab),
            pl.BlockSpec((ts, LANES), tab),
            pl.BlockSpec((MXU_DIM, MXU_DIM), const2),
            pl.BlockSpec((8, CONV_WIDTH), const2),
            pl.BlockSpec((1, CONV_WIDTH), const2),
        ],
        out_specs=[pl.BlockSpec((None, ts, ATTN_WIDTH), row)] * 4,
        out_shape=[out_sds] * 4,
        scratch_shapes=[pltpu.VMEM((ts + 8, CONV_WIDTH), F32)],
        compiler_params=pltpu.CompilerParams(
            dimension_semantics=("arbitrary", "arbitrary"), vmem_limit_bytes=VMEM_LIMIT),
        name="in_proj",
    )(x, g_mix, w_in, gq, gk, cos_t, sa_t, sb_t, bd, cw, g_conv)


def _attn_body(q_ref, kp_ref, kc_ref, vp_ref, vc_ref, gat_ref, bd_ref, bias_ref, o_ref,
               qf, kf, vf, acc_a, acc_b, m_a, m_b):
    tq = q_ref.shape[0]
    blk = ATTN_BLOCK
    first_tile = pl.program_id(2) == 0

    qf[...] = q_ref[...].astype(F32)
    kf[0:tq, :] = kp_ref[...].astype(F32)
    kf[tq:2 * tq, :] = kc_ref[...].astype(F32)
    vf[0:tq, :] = vp_ref[...].astype(F32)
    vf[tq:2 * tq, :] = vc_ref[...].astype(F32)

    lane = lax.broadcasted_iota(jnp.int32, (1, LANES), 1)
    sel_a = (lane < HEAD_DIM).astype(F32).astype(BF16)
    sel_b = (lane >= HEAD_DIM).astype(F32).astype(BF16)

    order = DILATIONS[::-1]
    for d in order:
        nblk = tq // (blk * d)
        shift = nblk.bit_length() - 1
        first_visit = d == order[0]

        def block(idx, carry, d=d, nblk=nblk, shift=shift, first_visit=first_visit):
            r = idx >> shift
            n = idx & (nblk - 1)
            rows_q = pl.ds(r + d * blk * n, blk, stride=d)
            rows_k = pl.ds(tq + d * blk * (n - 1) + r, 2 * blk, stride=d)
            qw = qf[rows_q, :].astype(BF16)
            kw = kf[rows_k, :].astype(BF16)
            vw = vf[rows_k, :].astype(BF16)
            bias = bias_ref[jnp.where(jnp.logical_and(first_tile, n == 0), 1, 0)]

            def one_head(sel_q, sel_one, acc, mst):
                s = lax.dot_general(qw * sel_q, kw, (((1,), (1,)), ((), ())), preferred_element_type=F32) + bias
                mb = jnp.max(s, axis=-1, keepdims=True)
                if first_visit:
                    m_new = jnp.broadcast_to(mb, (blk, LANES))
                else:
                    m_old = mst[rows_q, :]
                    m_new = jnp.maximum(m_old, mb)
                e = jnp.exp(s - jnp.concatenate([m_new, m_new], axis=1))
                pv = jnp.dot(e.astype(BF16), vw * sel_q + sel_one, preferred_element_type=F32)
                if first_visit:
                    acc[rows_q, :] = pv
                else:
                    acc[rows_q, :] = acc[rows_q, :] * jnp.exp(m_old - m_new) + pv
                mst[rows_q, :] = m_new

            one_head(sel_a, sel_b, acc_a, m_a)
            one_head(sel_b, sel_a, acc_b, m_b)
            return carry

        lax.fori_loop(0, tq // blk, block, 0, unroll=ATTN_UNROLL)

    head_a = lax.broadcasted_iota(jnp.int32, (tq, LANES), 1) < HEAD_DIM
    aa = acc_a[...]
    ab = acc_b[...]
    num = jnp.where(head_a, aa, ab)
    den = pltpu.roll(jnp.where(head_a, ab, aa), HEAD_DIM, 1)
    o = num / den
    o2 = (o * o).astype(BF16)
    msq = jnp.dot(o2, bd_ref[...], preferred_element_type=F32) * (1.0 / HEAD_DIM)
    o_ref[...] = (o * lax.rsqrt(msq + EPS) * gat_ref[...]).astype(BF16)


def _attention(q, k, v, g_attn, bd, bias):
    B, S, _ = q.shape
    tq = ATTN_TILE
    n_pairs = ATTN_WIDTH // LANES
    cur = lambda b, hp, j: (b, j, hp)
    prev = lambda b, hp, j: (b, jnp.maximum(j - 1, 0), hp)
    blk = (None, tq, LANES)
    return pl.pallas_call(
        _attn_body,
        grid=(B, n_pairs, S // tq),
        in_specs=[
            pl.BlockSpec(blk, cur),
            pl.BlockSpec(blk, prev),
            pl.BlockSpec(blk, cur),
            pl.BlockSpec(blk, prev),
            pl.BlockSpec(blk, cur),
            pl.BlockSpec((1, LANES), lambda b, hp, j: (0, hp)),
            pl.BlockSpec((LANES, LANES), lambda b, hp, j: (0, 0)),
            pl.BlockSpec((2, ATTN_BLOCK, 2 * ATTN_BLOCK), lambda b, hp, j: (0, 0, 0)),
        ],
        out_specs=pl.BlockSpec(blk, cur),
        out_shape=jax.ShapeDtypeStruct((B, S, ATTN_WIDTH), BF16),
        scratch_shapes=[
            pltpu.VMEM((tq, LANES), F32),
            pltpu.VMEM((2 * tq, LANES), F32), pltpu.VMEM((2 * tq, LANES), F32),
            pltpu.VMEM((tq, LANES), F32), pltpu.VMEM((tq, LANES), F32),
            pltpu.VMEM((tq, LANES), F32), pltpu.VMEM((tq, LANES), F32),
        ],
        compiler_params=pltpu.CompilerParams(
            dimension_semantics=("arbitrary", "arbitrary", "arbitrary"), vmem_limit_bytes=VMEM_LIMIT),
        name="attention",
    )(q, k, k, v, v, g_attn, bd, bias)


def _band_bias():
    qi = np.arange(ATTN_BLOCK)[:, None]
    ki = np.arange(2 * ATTN_BLOCK)[None, :]
    dist = qi + ATTN_BLOCK - ki
    band = (dist >= 0) & (dist <= N_BACK)
    no_prev = band & (ki >= ATTN_BLOCK)
    tab = np.stack([band, no_prev]).astype(np.float32)
    return jnp.asarray((1.0 - tab) * NEG_BIG, dtype=F32)


def _first_argmax(vals):
    best = vals[0]
    idx = jnp.zeros(best.shape, jnp.int32)
    for i in range(1, len(vals)):
        upd = vals[i] > best
        idx = jnp.where(upd, i, idx)
        best = jnp.where(upd, vals[i], best)
    return best, idx


def _out_route_body(x_ref, attn_ref, conv_ref, wo_ref, gffn_ref, wrh_ref, wrl_ref, br_ref, upper_ref,
                    h1_ref, xg_ref, rt_ref, cnt_ref, cnt):
    ts = x_ref.shape[0]
    h1 = (x_ref[...]
          + jnp.dot(attn_ref[...], wo_ref[0:ATTN_WIDTH, :], preferred_element_type=F32)
          + jnp.dot(conv_ref[...], wo_ref[ATTN_WIDTH:, :], preferred_element_type=F32))
    h1_ref[...] = h1
    ms = jnp.mean(h1 * h1, axis=-1, keepdims=True)
    xn = h1 * lax.rsqrt(ms + EPS) * gffn_ref[...]
    xg_ref[:, 0:D_MODEL] = xn

    xh = xn.astype(BF16)
    xl = (xn - xh.astype(F32)).astype(BF16)
    logits = (jnp.dot(xh, wrh_ref[...], preferred_element_type=F32)
              + jnp.dot(xl, wrh_ref[...], preferred_element_type=F32)
              + jnp.dot(xh, wrl_ref[...], preferred_element_type=F32)) + br_ref[...]
    lt = logits.T

    lg = [lt[ROUTER_GROUP_COL + i:ROUTER_GROUP_COL + i + 1, :] for i in range(N_GROUPS)]
    gbest, gi = _first_argmax(lg)
    sumexp = lg[0] * 0.0
    for i in range(N_GROUPS):
        sumexp = sumexp + jnp.exp(lg[i] - gbest)
    pg_top = 1.0 / sumexp

    sel = []
    for jx in range(EXPERTS_PER_GROUP):
        cand = [lt[ROUTER_EXPERT_COL + EXPERTS_PER_GROUP * g + jx:ROUTER_EXPERT_COL + EXPERTS_PER_GROUP * g + jx + 1, :]
                for g in range(N_GROUPS)]
        vj = cand[N_GROUPS - 1]
        for g in range(N_GROUPS - 2, -1, -1):
            vj = jnp.where(gi == g, cand[g], vj)
        sel.append(vj)
    b1, i1 = _first_argmax(sel)
    masked = [jnp.where(i1 == jx, -jnp.inf, sel[jx]) for jx in range(EXPERTS_PER_GROUP)]
    b2, i2 = _first_argmax(masked)
    e2 = jnp.exp(b2 - b1)
    t1 = 1.0 / (1.0 + e2)
    ga = pg_top * t1
    gb = pg_top * (e2 * t1)
    lo = jnp.minimum(i1, i2)
    hi = jnp.maximum(i1, i2)
    g_lo = jnp.where(i1 < i2, ga, gb)
    g_hi = jnp.where(i1 < i2, gb, ga)
    pair = hi - lo - 1
    for a in range(1, EXPERTS_PER_GROUP - 1):
        pair = pair + jnp.where(lo >= a, EXPERT_PAIRS.index((a, a + 1)) - EXPERT_PAIRS.index((a - 1, a)), 0)
    cls = gi * len(EXPERT_PAIRS) + pair

    erow = lax.broadcasted_iota(jnp.int32, (GATE_LANES, ts), 0)
    gt = jnp.where(erow == GATE_LO_LANE, g_lo, 0.0) + jnp.where(erow == GATE_HI_LANE, g_hi, 0.0)
    xg_ref[:, D_MODEL:XG_WIDTH] = gt.T

    @pl.when(jnp.logical_and(pl.program_id(0) == 0, pl.program_id(1) == 0))
    def _():
        cnt[...] = jnp.zeros(cnt.shape, F32)

    crow = lax.broadcasted_iota(jnp.int32, (CLASS_ROWS, ts), 0)
    oh = (crow == cls).astype(F32)
    before = jnp.dot(oh.astype(BF16), upper_ref[...], preferred_element_type=F32)
    rank = jnp.sum(oh * (before + cnt[:, 0:1]), axis=0, keepdims=True)
    cnt[...] = cnt[...] + jnp.sum(oh, axis=1, keepdims=True)
    cnt_ref[...] = cnt[...]
    r8 = lax.broadcasted_iota(jnp.int32, (8, ts), 0)
    rt_ref[...] = jnp.where(r8 == 0, cls, jnp.where(r8 == 1, rank.astype(jnp.int32), 0))


def _out_route(x, attn, conv, w_out, g_ffn, wr_hi, wr_lo, b_r):
    B, S, _ = x.shape
    ts = ROW_TILE
    row = lambda b, j: (b, j, 0)
    const2 = lambda b, j: (0, 0)
    tok = np.arange(ts)
    upper = jnp.asarray((tok[:, None] < tok[None, :]).astype(np.float32), dtype=BF16)
    return pl.pallas_call(
        _out_route_body,
        grid=(B, S // ts),
        in_specs=[
            pl.BlockSpec((None, ts, D_MODEL), row),
            pl.BlockSpec((None, ts, ATTN_WIDTH), row),
            pl.BlockSpec((None, ts, CONV_WIDTH), row),
            pl.BlockSpec((D_MODEL, D_MODEL), const2),
            pl.BlockSpec((1, D_MODEL), const2),
            pl.BlockSpec((D_MODEL, GATE_LANES), const2),
            pl.BlockSpec((D_MODEL, GATE_LANES), const2),
            pl.BlockSpec((1, GATE_LANES), const2),
            pl.BlockSpec((ts, ts), const2),
        ],
        out_specs=[
            pl.BlockSpec((None, ts, D_MODEL), row),
            pl.BlockSpec((None, ts, XG_WIDTH), row),
            pl.BlockSpec((None, None, 8, ts), lambda b, j: (b, j, 0, 0)),
            pl.BlockSpec((CLASS_ROWS, LANES), const2),
        ],
        out_shape=[
            jax.ShapeDtypeStruct((B, S, D_MODEL), F32),
            jax.ShapeDtypeStruct((B, S, XG_WIDTH), F32),
            jax.ShapeDtypeStruct((B, S // ts, 8, ts), jnp.int32),
            jax.ShapeDtypeStruct((CLASS_ROWS, LANES), F32),
        ],
        scratch_shapes=[pltpu.VMEM((CLASS_ROWS, LANES), F32)],
        compiler_params=pltpu.CompilerParams(
            dimension_semantics=("arbitrary", "arbitrary"), vmem_limit_bytes=VMEM_LIMIT),
        name="out_route",
    )(x, attn, conv, w_out, g_ffn, wr_hi, wr_lo, b_r, upper)


def _route_plan_body(cnt_ref, rt_ref, elo_ref, ehi_ref, pos_ref, sched_ref, cinfo_ref):
    n_tiles_tok, _, ts = pos_ref.shape
    shift = MOE_TILE.bit_length() - 1
    ntile = (cnt_ref[...].astype(jnp.int32) + (MOE_TILE - 1)) >> shift
    crow = lax.broadcasted_iota(jnp.int32, (CLASS_ROWS, LANES), 0)
    incl = ntile
    step = 1
    while step < CLASS_ROWS:
        incl = incl + jnp.where(crow >= step, pltpu.roll(incl, step, 0), 0)
        step *= 2
    tstart = incl - ntile
    total = incl[CLASS_ROWS - 1:CLASS_ROWS, 0:1]

    row_base = jnp.broadcast_to(tstart[:, 0:1] * MOE_TILE, (CLASS_ROWS, ts))
    crow_t = lax.broadcasted_iota(jnp.int32, (CLASS_ROWS, ts), 0)

    def token_tile(i, carry):
        cls = rt_ref[i, 0:1, :]
        rank = rt_ref[i, 1:2, :]
        base = jnp.sum(jnp.where(crow_t == cls, row_base, 0), axis=0, keepdims=True)
        pos_ref[i] = base + rank
        return carry

    lax.fori_loop(0, n_tiles_tok, token_tile, 0)

    lane_j = lax.broadcasted_iota(jnp.int32, (CLASS_ROWS, SCHED_LANES), 1)
    start_b = jnp.broadcast_to(tstart[:, 0:1], (CLASS_ROWS, SCHED_LANES))
    ntile_b = jnp.broadcast_to(ntile[:, 0:1], (CLASS_ROWS, SCHED_LANES))
    member = jnp.logical_and(lane_j >= start_b, lane_j < start_b + ntile_b)
    pick = lambda tab: jnp.sum(jnp.where(member, jnp.broadcast_to(tab[:, 0:1], member.shape), 0),
                               axis=0, keepdims=True)
    valid = jnp.sum(member.astype(jnp.int32), axis=0, keepdims=True)
    elo = pick(elo_ref[...])
    ehi = pick(ehi_ref[...])
    j1 = lane_j[0:1, :]
    last = total - 1
    at_last = j1 == last
    elo_last = jnp.sum(jnp.where(at_last, elo, 0), axis=1, keepdims=True)
    ehi_last = jnp.sum(jnp.where(at_last, ehi, 0), axis=1, keepdims=True)
    in_use = valid > 0
    elo = jnp.where(in_use, elo, elo_last)
    ehi = jnp.where(in_use, ehi, ehi_last)
    blk = jnp.minimum(j1, last)
    r8 = lax.broadcasted_iota(jnp.int32, (8, SCHED_LANES), 0)
    sched_ref[...] = jnp.where(r8 == 0, elo, jnp.where(r8 == 1, ehi, jnp.where(r8 == 2, valid,
                               jnp.where(r8 == 3, blk, 0))))
    lane_c = lax.broadcasted_iota(jnp.int32, (CLASS_ROWS, LANES), 1)
    cinfo_ref[...] = jnp.where(lane_c == 0, tstart + ntile - 1,
                               jnp.where(lane_c == 1, (ntile > 0).astype(jnp.int32), total))


def _route_plan(cnt, rt):
    n_tiles_tok, _, ts = rt.shape
    elo_tab = np.zeros((CLASS_ROWS, LANES), np.int32)
    ehi_tab = np.zeros((CLASS_ROWS, LANES), np.int32)
    for g in range(N_GROUPS):
        for p, (a, b) in enumerate(EXPERT_PAIRS):
            elo_tab[g * len(EXPERT_PAIRS) + p, :] = g * EXPERTS_PER_GROUP + a
            ehi_tab[g * len(EXPERT_PAIRS) + p, :] = g * EXPERTS_PER_GROUP + b
    full = lambda shape: pl.BlockSpec(shape, lambda i: (0,) * len(shape))
    return pl.pallas_call(
        _route_plan_body,
        grid=(1,),
        in_specs=[full((CLASS_ROWS, LANES)), full(rt.shape), full((CLASS_ROWS, LANES)), full((CLASS_ROWS, LANES))],
        out_specs=[full((n_tiles_tok, 1, ts)), full((8, SCHED_LANES)), full((CLASS_ROWS, LANES))],
        out_shape=[
            jax.ShapeDtypeStruct((n_tiles_tok, 1, ts), jnp.int32),
            jax.ShapeDtypeStruct((8, SCHED_LANES), jnp.int32),
            jax.ShapeDtypeStruct((CLASS_ROWS, LANES), jnp.int32),
        ],
        compiler_params=pltpu.CompilerParams(dimension_semantics=("arbitrary",)),
        name="route_plan",
    )(cnt, rt, jnp.asarray(elo_tab), jnp.asarray(ehi_tab))


def _permute_rows_body(*refs, scatter, zero_pad_tiles):
    if zero_pad_tiles:
        last_ref, has_ref, total_ref, pos_ref, src_hbm, dst_hbm, zbuf, sem, zsem = refs
    else:
        pos_ref, src_hbm, dst_hbm, sem = refs
    i = pl.program_id(0)
    n = pl.num_programs(0)
    ts = pos_ref.shape[-1]
    slot = i % 2

    if zero_pad_tiles:
        n_dst_tiles = dst_hbm.shape[0] // MOE_TILE

        def pad_copies():
            for c in range(N_CLASSES):
                yield has_ref[c] != 0, last_ref[c]
                yield total_ref[0] + c < n_dst_tiles, total_ref[0] + c

        def tile_copy(tile):
            return pltpu.make_async_copy(zbuf, dst_hbm.at[pl.ds(tile * MOE_TILE, MOE_TILE)], zsem)

        @pl.when(i == 0)
        def _():
            zbuf[...] = jnp.zeros(zbuf.shape, zbuf.dtype)
            for cond, tile in pad_copies():
                @pl.when(cond)
                def _():
                    tile_copy(tile).start()
            for cond, tile in pad_copies():
                @pl.when(cond)
                def _():
                    tile_copy(tile).wait()

    def row_copy(t, sl):
        p = pos_ref[0, t]
        tok = i * ts + t
        s_row, d_row = (tok, p) if scatter else (p, tok)
        return pltpu.make_async_copy(src_hbm.at[pl.ds(s_row, 1)], dst_hbm.at[pl.ds(d_row, 1)], sem.at[sl])

    def issue(t, carry):
        row_copy(t, slot).start()
        return carry

    lax.fori_loop(0, ts, issue, 0, unroll=8)

    def drain(sl):
        def one(t, carry):
            pltpu.make_async_copy(src_hbm.at[pl.ds(0, 1)], dst_hbm.at[pl.ds(0, 1)], sem.at[sl]).wait()
            return carry
        lax.fori_loop(0, ts, one, 0, unroll=8)

    @pl.when(i > 0)
    def _():
        drain(1 - slot)

    @pl.when(i == n - 1)
    def _():
        drain(slot)


def _permute_rows(pos, src, n_dst_rows, scatter, cinfo=None):
    n_tiles_tok, _, ts = pos.shape
    width = src.shape[1]
    zero_pad_tiles = cinfo is not None
    smem_pos = pl.BlockSpec((None, 1, ts), lambda i, *_: (i, 0, 0), memory_space=pltpu.SMEM)
    any_spec = pl.BlockSpec(memory_space=pl.ANY)
    scratch = [pltpu.SemaphoreType.DMA((2,))]
    args = [pos, src]
    n_prefetch = 0
    if zero_pad_tiles:
        scratch = [pltpu.VMEM((MOE_TILE, width), src.dtype), pltpu.SemaphoreType.DMA((2,)), pltpu.SemaphoreType.DMA(())]
        args = [cinfo[:, 0], cinfo[:, 1], cinfo[0:1, 2], pos, src]
        n_prefetch = 3
    return pl.pallas_call(
        functools.partial(_permute_rows_body, scatter=scatter, zero_pad_tiles=zero_pad_tiles),
        grid_spec=pltpu.PrefetchScalarGridSpec(
            num_scalar_prefetch=n_prefetch,
            grid=(n_tiles_tok,),
            in_specs=[smem_pos, any_spec],
            out_specs=any_spec,
            scratch_shapes=scratch,
        ),
        out_shape=jax.ShapeDtypeStruct((n_dst_rows, width), src.dtype),
        compiler_params=pltpu.CompilerParams(dimension_semantics=("arbitrary",)),
        name="scatter_rows" if scatter else "gather_rows",
    )(*args)


def _moe_body(elo_ref, ehi_ref, valid_ref, blk_ref, xs_ref, w1l_ref, w3l_ref, w2l_ref, w1h_ref, w3h_ref, w2h_ref,
              ys_ref):
    @pl.when(valid_ref[pl.program_id(0)] != 0)
    def _():
        x = xs_ref[:, 0:D_MODEL].astype(BF16)

        def expert(w1_ref, w3_ref, w2_ref):
            a = jnp.dot(x, w1_ref[...], preferred_element_type=F32)
            b = jnp.dot(x, w3_ref[...], preferred_element_type=F32)
            hdn = (a * (1.0 / (1.0 + jnp.exp(-a))) * b).astype(BF16)
            return jnp.dot(hdn, w2_ref[...], preferred_element_type=F32)

        g_lo = xs_ref[:, D_MODEL + GATE_LO_LANE:D_MODEL + GATE_LO_LANE + 1]
        g_hi = xs_ref[:, D_MODEL + GATE_HI_LANE:D_MODEL + GATE_HI_LANE + 1]
        ys_ref[...] = (g_lo * expert(w1l_ref, w3l_ref, w2l_ref)
                       + g_hi * expert(w1h_ref, w3h_ref, w2h_ref))

    @pl.when(valid_ref[pl.program_id(0)] == 0)
    def _():
        ys_ref[...] = jnp.zeros(ys_ref.shape, F32)


def _moe(sched, xs, w1, w3, w2):
    n_tiles = xs.shape[0] // MOE_TILE
    rows = lambda j, elo, ehi, valid, blk: (blk[j], 0)
    w_lo = lambda j, elo, ehi, valid, blk: (elo[j], 0, 0)
    w_hi = lambda j, elo, ehi, valid, blk: (ehi[j], 0, 0)
    up = (None, D_MODEL, D_FF)
    down = (None, D_FF, D_MODEL)
    return pl.pallas_call(
        _moe_body,
        grid_spec=pltpu.PrefetchScalarGridSpec(
            num_scalar_prefetch=4,
            grid=(n_tiles,),
            in_specs=[
                pl.BlockSpec((MOE_TILE, XG_WIDTH), rows),
                pl.BlockSpec(up, w_lo), pl.BlockSpec(up, w_lo), pl.BlockSpec(down, w_lo),
                pl.BlockSpec(up, w_hi), pl.BlockSpec(up, w_hi), pl.BlockSpec(down, w_hi),
            ],
            out_specs=pl.BlockSpec((MOE_TILE, D_MODEL), lambda j, *_: (j, 0)),
        ),
        out_shape=jax.ShapeDtypeStruct((xs.shape[0], D_MODEL), F32),
        compiler_params=pltpu.CompilerParams(dimension_semantics=("arbitrary",), vmem_limit_bytes=VMEM_LIMIT),
        name="moe",
    )(sched[0, :n_tiles], sched[1, :n_tiles], sched[2, :n_tiles], sched[3, :n_tiles], xs, w1, w3, w2, w1, w3, w2)


def _ple_body(h1_ref, y_ref, p_ref, gple_ref, wg_ref, wp_ref, gpost_ref, o_ref):
    h2 = h1_ref[...] + y_ref[...]
    ms = jnp.mean(h2 * h2, axis=-1, keepdims=True)
    hn = (h2 * lax.rsqrt(ms + EPS) * gple_ref[...]).astype(BF16)
    z = jnp.dot(hn, wg_ref[...], preferred_element_type=F32)
    gate = 1.0 / (1.0 + jnp.exp(-z))
    pp = jnp.dot(p_ref[...].astype(BF16), wp_ref[...], preferred_element_type=F32)
    pms = jnp.mean(pp * pp, axis=-1, keepdims=True)
    pn = pp * lax.rsqrt(pms + EPS) * gpost_ref[...]
    o_ref[...] = h2 + gate * pn


def _ple(h1, y, p, g_ple, wg, wp, g_post):
    T = h1.shape[0]
    ts = ROW_TILE
    row = lambda i: (i, 0)
    const = lambda i: (0, 0)
    return pl.pallas_call(
        _ple_body,
        grid=(T // ts,),
        in_specs=[
            pl.BlockSpec((ts, D_MODEL), row),
            pl.BlockSpec((ts, D_MODEL), row),
            pl.BlockSpec((ts, PLE_DIM), row),
            pl.BlockSpec((1, D_MODEL), const),
            pl.BlockSpec((D_MODEL, D_MODEL), const),
            pl.BlockSpec((PLE_DIM, D_MODEL), const),
            pl.BlockSpec((1, D_MODEL), const),
        ],
        out_specs=pl.BlockSpec((ts, D_MODEL), row),
        out_shape=jax.ShapeDtypeStruct((T, D_MODEL), F32),
        compiler_params=pltpu.CompilerParams(
            dimension_semantics=("arbitrary",), vmem_limit_bytes=VMEM_LIMIT),
        name="ple",
    )(h1, y, p, g_ple, wg, wp, g_post)


def _rope_tables(S):
    pos = jnp.arange(S)
    inv = ROPE_THETA ** (-jnp.arange(0, ROPE_DIM, 2, dtype=F32) / ROPE_DIM)
    ang = pos.astype(F32)[:, None] * inv[None, :]
    cos, sin = jnp.cos(ang), jnp.sin(ang)
    ones = jnp.ones((S, HEAD_DIM - ROPE_DIM), F32)
    zeros_h = jnp.zeros((S, ROPE_HALF), F32)
    zeros_r = jnp.zeros((S, HEAD_DIM - ROPE_DIM), F32)
    cos_h = jnp.concatenate([cos, cos, ones], axis=1)
    sa_h = jnp.concatenate([-sin, zeros_h, zeros_r], axis=1)
    sb_h = jnp.concatenate([zeros_h, sin, zeros_r], axis=1)
    pair = lambda t: jnp.concatenate([t, t], axis=1)
    return pair(cos_h), pair(sa_h), pair(sb_h)


def _layer(h, p_i, g_mix, w_in, q_norm, k_norm, conv_w, g_attn_out, g_conv_out, w_out, g_ffn,
           w_rg, b_rg, w_re, b_re, w1, w3, w2, g_ple, w_ple_gate, w_ple_proj, g_ple_post):
    B, S, _ = h.shape
    T = B * S
    row = lambda g: g.reshape(1, -1).astype(F32)

    cos_t, sa_t, sb_t = _rope_tables(S)
    bd256 = _block_diag_ones(MXU_DIM, HEAD_DIM)
    bd128 = _block_diag_ones(LANES, HEAD_DIM)
    gq = row(jnp.tile(q_norm, N_HEADS) * (HEAD_DIM ** -0.5))
    gk = row(jnp.tile(k_norm, N_HEADS))
    cw = jnp.zeros((8, CONV_WIDTH), F32).at[0:CONV_K].set(conv_w)

    q, k, v, convn = _in_proj(h, row(g_mix), w_in.astype(BF16), gq, gk, cos_t, sa_t, sb_t, bd256, cw,
                              row(g_conv_out))
    attn = _attention(q, k, v, row(g_attn_out), bd128, _band_bias())

    w_r = jnp.zeros((D_MODEL, GATE_LANES), F32)
    w_r = w_r.at[:, ROUTER_GROUP_COL:ROUTER_GROUP_COL + N_GROUPS].set(w_rg)
    w_r = w_r.at[:, ROUTER_EXPERT_COL:ROUTER_EXPERT_COL + N_EXPERTS].set(w_re)
    b_r = jnp.zeros((1, GATE_LANES), F32)
    b_r = b_r.at[0, ROUTER_GROUP_COL:ROUTER_GROUP_COL + N_GROUPS].set(b_rg)
    b_r = b_r.at[0, ROUTER_EXPERT_COL:ROUTER_EXPERT_COL + N_EXPERTS].set(b_re)
    wr_hi = w_r.astype(BF16)
    wr_lo = (w_r - wr_hi.astype(F32)).astype(BF16)

    h1, xg, rt, cnt = _out_route(h, attn, convn, w_out.astype(BF16), row(g_ffn), wr_hi, wr_lo, b_r)
    n_sorted_tiles = T // MOE_TILE + N_CLASSES
    assert n_sorted_tiles <= SCHED_LANES
    pos, sched, cinfo = _route_plan(cnt, rt.reshape(T // ROW_TILE, 8, ROW_TILE))
    xs = _permute_rows(pos, xg.reshape(T, XG_WIDTH), n_sorted_tiles * MOE_TILE, scatter=True, cinfo=cinfo)
    ys = _moe(sched, xs, w1.astype(BF16), w3.astype(BF16), w2.astype(BF16))
    y = _permute_rows(pos, ys, T, scatter=False)
    out = _ple(h1.reshape(T, D_MODEL), y, p_i.reshape(T, PLE_DIM), row(g_ple),
               w_ple_gate.astype(BF16), w_ple_proj.astype(BF16), row(g_ple_post))
    return out.reshape(B, S, D_MODEL)


def kernel(x, p, g_mix, w_in, q_norm, k_norm, conv_w, g_attn_out, g_conv_out, w_out, g_ffn, w_router_group, b_router_group, w_router_expert, b_router_expert, w1, w3, w2, g_ple, w_ple_gate, w_ple_proj, g_ple_post):
    h = x
    for i in range(p.shape[0]):
        h = _layer(h, p[i], g_mix[i], w_in[i], q_norm[i], k_norm[i], conv_w[i], g_attn_out[i], g_conv_out[i],
                   w_out[i], g_ffn[i], w_router_group[i], b_router_group[i], w_router_expert[i],
                   b_router_expert[i], w1[i], w3[i], w2[i], g_ple[i], w_ple_gate[i], w_ple_proj[i],
                   g_ple_post[i])
    return h
```

```python
import functools

import numpy as np
import jax
import jax.numpy as jnp
from jax import lax
from jax.experimental import pallas as pl
from jax.experimental.pallas import tpu as pltpu

F32 = jnp.float32
BF16 = jnp.bfloat16

D_MODEL = 1024
PLE_DIM = 256
HEAD_DIM = 64
N_HEADS = 8
ATTN_WIDTH = N_HEADS * HEAD_DIM
CONV_WIDTH = D_MODEL - ATTN_WIDTH
CONV_K = 3
DILATIONS = (1, 4, 16)
N_BACK = 128
ATTN_BLOCK = 128
ROPE_THETA = 500000.0
ROPE_DIM = HEAD_DIM // 4
ROPE_HALF = ROPE_DIM // 2
N_GROUPS = 4
EXPERTS_PER_GROUP = 4
N_EXPERTS = N_GROUPS * EXPERTS_PER_GROUP
D_FF = 512
EPS = 1e-6

LANES = 128
MXU_DIM = 256
NEG_BIG = -1e30

ROW_TILE = 512
ATTN_TILE = 2048
ATTN_UNROLL = 16
MOE_TILE = 256
GATE_LANES = LANES
XG_WIDTH = D_MODEL + GATE_LANES
GATE_LO_LANE = 0
GATE_HI_LANE = 1
EXPERT_PAIRS = tuple((a, b) for a in range(EXPERTS_PER_GROUP) for b in range(a + 1, EXPERTS_PER_GROUP))
N_CLASSES = N_GROUPS * len(EXPERT_PAIRS)
CLASS_ROWS = 32
SCHED_LANES = 256
ROUTER_GROUP_COL = 0
ROUTER_EXPERT_COL = 8
VMEM_LIMIT = 48 * 1024 * 1024


def _block_diag_ones(n, seg):
    idx = np.arange(n) // seg
    return jnp.asarray((idx[:, None] == idx[None, :]).astype(np.float32), dtype=BF16)


def _segment_mean_sq(t, bd):
    t2 = (t * t).astype(BF16)
    parts = [jnp.dot(t2[:, i:i + MXU_DIM], bd, preferred_element_type=F32)
             for i in range(0, t.shape[1], MXU_DIM)]
    return jnp.concatenate(parts, axis=1) * (1.0 / HEAD_DIM)


def _in_proj_body(x_ref, gmix_ref, win_ref, gq_ref, gk_ref, cos_ref, sa_ref, sb_ref, bd_ref, cw_ref, gconv_ref,
                  q_ref, k_ref, v_ref, conv_ref, ubuf):
    ts = x_ref.shape[0]
    x = x_ref[...]
    ms = jnp.mean(x * x, axis=-1, keepdims=True)
    xn = (x * lax.rsqrt(ms + EPS) * gmix_ref[...]).astype(BF16)
    bd = bd_ref[...]

    def proj(c):
        return jnp.dot(xn, win_ref[:, c * ATTN_WIDTH:(c + 1) * ATTN_WIDTH], preferred_element_type=F32)

    reps = ATTN_WIDTH // LANES
    cos = jnp.concatenate([cos_ref[...]] * reps, axis=1)
    sa = jnp.concatenate([sa_ref[...]] * reps, axis=1)
    sb = jnp.concatenate([sb_ref[...]] * reps, axis=1)

    def norm_rope(t, g):
        tn = t * lax.rsqrt(_segment_mean_sq(t, bd) + EPS) * g
        up = pltpu.roll(tn, ATTN_WIDTH - ROPE_HALF, 1)
        dn = pltpu.roll(tn, ROPE_HALF, 1)
        return tn * cos + up * sa + dn * sb

    q_ref[...] = norm_rope(proj(0), gq_ref[...]).astype(BF16)
    k_ref[...] = norm_rope(proj(1), gk_ref[...]).astype(BF16)
    v_ref[...] = proj(2).astype(BF16)

    cb = proj(3)
    u = proj(4) * proj(5)

    @pl.when(pl.program_id(1) == 0)
    def _():
        ubuf[0:8, :] = jnp.zeros((8, CONV_WIDTH), F32)

    ubuf[8:8 + ts, :] = u
    u1 = ubuf[7:7 + ts, :]
    u2 = ubuf[6:6 + ts, :]
    y = cw_ref[0:1, :] * u2 + cw_ref[1:2, :] * u1 + cw_ref[2:3, :] * u
    ubuf[0:8, :] = ubuf[ts:ts + 8, :]
    conv = cb * y
    convn = conv * lax.rsqrt(_segment_mean_sq(conv, bd) + EPS) * gconv_ref[...]
    conv_ref[...] = convn.astype(BF16)


def _in_proj(x, g_mix, w_in, gq, gk, cos_t, sa_t, sb_t, bd, cw, g_conv):
    B, S, _ = x.shape
    ts = ROW_TILE
    row = lambda b, j: (b, j, 0)
    const2 = lambda b, j: (0, 0)
    tab = lambda b, j: (j, 0)
    out_sds = jax.ShapeDtypeStruct((B, S, ATTN_WIDTH), BF16)
    return pl.pallas_call(
        _in_proj_body,
        grid=(B, S // ts),
        in_specs=[
            pl.BlockSpec((None, ts, D_MODEL), row),
            pl.BlockSpec((1, D_MODEL), const2),
            pl.BlockSpec((D_MODEL, 6 * ATTN_WIDTH), const2),
            pl.BlockSpec((1, ATTN_WIDTH), const2),
            pl.BlockSpec((1, ATTN_WIDTH), const2),
            pl.BlockSpec((ts, LANES), tab),
            pl.BlockSpec((ts, LANES), tab),
            pl.BlockSpec((ts, LANES), tab),
            pl.BlockSpec((MXU_DIM, MXU_DIM), const2),
            pl.BlockSpec((8, CONV_WIDTH), const2),
            pl.BlockSpec((1, CONV_WIDTH), const2),
        ],
        out_specs=[pl.BlockSpec((None, ts, ATTN_WIDTH), row)] * 4,
        out_shape=[out_sds] * 4,
        scratch_shapes=[pltpu.VMEM((ts + 8, CONV_WIDTH), F32)],
        compiler_params=pltpu.CompilerParams(
            dimension_semantics=("arbitrary", "arbitrary"), vmem_limit_bytes=VMEM_LIMIT),
        name="in_proj",
    )(x, g_mix, w_in, gq, gk, cos_t, sa_t, sb_t, bd, cw, g_conv)


def _attn_body(q_ref, kp_ref, kc_ref, vp_ref, vc_ref, gat_ref, bd_ref, bias_ref, o_ref,
               qf, kf, vf, acc_a, acc_b, m_a, m_b):
    tq = q_ref.shape[0]
    blk = ATTN_BLOCK
    first_tile = pl.program_id(2) == 0

    qf[...] = q_ref[...].astype(F32)
    kf[0:tq, :] = kp_ref[...].astype(F32)
    kf[tq:2 * tq, :] = kc_ref[...].astype(F32)
    vf[0:tq, :] = vp_ref[...].astype(F32)
    vf[tq:2 * tq, :] = vc_ref[...].astype(F32)

    lane = lax.broadcasted_iota(jnp.int32, (1, LANES), 1)
    sel_a = (lane < HEAD_DIM).astype(F32).astype(BF16)
    sel_b = (lane >= HEAD_DIM).astype(F32).astype(BF16)

    order = DILATIONS[::-1]
    for d in order:
        nblk = tq // (blk * d)
        shift = nblk.bit_length() - 1
        first_visit = d == order[0]

        def block(idx, carry, d=d, nblk=nblk, shift=shift, first_visit=first_visit):
            r = idx >> shift
            n = idx & (nblk - 1)
            rows_q = pl.ds(r + d * blk * n, blk, stride=d)
            rows_k = pl.ds(tq + d * blk * (n - 1) + r, 2 * blk, stride=d)
            qw = qf[rows_q, :].astype(BF16)
            kw = kf[rows_k, :].astype(BF16)
            vw = vf[rows_k, :].astype(BF16)
            bias = bias_ref[jnp.where(jnp.logical_and(first_tile, n == 0), 1, 0)]

            def one_head(sel_q, sel_one, acc, mst):
                s = lax.dot_general(qw * sel_q, kw, (((1,), (1,)), ((), ())), preferred_element_type=F32) + bias
                mb = jnp.max(s, axis=-1, keepdims=True)
                if first_visit:
                    m_new = jnp.broadcast_to(mb, (blk, LANES))
                else:
                    m_old = mst[rows_q, :]
                    m_new = jnp.maximum(m_old, mb)
                e = jnp.exp(s - jnp.concatenate([m_new, m_new], axis=1))
                pv = jnp.dot(e.astype(BF16), vw * sel_q + sel_one, preferred_element_type=F32)
                if first_visit:
                    acc[rows_q, :] = pv
                else:
                    acc[rows_q, :] = acc[rows_q, :] * jnp.exp(m_old - m_new) + pv
                mst[rows_q, :] = m_new

            one_head(sel_a, sel_b, acc_a, m_a)
            one_head(sel_b, sel_a, acc_b, m_b)
            return carry

        lax.fori_loop(0, tq // blk, block, 0, unroll=ATTN_UNROLL)

    head_a = lax.broadcasted_iota(jnp.int32, (tq, LANES), 1) < HEAD_DIM
    aa = acc_a[...]
    ab = acc_b[...]
    num = jnp.where(head_a, aa, ab)
    den = pltpu.roll(jnp.where(head_a, ab, aa), HEAD_DIM, 1)
    o = num / den
    o2 = (o * o).astype(BF16)
    msq = jnp.dot(o2, bd_ref[...], preferred_element_type=F32) * (1.0 / HEAD_DIM)
    o_ref[...] = (o * lax.rsqrt(msq + EPS) * gat_ref[...]).astype(BF16)


def _attention(q, k, v, g_attn, bd, bias):
    B, S, _ = q.shape
    tq = ATTN_TILE
    n_pairs = ATTN_WIDTH // LANES
    cur = lambda b, hp, j: (b, j, hp)
    prev = lambda b, hp, j: (b, jnp.maximum(j - 1, 0), hp)
    blk = (None, tq, LANES)
    return pl.pallas_call(
        _attn_body,
        grid=(B, n_pairs, S // tq),
        in_specs=[
            pl.BlockSpec(blk, cur),
            pl.BlockSpec(blk, prev),
            pl.BlockSpec(blk, cur),
            pl.BlockSpec(blk, prev),
            pl.BlockSpec(blk, cur),
            pl.BlockSpec((1, LANES), lambda b, hp, j: (0, hp)),
            pl.BlockSpec((LANES, LANES), lambda b, hp, j: (0, 0)),
            pl.BlockSpec((2, ATTN_BLOCK, 2 * ATTN_BLOCK), lambda b, hp, j: (0, 0, 0)),
        ],
        out_specs=pl.BlockSpec(blk, cur),
        out_shape=jax.ShapeDtypeStruct((B, S, ATTN_WIDTH), BF16),
        scratch_shapes=[
            pltpu.VMEM((tq, LANES), F32),
            pltpu.VMEM((2 * tq, LANES), F32), pltpu.VMEM((2 * tq, LANES), F32),
            pltpu.VMEM((tq, LANES), F32), pltpu.VMEM((tq, LANES), F32),
            pltpu.VMEM((tq, LANES), F32), pltpu.VMEM((tq, LANES), F32),
        ],
        compiler_params=pltpu.CompilerParams(
            dimension_semantics=("arbitrary", "arbitrary", "arbitrary"), vmem_limit_bytes=VMEM_LIMIT),
        name="attention",
    )(q, k, k, v, v, g_attn, bd, bias)


def _band_bias():
    qi = np.arange(ATTN_BLOCK)[:, None]
    ki = np.arange(2 * ATTN_BLOCK)[None, :]
    dist = qi + ATTN_BLOCK - ki
    band = (dist >= 0) & (dist <= N_BACK)
    no_prev = band & (ki >= ATTN_BLOCK)
    tab = np.stack([band, no_prev]).astype(np.float32)
    return jnp.asarray((1.0 - tab) * NEG_BIG, dtype=F32)


def _first_argmax(vals):
    best = vals[0]
    idx = jnp.zeros(best.shape, jnp.int32)
    for i in range(1, len(vals)):
        upd = vals[i] > best
        idx = jnp.where(upd, i, idx)
        best = jnp.where(upd, vals[i], best)
    return best, idx


def _out_route_body(x_ref, attn_ref, conv_ref, wo_ref, gffn_ref, wrh_ref, wrl_ref, br_ref, upper_ref,
                    h1_ref, rt_ref, rg_ref, cnt_ref, cnt):
    ts = x_ref.shape[0]
    h1 = (x_ref[...]
          + jnp.dot(attn_ref[...], wo_ref[0:ATTN_WIDTH, :], preferred_element_type=F32)
          + jnp.dot(conv_ref[...], wo_ref[ATTN_WIDTH:, :], preferred_element_type=F32))
    h1_ref[...] = h1
    ms = jnp.mean(h1 * h1, axis=-1, keepdims=True)
    xn = h1 * lax.rsqrt(ms + EPS) * gffn_ref[...]

    xh = xn.astype(BF16)
    xl = (xn - xh.astype(F32)).astype(BF16)
    logits = (jnp.dot(xh, wrh_ref[...], preferred_element_type=F32)
              + jnp.dot(xl, wrh_ref[...], preferred_element_type=F32)
              + jnp.dot(xh, wrl_ref[...], preferred_element_type=F32)) + br_ref[...]
    lt = logits.T

    lg = [lt[ROUTER_GROUP_COL + i:ROUTER_GROUP_COL + i + 1, :] for i in range(N_GROUPS)]
    gbest, gi = _first_argmax(lg)
    sumexp = lg[0] * 0.0
    for i in range(N_GROUPS):
        sumexp = sumexp + jnp.exp(lg[i] - gbest)
    pg_top = 1.0 / sumexp

    sel = []
    for jx in range(EXPERTS_PER_GROUP):
        cand = [lt[ROUTER_EXPERT_COL + EXPERTS_PER_GROUP * g + jx:ROUTER_EXPERT_COL + EXPERTS_PER_GROUP * g + jx + 1, :]
                for g in range(N_GROUPS)]
        vj = cand[N_GROUPS - 1]
        for g in range(N_GROUPS - 2, -1, -1):
            vj = jnp.where(gi == g, cand[g], vj)
        sel.append(vj)
    b1, i1 = _first_argmax(sel)
    masked = [jnp.where(i1 == jx, -jnp.inf, sel[jx]) for jx in range(EXPERTS_PER_GROUP)]
    b2, i2 = _first_argmax(masked)
    e2 = jnp.exp(b2 - b1)
    t1 = 1.0 / (1.0 + e2)
    ga = pg_top * t1
    gb = pg_top * (e2 * t1)
    lo = jnp.minimum(i1, i2)
    hi = jnp.maximum(i1, i2)
    g_lo = jnp.where(i1 < i2, ga, gb)
    g_hi = jnp.where(i1 < i2, gb, ga)
    pair = hi - lo - 1
    for a in range(1, EXPERTS_PER_GROUP - 1):
        pair = pair + jnp.where(lo >= a, EXPERT_PAIRS.index((a, a + 1)) - EXPERT_PAIRS.index((a - 1, a)), 0)
    cls = gi * len(EXPERT_PAIRS) + pair

    r8 = lax.broadcasted_iota(jnp.int32, (8, ts), 0)
    rg_ref[...] = jnp.where(r8 == 0, g_lo, jnp.where(r8 == 1, g_hi, 0.0))

    @pl.when(jnp.logical_and(pl.program_id(0) == 0, pl.program_id(1) == 0))
    def _():
        cnt[...] = jnp.zeros(cnt.shape, F32)

    crow = lax.broadcasted_iota(jnp.int32, (CLASS_ROWS, ts), 0)
    oh = (crow == cls).astype(F32)
    before = jnp.dot(oh.astype(BF16), upper_ref[...], preferred_element_type=F32)
    rank = jnp.sum(oh * (before + cnt[:, 0:1]), axis=0, keepdims=True)
    cnt[...] = cnt[...] + jnp.sum(oh, axis=1, keepdims=True)
    cnt_ref[...] = cnt[...]
    rt_ref[...] = jnp.where(r8 == 0, cls, jnp.where(r8 == 1, rank.astype(jnp.int32), 0))


def _out_route(x, attn, conv, w_out, g_ffn, wr_hi, wr_lo, b_r):
    B, S, _ = x.shape
    ts = ROW_TILE
    row = lambda b, j: (b, j, 0)
    const2 = lambda b, j: (0, 0)
    tok = np.arange(ts)
    upper = jnp.asarray((tok[:, None] < tok[None, :]).astype(np.float32), dtype=BF16)
    return pl.pallas_call(
        _out_route_body,
        grid=(B, S // ts),
        in_specs=[
            pl.BlockSpec((None, ts, D_MODEL), row),
            pl.BlockSpec((None, ts, ATTN_WIDTH), row),
            pl.BlockSpec((None, ts, CONV_WIDTH), row),
            pl.BlockSpec((D_MODEL, D_MODEL), const2),
            pl.BlockSpec((1, D_MODEL), const2),
            pl.BlockSpec((D_MODEL, GATE_LANES), const2),
            pl.BlockSpec((D_MODEL, GATE_LANES), const2),
            pl.BlockSpec((1, GATE_LANES), const2),
            pl.BlockSpec((ts, ts), const2),
        ],
        out_specs=[
            pl.BlockSpec((None, ts, D_MODEL), row),
            pl.BlockSpec((None, None, 8, ts), lambda b, j: (b, j, 0, 0)),
            pl.BlockSpec((None, None, 8, ts), lambda b, j: (b, j, 0, 0)),
            pl.BlockSpec((CLASS_ROWS, LANES), const2),
        ],
        out_shape=[
            jax.ShapeDtypeStruct((B, S, D_MODEL), F32),
            jax.ShapeDtypeStruct((B, S // ts, 8, ts), jnp.int32),
            jax.ShapeDtypeStruct((B, S // ts, 8, ts), F32),
            jax.ShapeDtypeStruct((CLASS_ROWS, LANES), F32),
        ],
        scratch_shapes=[pltpu.VMEM((CLASS_ROWS, LANES), F32)],
        compiler_params=pltpu.CompilerParams(
            dimension_semantics=("arbitrary", "arbitrary"), vmem_limit_bytes=VMEM_LIMIT),
        name="out_route",
    )(x, attn, conv, w_out, g_ffn, wr_hi, wr_lo, b_r, upper)


def _route_plan_body(cnt_ref, rt_ref, elo_ref, ehi_ref, pos_ref, sched_ref, cinfo_ref):
    n_tiles_tok, _, ts = pos_ref.shape
    shift = MOE_TILE.bit_length() - 1
    ntile = (cnt_ref[...].astype(jnp.int32) + (MOE_TILE - 1)) >> shift
    crow = lax.broadcasted_iota(jnp.int32, (CLASS_ROWS, LANES), 0)
    incl = ntile
    step = 1
    while step < CLASS_ROWS:
        incl = incl + jnp.where(crow >= step, pltpu.roll(incl, step, 0), 0)
        step *= 2
    tstart = incl - ntile
    total = incl[CLASS_ROWS - 1:CLASS_ROWS, 0:1]

    row_base = jnp.broadcast_to(tstart[:, 0:1] * MOE_TILE, (CLASS_ROWS, ts))
    crow_t = lax.broadcasted_iota(jnp.int32, (CLASS_ROWS, ts), 0)

    def token_tile(i, carry):
        cls = rt_ref[i, 0:1, :]
        rank = rt_ref[i, 1:2, :]
        base = jnp.sum(jnp.where(crow_t == cls, row_base, 0), axis=0, keepdims=True)
        pos_ref[i] = base + rank
        return carry

    lax.fori_loop(0, n_tiles_tok, token_tile, 0)

    lane_j = lax.broadcasted_iota(jnp.int32, (CLASS_ROWS, SCHED_LANES), 1)
    start_b = jnp.broadcast_to(tstart[:, 0:1], (CLASS_ROWS, SCHED_LANES))
    ntile_b = jnp.broadcast_to(ntile[:, 0:1], (CLASS_ROWS, SCHED_LANES))
    member = jnp.logical_and(lane_j >= start_b, lane_j < start_b + ntile_b)
    pick = lambda tab: jnp.sum(jnp.where(member, jnp.broadcast_to(tab[:, 0:1], member.shape), 0),
                               axis=0, keepdims=True)
    valid = jnp.sum(member.astype(jnp.int32), axis=0, keepdims=True)
    elo = pick(elo_ref[...])
    ehi = pick(ehi_ref[...])
    j1 = lane_j[0:1, :]
    last = total - 1
    at_last = j1 == last
    elo_last = jnp.sum(jnp.where(at_last, elo, 0), axis=1, keepdims=True)
    ehi_last = jnp.sum(jnp.where(at_last, ehi, 0), axis=1, keepdims=True)
    in_use = valid > 0
    elo = jnp.where(in_use, elo, elo_last)
    ehi = jnp.where(in_use, ehi, ehi_last)
    blk = jnp.minimum(j1, last)
    r8 = lax.broadcasted_iota(jnp.int32, (8, SCHED_LANES), 0)
    sched_ref[...] = jnp.where(r8 == 0, elo, jnp.where(r8 == 1, ehi, jnp.where(r8 == 2, valid,
                               jnp.where(r8 == 3, blk, 0))))
    lane_c = lax.broadcasted_iota(jnp.int32, (CLASS_ROWS, LANES), 1)
    cinfo_ref[...] = jnp.where(lane_c == 0, tstart + ntile - 1,
                               jnp.where(lane_c == 1, (ntile > 0).astype(jnp.int32), total))


def _route_plan(cnt, rt):
    n_tiles_tok, _, ts = rt.shape
    elo_tab = np.zeros((CLASS_ROWS, LANES), np.int32)
    ehi_tab = np.zeros((CLASS_ROWS, LANES), np.int32)
    for g in range(N_GROUPS):
        for p, (a, b) in enumerate(EXPERT_PAIRS):
            elo_tab[g * len(EXPERT_PAIRS) + p, :] = g * EXPERTS_PER_GROUP + a
            ehi_tab[g * len(EXPERT_PAIRS) + p, :] = g * EXPERTS_PER_GROUP + b
    full = lambda shape: pl.BlockSpec(shape, lambda i: (0,) * len(shape))
    return pl.pallas_call(
        _route_plan_body,
        grid=(1,),
        in_specs=[full((CLASS_ROWS, LANES)), full(rt.shape), full((CLASS_ROWS, LANES)), full((CLASS_ROWS, LANES))],
        out_specs=[full((n_tiles_tok, 1, ts)), full((8, SCHED_LANES)), full((CLASS_ROWS, LANES))],
        out_shape=[
            jax.ShapeDtypeStruct((n_tiles_tok, 1, ts), jnp.int32),
            jax.ShapeDtypeStruct((8, SCHED_LANES), jnp.int32),
            jax.ShapeDtypeStruct((CLASS_ROWS, LANES), jnp.int32),
        ],
        compiler_params=pltpu.CompilerParams(dimension_semantics=("arbitrary",)),
        name="route_plan",
    )(cnt, rt, jnp.asarray(elo_tab), jnp.asarray(ehi_tab))


DMA_UNROLL = 8


def _start_rows(n_rows, make_copy):
    def trip(t8, carry):
        for u in range(DMA_UNROLL):
            make_copy(t8 * DMA_UNROLL + u).start(priority=u % 2)
        return carry
    lax.fori_loop(0, n_rows // DMA_UNROLL, trip, 0)


def _wait_rows(n_rows, make_copy):
    def trip(t8, carry):
        for _ in range(DMA_UNROLL):
            make_copy(0).wait()
        return carry
    lax.fori_loop(0, n_rows // DMA_UNROLL, trip, 0)


def _scatter_rows_body(last_ref, has_ref, total_ref, pos_ref, h1_ref, gffn_ref, rg_ref, xs_hbm,
                       rows, zbuf, sem, zsem):
    i = pl.program_id(0)
    n = pl.num_programs(0)
    ts = h1_ref.shape[0]
    slot = i % 2

    n_dst_tiles = xs_hbm.shape[0] // MOE_TILE

    def pad_copies():
        for c in range(N_CLASSES):
            yield has_ref[c] != 0, last_ref[c]
            yield total_ref[0] + c < n_dst_tiles, total_ref[0] + c

    def tile_copy(tile):
        return pltpu.make_async_copy(zbuf, xs_hbm.at[pl.ds(tile * MOE_TILE, MOE_TILE)], zsem)

    @pl.when(i == 0)
    def _():
        zbuf[...] = jnp.zeros(zbuf.shape, zbuf.dtype)
        for cond, tile in pad_copies():
            @pl.when(cond)
            def _():
                tile_copy(tile).start()
        for cond, tile in pad_copies():
            @pl.when(cond)
            def _():
                tile_copy(tile).wait()

    h1 = h1_ref[...]
    ms = jnp.mean(h1 * h1, axis=-1, keepdims=True)
    rows[slot, :, 0:D_MODEL] = h1 * lax.rsqrt(ms + EPS) * gffn_ref[...]
    erow = lax.broadcasted_iota(jnp.int32, (GATE_LANES, ts), 0)
    gt = (jnp.where(erow == GATE_LO_LANE, rg_ref[0:1, :], 0.0)
          + jnp.where(erow == GATE_HI_LANE, rg_ref[1:2, :], 0.0))
    rows[slot, :, D_MODEL:XG_WIDTH] = gt.T

    def row_copy(sl):
        return lambda t: pltpu.make_async_copy(
            rows.at[sl, pl.ds(t, 1)], xs_hbm.at[pl.ds(pos_ref[0, t], 1)], sem.at[sl])

    _start_rows(ts, row_copy(slot))

    @pl.when(i > 0)
    def _():
        _wait_rows(ts, row_copy(1 - slot))

    @pl.when(i == n - 1)
    def _():
        _wait_rows(ts, row_copy(slot))


def _scatter_rows(cinfo, pos, h1, g_ffn, rg, n_dst_rows):
    n_tiles_tok, _, ts = pos.shape
    return pl.pallas_call(
        _scatter_rows_body,
        grid_spec=pltpu.PrefetchScalarGridSpec(
            num_scalar_prefetch=3,
            grid=(n_tiles_tok,),
            in_specs=[
                pl.BlockSpec((None, 1, ts), lambda i, *_: (i, 0, 0), memory_space=pltpu.SMEM),
                pl.BlockSpec((ts, D_MODEL), lambda i, *_: (i, 0)),
                pl.BlockSpec((1, D_MODEL), lambda i, *_: (0, 0)),
                pl.BlockSpec((None, 8, ts), lambda i, *_: (i, 0, 0)),
            ],
            out_specs=pl.BlockSpec(memory_space=pl.ANY),
            scratch_shapes=[
                pltpu.VMEM((2, ts, XG_WIDTH), F32),
                pltpu.VMEM((MOE_TILE, XG_WIDTH), F32),
                pltpu.SemaphoreType.DMA((2,)),
                pltpu.SemaphoreType.DMA(()),
            ],
        ),
        out_shape=jax.ShapeDtypeStruct((n_dst_rows, XG_WIDTH), F32),
        compiler_params=pltpu.CompilerParams(dimension_semantics=("arbitrary",), vmem_limit_bytes=VMEM_LIMIT),
        name="scatter_rows",
    )(cinfo[:, 0], cinfo[:, 1], cinfo[0:1, 2], pos, h1, g_ffn, rg)


def _moe_body(elo_ref, ehi_ref, valid_ref, blk_ref, xs_ref, w1l_ref, w3l_ref, w2l_ref, w1h_ref, w3h_ref, w2h_ref,
              ys_ref):
    @pl.when(valid_ref[pl.program_id(0)] != 0)
    def _():
        x = xs_ref[:, 0:D_MODEL].astype(BF16)

        def expert(w1_ref, w3_ref, w2_ref):
            a = jnp.dot(x, w1_ref[...], preferred_element_type=F32)
            b = jnp.dot(x, w3_ref[...], preferred_element_type=F32)
            hdn = (a * (1.0 / (1.0 + jnp.exp(-a))) * b).astype(BF16)
            return jnp.dot(hdn, w2_ref[...], preferred_element_type=F32)

        g_lo = xs_ref[:, D_MODEL + GATE_LO_LANE:D_MODEL + GATE_LO_LANE + 1]
        g_hi = xs_ref[:, D_MODEL + GATE_HI_LANE:D_MODEL + GATE_HI_LANE + 1]
        ys_ref[...] = (g_lo * expert(w1l_ref, w3l_ref, w2l_ref)
                       + g_hi * expert(w1h_ref, w3h_ref, w2h_ref))

    @pl.when(valid_ref[pl.program_id(0)] == 0)
    def _():
        ys_ref[...] = jnp.zeros(ys_ref.shape, F32)


def _moe(sched, xs, w1, w3, w2):
    n_tiles = xs.shape[0] // MOE_TILE
    rows = lambda j, elo, ehi, valid, blk: (blk[j], 0)
    w_lo = lambda j, elo, ehi, valid, blk: (elo[j], 0, 0)
    w_hi = lambda j, elo, ehi, valid, blk: (ehi[j], 0, 0)
    up = (None, D_MODEL, D_FF)
    down = (None, D_FF, D_MODEL)
    return pl.pallas_call(
        _moe_body,
        grid_spec=pltpu.PrefetchScalarGridSpec(
            num_scalar_prefetch=4,
            grid=(n_tiles,),
            in_specs=[
                pl.BlockSpec((MOE_TILE, XG_WIDTH), rows),
                pl.BlockSpec(up, w_lo), pl.BlockSpec(up, w_lo), pl.BlockSpec(down, w_lo),
                pl.BlockSpec(up, w_hi), pl.BlockSpec(up, w_hi), pl.BlockSpec(down, w_hi),
            ],
            out_specs=pl.BlockSpec((MOE_TILE, D_MODEL), lambda j, *_: (j, 0)),
        ),
        out_shape=jax.ShapeDtypeStruct((xs.shape[0], D_MODEL), F32),
        compiler_params=pltpu.CompilerParams(dimension_semantics=("arbitrary",), vmem_limit_bytes=VMEM_LIMIT),
        name="moe",
    )(sched[0, :n_tiles], sched[1, :n_tiles], sched[2, :n_tiles], sched[3, :n_tiles], xs, w1, w3, w2, w1, w3, w2)


def _ple_body(pos_ref, posn_ref, h1_ref, ys_hbm, p_ref, gple_ref, wg_ref, wp_ref, gpost_ref, o_ref, ybuf, sem):
    i = pl.program_id(0)
    n = pl.num_programs(0)
    ts = h1_ref.shape[0]
    slot = i % 2

    def row_copy(pref, sl):
        return lambda t: pltpu.make_async_copy(
            ys_hbm.at[pl.ds(pref[0, t], 1)], ybuf.at[sl, pl.ds(t, 1)], sem.at[sl])

    @pl.when(i == 0)
    def _():
        _start_rows(ts, row_copy(pos_ref, 0))

    @pl.when(i + 1 < n)
    def _():
        _start_rows(ts, row_copy(posn_ref, 1 - slot))

    _wait_rows(ts, row_copy(pos_ref, slot))

    h2 = h1_ref[...] + ybuf[slot]
    ms = jnp.mean(h2 * h2, axis=-1, keepdims=True)
    hn = (h2 * lax.rsqrt(ms + EPS) * gple_ref[...]).astype(BF16)
    z = jnp.dot(hn, wg_ref[...], preferred_element_type=F32)
    gate = 1.0 / (1.0 + jnp.exp(-z))
    pp = jnp.dot(p_ref[...].astype(BF16), wp_ref[...], preferred_element_type=F32)
    pms = jnp.mean(pp * pp, axis=-1, keepdims=True)
    pn = pp * lax.rsqrt(pms + EPS) * gpost_ref[...]
    o_ref[...] = h2 + gate * pn


def _ple(pos, h1, ys, p, g_ple, wg, wp, g_post):
    T = h1.shape[0]
    n_tiles_tok, _, ts = pos.shape
    row = lambda i: (i, 0)
    const = lambda i: (0, 0)
    return pl.pallas_call(
        _ple_body,
        grid=(n_tiles_tok,),
        in_specs=[
            pl.BlockSpec((None, 1, ts), lambda i: (i, 0, 0), memory_space=pltpu.SMEM),
            pl.BlockSpec((None, 1, ts), lambda i: (jnp.minimum(i + 1, n_tiles_tok - 1), 0, 0),
                         memory_space=pltpu.SMEM),
            pl.BlockSpec((ts, D_MODEL), row),
            pl.BlockSpec(memory_space=pl.ANY),
            pl.BlockSpec((ts, PLE_DIM), row),
            pl.BlockSpec((1, D_MODEL), const),
            pl.BlockSpec((D_MODEL, D_MODEL), const),
            pl.BlockSpec((PLE_DIM, D_MODEL), const),
            pl.BlockSpec((1, D_MODEL), const),
        ],
        out_specs=pl.BlockSpec((ts, D_MODEL), row),
        out_shape=jax.ShapeDtypeStruct((T, D_MODEL), F32),
        scratch_shapes=[pltpu.VMEM((2, ts, D_MODEL), F32), pltpu.SemaphoreType.DMA((2,))],
        compiler_params=pltpu.CompilerParams(
            dimension_semantics=("arbitrary",), vmem_limit_bytes=VMEM_LIMIT),
        name="ple",
    )(pos, pos, h1, ys, p, g_ple, wg, wp, g_post)


def _rope_tables(S):
    pos = jnp.arange(S)
    inv = ROPE_THETA ** (-jnp.arange(0, ROPE_DIM, 2, dtype=F32) / ROPE_DIM)
    ang = pos.astype(F32)[:, None] * inv[None, :]
    cos, sin = jnp.cos(ang), jnp.sin(ang)
    ones = jnp.ones((S, HEAD_DIM - ROPE_DIM), F32)
    zeros_h = jnp.zeros((S, ROPE_HALF), F32)
    zeros_r = jnp.zeros((S, HEAD_DIM - ROPE_DIM), F32)
    cos_h = jnp.concatenate([cos, cos, ones], axis=1)
    sa_h = jnp.concatenate([-sin, zeros_h, zeros_r], axis=1)
    sb_h = jnp.concatenate([zeros_h, sin, zeros_r], axis=1)
    pair = lambda t: jnp.concatenate([t, t], axis=1)
    return pair(cos_h), pair(sa_h), pair(sb_h)


def _layer(h, p_i, g_mix, w_in, q_norm, k_norm, conv_w, g_attn_out, g_conv_out, w_out, g_ffn,
           w_rg, b_rg, w_re, b_re, w1, w3, w2, g_ple, w_ple_gate, w_ple_proj, g_ple_post):
    B, S, _ = h.shape
    T = B * S
    row = lambda g: g.reshape(1, -1).astype(F32)

    cos_t, sa_t, sb_t = _rope_tables(S)
    bd256 = _block_diag_ones(MXU_DIM, HEAD_DIM)
    bd128 = _block_diag_ones(LANES, HEAD_DIM)
    gq = row(jnp.tile(q_norm, N_HEADS) * (HEAD_DIM ** -0.5))
    gk = row(jnp.tile(k_norm, N_HEADS))
    cw = jnp.zeros((8, CONV_WIDTH), F32).at[0:CONV_K].set(conv_w)

    q, k, v, convn = _in_proj(h, row(g_mix), w_in.astype(BF16), gq, gk, cos_t, sa_t, sb_t, bd256, cw,
                              row(g_conv_out))
    attn = _attention(q, k, v, row(g_attn_out), bd128, _band_bias())

    w_r = jnp.zeros((D_MODEL, GATE_LANES), F32)
    w_r = w_r.at[:, ROUTER_GROUP_COL:ROUTER_GROUP_COL + N_GROUPS].set(w_rg)
    w_r = w_r.at[:, ROUTER_EXPERT_COL:ROUTER_EXPERT_COL + N_EXPERTS].set(w_re)
    b_r = jnp.zeros((1, GATE_LANES), F32)
    b_r = b_r.at[0, ROUTER_GROUP_COL:ROUTER_GROUP_COL + N_GROUPS].set(b_rg)
    b_r = b_r.at[0, ROUTER_EXPERT_COL:ROUTER_EXPERT_COL + N_EXPERTS].set(b_re)
    wr_hi = w_r.astype(BF16)
    wr_lo = (w_r - wr_hi.astype(F32)).astype(BF16)

    h1, rt, rg, cnt = _out_route(h, attn, convn, w_out.astype(BF16), row(g_ffn), wr_hi, wr_lo, b_r)
    n_tok_tiles = T // ROW_TILE
    n_sorted_tiles = T // MOE_TILE + N_CLASSES
    assert n_sorted_tiles <= SCHED_LANES
    pos, sched, cinfo = _route_plan(cnt, rt.reshape(n_tok_tiles, 8, ROW_TILE))
    h1 = h1.reshape(T, D_MODEL)
    xs = _scatter_rows(cinfo, pos, h1, row(g_ffn), rg.reshape(n_tok_tiles, 8, ROW_TILE),
                       n_sorted_tiles * MOE_TILE)
    ys = _moe(sched, xs, w1.astype(BF16), w3.astype(BF16), w2.astype(BF16))
    out = _ple(pos, h1, ys, p_i.reshape(T, PLE_DIM), row(g_ple),
               w_ple_gate.astype(BF16), w_ple_proj.astype(BF16), row(g_ple_post))
    return out.reshape(B, S, D_MODEL)


def kernel(x, p, g_mix, w_in, q_norm, k_norm, conv_w, g_attn_out, g_conv_out, w_out, g_ffn, w_router_group, b_router_group, w_router_expert, b_router_expert, w1, w3, w2, g_ple, w_ple_gate, w_ple_proj, g_ple_post):
    h = x
    for i in range(p.shape[0]):
        h = _layer(h, p[i], g_mix[i], w_in[i], q_norm[i], k_norm[i], conv_w[i], g_attn_out[i], g_conv_out[i],
                   w_out[i], g_ffn[i], w_router_group[i], b_router_group[i], w_router_expert[i],
                   b_router_expert[i], w1[i], w3[i], w2[i], g_ple[i], w_ple_gate[i], w_ple_proj[i],
                   g_ple_post[i])
    return h
```

```python
import functools

import numpy as np
import jax
import jax.numpy as jnp
from jax import lax
from jax.experimental import pallas as pl
from jax.experimental.pallas import tpu as pltpu

F32 = jnp.float32
BF16 = jnp.bfloat16

D_MODEL = 1024
PLE_DIM = 256
HEAD_DIM = 64
N_HEADS = 8
ATTN_WIDTH = N_HEADS * HEAD_DIM
CONV_WIDTH = D_MODEL - ATTN_WIDTH
CONV_K = 3
DILATIONS = (1, 4, 16)
N_BACK = 128
ATTN_BLOCK = 128
ROPE_THETA = 500000.0
ROPE_DIM = HEAD_DIM // 4
ROPE_HALF = ROPE_DIM // 2
N_GROUPS = 4
EXPERTS_PER_GROUP = 4
N_EXPERTS = N_GROUPS * EXPERTS_PER_GROUP
D_FF = 512
EPS = 1e-6

LANES = 128
MXU_DIM = 256
NEG_BIG = -1e30
LOG2_E = 1.4426950408889634

ROW_TILE = 512
ATTN_TILE = 2048
ATTN_UNROLL = 16
MOE_TILE = 256
GATE_LANES = LANES
XG_WIDTH = D_MODEL + GATE_LANES
GATE_LO_LANE = 0
GATE_HI_LANE = 1
EXPERT_PAIRS = tuple((a, b) for a in range(EXPERTS_PER_GROUP) for b in range(a + 1, EXPERTS_PER_GROUP))
N_CLASSES = N_GROUPS * len(EXPERT_PAIRS)
CLASS_ROWS = 32
SCHED_LANES = 256
ROUTER_GROUP_COL = 0
ROUTER_EXPERT_COL = 8
VMEM_LIMIT = 48 * 1024 * 1024


def _block_diag_ones(n, seg):
    idx = np.arange(n) // seg
    return jnp.asarray((idx[:, None] == idx[None, :]).astype(np.float32), dtype=BF16)


def _segment_mean_sq(t, bd):
    t2 = (t * t).astype(BF16)
    parts = [jnp.dot(t2[:, i:i + MXU_DIM], bd, preferred_element_type=F32)
             for i in range(0, t.shape[1], MXU_DIM)]
    return jnp.concatenate(parts, axis=1) * (1.0 / HEAD_DIM)


def _in_proj_body(x_ref, gmix_ref, win_ref, gq_ref, gk_ref, cos_ref, sa_ref, sb_ref, bd_ref, cw_ref, gconv_ref,
                  q_ref, k_ref, v_ref, conv_ref, ubuf):
    ts = x_ref.shape[0]
    x = x_ref[...]
    ms = jnp.mean(x * x, axis=-1, keepdims=True)
    xn = (x * lax.rsqrt(ms + EPS) * gmix_ref[...]).astype(BF16)
    bd = bd_ref[...]

    def proj(c):
        return jnp.dot(xn, win_ref[:, c * ATTN_WIDTH:(c + 1) * ATTN_WIDTH], preferred_element_type=F32)

    reps = ATTN_WIDTH // LANES
    cos = jnp.concatenate([cos_ref[...]] * reps, axis=1)
    sa = jnp.concatenate([sa_ref[...]] * reps, axis=1)
    sb = jnp.concatenate([sb_ref[...]] * reps, axis=1)

    def norm_rope(t, g):
        tn = t * lax.rsqrt(_segment_mean_sq(t, bd) + EPS) * g
        up = pltpu.roll(tn, ATTN_WIDTH - ROPE_HALF, 1)
        dn = pltpu.roll(tn, ROPE_HALF, 1)
        return tn * cos + up * sa + dn * sb

    q_ref[...] = norm_rope(proj(0), gq_ref[...]).astype(BF16)
    k_ref[...] = norm_rope(proj(1), gk_ref[...]).astype(BF16)
    v_ref[...] = proj(2).astype(BF16)

    cb = proj(3)
    u = proj(4) * proj(5)

    @pl.when(pl.program_id(1) == 0)
    def _():
        ubuf[0:8, :] = jnp.zeros((8, CONV_WIDTH), F32)

    ubuf[8:8 + ts, :] = u
    u1 = ubuf[7:7 + ts, :]
    u2 = ubuf[6:6 + ts, :]
    y = cw_ref[0:1, :] * u2 + cw_ref[1:2, :] * u1 + cw_ref[2:3, :] * u
    ubuf[0:8, :] = ubuf[ts:ts + 8, :]
    conv = cb * y
    convn = conv * lax.rsqrt(_segment_mean_sq(conv, bd) + EPS) * gconv_ref[...]
    conv_ref[...] = convn.astype(BF16)


def _in_proj(x, g_mix, w_in, gq, gk, cos_t, sa_t, sb_t, bd, cw, g_conv):
    B, S, _ = x.shape
    ts = ROW_TILE
    row = lambda b, j: (b, j, 0)
    const2 = lambda b, j: (0, 0)
    tab = lambda b, j: (j, 0)
    out_sds = jax.ShapeDtypeStruct((B, S, ATTN_WIDTH), BF16)
    return pl.pallas_call(
        _in_proj_body,
        grid=(B, S // ts),
        in_specs=[
            pl.BlockSpec((None, ts, D_MODEL), row),
            pl.BlockSpec((1, D_MODEL), const2),
            pl.BlockSpec((D_MODEL, 6 * ATTN_WIDTH), const2),
            pl.BlockSpec((1, ATTN_WIDTH), const2),
            pl.BlockSpec((1, ATTN_WIDTH), const2),
            pl.BlockSpec((ts, LANES), tab),
            pl.BlockSpec((ts, LANES), tab),
            pl.BlockSpec((ts, LANES), tab),
            pl.BlockSpec((MXU_DIM, MXU_DIM), const2),
            pl.BlockSpec((8, CONV_WIDTH), const2),
            pl.BlockSpec((1, CONV_WIDTH), const2),
        ],
        out_specs=[pl.BlockSpec((None, ts, ATTN_WIDTH), row)] * 4,
        out_shape=[out_sds] * 4,
        scratch_shapes=[pltpu.VMEM((ts + 8, CONV_WIDTH), F32)],
        compiler_params=pltpu.CompilerParams(
            dimension_semantics=("arbitrary", "arbitrary"), vmem_limit_bytes=VMEM_LIMIT),
        name="in_proj",
    )(x, g_mix, w_in, gq, gk, cos_t, sa_t, sb_t, bd, cw, g_conv)


def _attn_body(q_ref, kp_ref, kc_ref, vp_ref, vc_ref, gat_ref, bd_ref, bias_ref, o_ref,
               qf, kf, vf, acc_a, acc_b, m_a, m_b):
    tq = q_ref.shape[0]
    blk = ATTN_BLOCK
    first_tile = pl.program_id(2) == 0

    qf[...] = q_ref[...].astype(F32)
    kf[0:tq, :] = kp_ref[...].astype(F32)
    kf[tq:2 * tq, :] = kc_ref[...].astype(F32)
    vf[0:tq, :] = vp_ref[...].astype(F32)
    vf[tq:2 * tq, :] = vc_ref[...].astype(F32)

    lane = lax.broadcasted_iota(jnp.int32, (1, LANES), 1)
    sel_a = (lane < HEAD_DIM).astype(F32).astype(BF16)
    sel_b = (lane >= HEAD_DIM).astype(F32).astype(BF16)

    order = DILATIONS[::-1]
    for d in order:
        nblk = tq // (blk * d)
        shift = nblk.bit_length() - 1
        first_visit = d == order[0]

        def block(idx, carry, d=d, nblk=nblk, shift=shift, first_visit=first_visit):
            r = idx >> shift
            n = idx & (nblk - 1)
            rows_q = pl.ds(r + d * blk * n, blk, stride=d)
            rows_k = pl.ds(tq + d * blk * (n - 1) + r, 2 * blk, stride=d)
            qw = qf[rows_q, :].astype(BF16)
            kw = kf[rows_k, :].astype(BF16)
            vw = vf[rows_k, :].astype(BF16)
            bias = bias_ref[jnp.where(jnp.logical_and(first_tile, n == 0), 1, 0)]

            def one_head(sel_q, sel_one, acc, mst):
                s = lax.dot_general(qw * sel_q, kw, (((1,), (1,)), ((), ())), preferred_element_type=F32) + bias
                mb = jnp.max(s, axis=-1, keepdims=True)
                if first_visit:
                    m_new = jnp.broadcast_to(mb, (blk, LANES))
                else:
                    m_old = mst[rows_q, :]
                    m_new = jnp.maximum(m_old, mb)
                e = jnp.exp2(s - jnp.concatenate([m_new, m_new], axis=1))
                pv = jnp.dot(e.astype(BF16), vw * sel_q + sel_one, preferred_element_type=F32)
                if first_visit:
                    acc[rows_q, :] = pv
                else:
                    acc[rows_q, :] = acc[rows_q, :] * jnp.exp2(m_old - m_new) + pv
                mst[rows_q, :] = m_new

            one_head(sel_a, sel_b, acc_a, m_a)
            one_head(sel_b, sel_a, acc_b, m_b)
            return carry

        lax.fori_loop(0, tq // blk, block, 0, unroll=ATTN_UNROLL)

    head_a = lax.broadcasted_iota(jnp.int32, (tq, LANES), 1) < HEAD_DIM
    aa = acc_a[...]
    ab = acc_b[...]
    num = jnp.where(head_a, aa, ab)
    den = pltpu.roll(jnp.where(head_a, ab, aa), HEAD_DIM, 1)
    o = num / den
    o2 = (o * o).astype(BF16)
    msq = jnp.dot(o2, bd_ref[...], preferred_element_type=F32) * (1.0 / HEAD_DIM)
    o_ref[...] = (o * lax.rsqrt(msq + EPS) * gat_ref[...]).astype(BF16)


def _attention(q, k, v, g_attn, bd, bias):
    B, S, _ = q.shape
    tq = ATTN_TILE
    n_pairs = ATTN_WIDTH // LANES
    cur = lambda b, hp, j: (b, j, hp)
    prev = lambda b, hp, j: (b, jnp.maximum(j - 1, 0), hp)
    blk = (None, tq, LANES)
    return pl.pallas_call(
        _attn_body,
        grid=(B, n_pairs, S // tq),
        in_specs=[
            pl.BlockSpec(blk, cur),
            pl.BlockSpec(blk, prev),
            pl.BlockSpec(blk, cur),
            pl.BlockSpec(blk, prev),
            pl.BlockSpec(blk, cur),
            pl.BlockSpec((1, LANES), lambda b, hp, j: (0, hp)),
            pl.BlockSpec((LANES, LANES), lambda b, hp, j: (0, 0)),
            pl.BlockSpec((2, ATTN_BLOCK, 2 * ATTN_BLOCK), lambda b, hp, j: (0, 0, 0)),
        ],
        out_specs=pl.BlockSpec(blk, cur),
        out_shape=jax.ShapeDtypeStruct((B, S, ATTN_WIDTH), BF16),
        scratch_shapes=[
            pltpu.VMEM((tq, LANES), F32),
            pltpu.VMEM((2 * tq, LANES), F32), pltpu.VMEM((2 * tq, LANES), F32),
            pltpu.VMEM((tq, LANES), F32), pltpu.VMEM((tq, LANES), F32),
            pltpu.VMEM((tq, LANES), F32), pltpu.VMEM((tq, LANES), F32),
        ],
        compiler_params=pltpu.CompilerParams(
            dimension_semantics=("arbitrary", "arbitrary", "arbitrary"), vmem_limit_bytes=VMEM_LIMIT),
        name="attention",
    )(q, k, k, v, v, g_attn, bd, bias)


def _band_bias():
    qi = np.arange(ATTN_BLOCK)[:, None]
    ki = np.arange(2 * ATTN_BLOCK)[None, :]
    dist = qi + ATTN_BLOCK - ki
    band = (dist >= 0) & (dist <= N_BACK)
    no_prev = band & (ki >= ATTN_BLOCK)
    tab = np.stack([band, no_prev]).astype(np.float32)
    return jnp.asarray((1.0 - tab) * NEG_BIG, dtype=F32)


def _first_argmax(vals):
    best = vals[0]
    idx = jnp.zeros(best.shape, jnp.int32)
    for i in range(1, len(vals)):
        upd = vals[i] > best
        idx = jnp.where(upd, i, idx)
        best = jnp.where(upd, vals[i], best)
    return best, idx


def _out_route_body(x_ref, attn_ref, conv_ref, wo_ref, gffn_ref, wrh_ref, wrl_ref, br_ref, upper_ref,
                    h1_ref, rt_ref, rg_ref, cnt_ref, cnt):
    ts = x_ref.shape[0]
    h1 = (x_ref[...]
          + jnp.dot(attn_ref[...], wo_ref[0:ATTN_WIDTH, :], preferred_element_type=F32)
          + jnp.dot(conv_ref[...], wo_ref[ATTN_WIDTH:, :], preferred_element_type=F32))
    h1_ref[...] = h1
    ms = jnp.mean(h1 * h1, axis=-1, keepdims=True)
    xn = h1 * lax.rsqrt(ms + EPS) * gffn_ref[...]

    xh = xn.astype(BF16)
    xl = (xn - xh.astype(F32)).astype(BF16)
    logits = (jnp.dot(xh, wrh_ref[...], preferred_element_type=F32)
              + jnp.dot(xl, wrh_ref[...], preferred_element_type=F32)
              + jnp.dot(xh, wrl_ref[...], preferred_element_type=F32)) + br_ref[...]
    lt = logits.T

    lg = [lt[ROUTER_GROUP_COL + i:ROUTER_GROUP_COL + i + 1, :] for i in range(N_GROUPS)]
    gbest, gi = _first_argmax(lg)
    sumexp = lg[0] * 0.0
    for i in range(N_GROUPS):
        sumexp = sumexp + jnp.exp(lg[i] - gbest)
    pg_top = 1.0 / sumexp

    sel = []
    for jx in range(EXPERTS_PER_GROUP):
        cand = [lt[ROUTER_EXPERT_COL + EXPERTS_PER_GROUP * g + jx:ROUTER_EXPERT_COL + EXPERTS_PER_GROUP * g + jx + 1, :]
                for g in range(N_GROUPS)]
        vj = cand[N_GROUPS - 1]
        for g in range(N_GROUPS - 2, -1, -1):
            vj = jnp.where(gi == g, cand[g], vj)
        sel.append(vj)
    b1, i1 = _first_argmax(sel)
    masked = [jnp.where(i1 == jx, -jnp.inf, sel[jx]) for jx in range(EXPERTS_PER_GROUP)]
    b2, i2 = _first_argmax(masked)
    e2 = jnp.exp(b2 - b1)
    t1 = 1.0 / (1.0 + e2)
    ga = pg_top * t1
    gb = pg_top * (e2 * t1)
    lo = jnp.minimum(i1, i2)
    hi = jnp.maximum(i1, i2)
    g_lo = jnp.where(i1 < i2, ga, gb)
    g_hi = jnp.where(i1 < i2, gb, ga)
    pair = hi - lo - 1
    for a in range(1, EXPERTS_PER_GROUP - 1):
        pair = pair + jnp.where(lo >= a, EXPERT_PAIRS.index((a, a + 1)) - EXPERT_PAIRS.index((a - 1, a)), 0)
    cls = gi * len(EXPERT_PAIRS) + pair

    r8 = lax.broadcasted_iota(jnp.int32, (8, ts), 0)
    rg_ref[...] = jnp.where(r8 == 0, g_lo, jnp.where(r8 == 1, g_hi, 0.0))

    @pl.when(jnp.logical_and(pl.program_id(0) == 0, pl.program_id(1) == 0))
    def _():
        cnt[...] = jnp.zeros(cnt.shape, F32)

    crow = lax.broadcasted_iota(jnp.int32, (CLASS_ROWS, ts), 0)
    oh = (crow == cls).astype(F32)
    before = jnp.dot(oh.astype(BF16), upper_ref[...], preferred_element_type=F32)
    rank = jnp.sum(oh * (before + cnt[:, 0:1]), axis=0, keepdims=True)
    cnt[...] = cnt[...] + jnp.sum(oh, axis=1, keepdims=True)
    cnt_ref[...] = cnt[...]
    rt_ref[...] = jnp.where(r8 == 0, cls, jnp.where(r8 == 1, rank.astype(jnp.int32), 0))


def _out_route(x, attn, conv, w_out, g_ffn, wr_hi, wr_lo, b_r):
    B, S, _ = x.shape
    ts = ROW_TILE
    row = lambda b, j: (b, j, 0)
    const2 = lambda b, j: (0, 0)
    tok = np.arange(ts)
    upper = jnp.asarray((tok[:, None] < tok[None, :]).astype(np.float32), dtype=BF16)
    return pl.pallas_call(
        _out_route_body,
        grid=(B, S // ts),
        in_specs=[
            pl.BlockSpec((None, ts, D_MODEL), row),
            pl.BlockSpec((None, ts, ATTN_WIDTH), row),
            pl.BlockSpec((None, ts, CONV_WIDTH), row),
            pl.BlockSpec((D_MODEL, D_MODEL), const2),
            pl.BlockSpec((1, D_MODEL), const2),
            pl.BlockSpec((D_MODEL, GATE_LANES), const2),
            pl.BlockSpec((D_MODEL, GATE_LANES), const2),
            pl.BlockSpec((1, GATE_LANES), const2),
            pl.BlockSpec((ts, ts), const2),
        ],
        out_specs=[
            pl.BlockSpec((None, ts, D_MODEL), row),
            pl.BlockSpec((None, None, 8, ts), lambda b, j: (b, j, 0, 0)),
            pl.BlockSpec((None, None, 8, ts), lambda b, j: (b, j, 0, 0)),
            pl.BlockSpec((CLASS_ROWS, LANES), const2),
        ],
        out_shape=[
            jax.ShapeDtypeStruct((B, S, D_MODEL), F32),
            jax.ShapeDtypeStruct((B, S // ts, 8, ts), jnp.int32),
            jax.ShapeDtypeStruct((B, S // ts, 8, ts), F32),
            jax.ShapeDtypeStruct((CLASS_ROWS, LANES), F32),
        ],
        scratch_shapes=[pltpu.VMEM((CLASS_ROWS, LANES), F32)],
        compiler_params=pltpu.CompilerParams(
            dimension_semantics=("arbitrary", "arbitrary"), vmem_limit_bytes=VMEM_LIMIT),
        name="out_route",
    )(x, attn, conv, w_out, g_ffn, wr_hi, wr_lo, b_r, upper)


def _route_plan_body(cnt_ref, rt_ref, elo_ref, ehi_ref, pos_ref, sched_ref, cinfo_ref):
    n_tiles_tok, _, ts = pos_ref.shape
    shift = MOE_TILE.bit_length() - 1
    ntile = (cnt_ref[...].astype(jnp.int32) + (MOE_TILE - 1)) >> shift
    crow = lax.broadcasted_iota(jnp.int32, (CLASS_ROWS, LANES), 0)
    incl = ntile
    step = 1
    while step < CLASS_ROWS:
        incl = incl + jnp.where(crow >= step, pltpu.roll(incl, step, 0), 0)
        step *= 2
    tstart = incl - ntile
    total = incl[CLASS_ROWS - 1:CLASS_ROWS, 0:1]

    row_base = jnp.broadcast_to(tstart[:, 0:1] * MOE_TILE, (CLASS_ROWS, ts))
    crow_t = lax.broadcasted_iota(jnp.int32, (CLASS_ROWS, ts), 0)

    def token_tile(i, carry):
        cls = rt_ref[i, 0:1, :]
        rank = rt_ref[i, 1:2, :]
        base = jnp.sum(jnp.where(crow_t == cls, row_base, 0), axis=0, keepdims=True)
        pos_ref[i] = base + rank
        return carry

    lax.fori_loop(0, n_tiles_tok, token_tile, 0)

    lane_j = lax.broadcasted_iota(jnp.int32, (CLASS_ROWS, SCHED_LANES), 1)
    start_b = jnp.broadcast_to(tstart[:, 0:1], (CLASS_ROWS, SCHED_LANES))
    ntile_b = jnp.broadcast_to(ntile[:, 0:1], (CLASS_ROWS, SCHED_LANES))
    member = jnp.logical_and(lane_j >= start_b, lane_j < start_b + ntile_b)
    pick = lambda tab: jnp.sum(jnp.where(member, jnp.broadcast_to(tab[:, 0:1], member.shape), 0),
                               axis=0, keepdims=True)
    valid = jnp.sum(member.astype(jnp.int32), axis=0, keepdims=True)
    elo = pick(elo_ref[...])
    ehi = pick(ehi_ref[...])
    j1 = lane_j[0:1, :]
    last = total - 1
    at_last = j1 == last
    elo_last = jnp.sum(jnp.where(at_last, elo, 0), axis=1, keepdims=True)
    ehi_last = jnp.sum(jnp.where(at_last, ehi, 0), axis=1, keepdims=True)
    in_use = valid > 0
    elo = jnp.where(in_use, elo, elo_last)
    ehi = jnp.where(in_use, ehi, ehi_last)
    blk = jnp.minimum(j1, last)
    r8 = lax.broadcasted_iota(jnp.int32, (8, SCHED_LANES), 0)
    sched_ref[...] = jnp.where(r8 == 0, elo, jnp.where(r8 == 1, ehi, jnp.where(r8 == 2, valid,
                               jnp.where(r8 == 3, blk, 0))))
    lane_c = lax.broadcasted_iota(jnp.int32, (CLASS_ROWS, LANES), 1)
    cinfo_ref[...] = jnp.where(lane_c == 0, tstart + ntile - 1,
                               jnp.where(lane_c == 1, (ntile > 0).astype(jnp.int32), total))


def _route_plan(cnt, rt):
    n_tiles_tok, _, ts = rt.shape
    elo_tab = np.zeros((CLASS_ROWS, LANES), np.int32)
    ehi_tab = np.zeros((CLASS_ROWS, LANES), np.int32)
    for g in range(N_GROUPS):
        for p, (a, b) in enumerate(EXPERT_PAIRS):
            elo_tab[g * len(EXPERT_PAIRS) + p, :] = g * EXPERTS_PER_GROUP + a
            ehi_tab[g * len(EXPERT_PAIRS) + p, :] = g * EXPERTS_PER_GROUP + b
    full = lambda shape: pl.BlockSpec(shape, lambda i: (0,) * len(shape))
    return pl.pallas_call(
        _route_plan_body,
        grid=(1,),
        in_specs=[full((CLASS_ROWS, LANES)), full(rt.shape), full((CLASS_ROWS, LANES)), full((CLASS_ROWS, LANES))],
        out_specs=[full((n_tiles_tok, 1, ts)), full((8, SCHED_LANES)), full((CLASS_ROWS, LANES))],
        out_shape=[
            jax.ShapeDtypeStruct((n_tiles_tok, 1, ts), jnp.int32),
            jax.ShapeDtypeStruct((8, SCHED_LANES), jnp.int32),
            jax.ShapeDtypeStruct((CLASS_ROWS, LANES), jnp.int32),
        ],
        compiler_params=pltpu.CompilerParams(dimension_semantics=("arbitrary",)),
        name="route_plan",
    )(cnt, rt, jnp.asarray(elo_tab), jnp.asarray(ehi_tab))


DMA_UNROLL = 8


def _start_rows(n_rows, make_copy):
    for t in range(n_rows):
        make_copy(t).start(priority=t % 2)


def _wait_rows(n_rows, make_copy):
    def trip(t8, carry):
        for _ in range(DMA_UNROLL):
            make_copy(0).wait()
        return carry
    lax.fori_loop(0, n_rows // DMA_UNROLL, trip, 0)


def _scatter_rows_body(last_ref, has_ref, total_ref, pos_ref, h1_ref, gffn_ref, rg_ref, xs_hbm,
                       rows, zbuf, sem, zsem):
    i = pl.program_id(0)
    n = pl.num_programs(0)
    ts = h1_ref.shape[0]
    slot = i % 2

    n_dst_tiles = xs_hbm.shape[0] // MOE_TILE

    def pad_copies():
        for c in range(N_CLASSES):
            yield has_ref[c] != 0, last_ref[c]
            yield total_ref[0] + c < n_dst_tiles, total_ref[0] + c

    def tile_copy(tile):
        return pltpu.make_async_copy(zbuf, xs_hbm.at[pl.ds(tile * MOE_TILE, MOE_TILE)], zsem)

    @pl.when(i == 0)
    def _():
        zbuf[...] = jnp.zeros(zbuf.shape, zbuf.dtype)
        for cond, tile in pad_copies():
            @pl.when(cond)
            def _():
                tile_copy(tile).start()
        for cond, tile in pad_copies():
            @pl.when(cond)
            def _():
                tile_copy(tile).wait()

    h1 = h1_ref[...]
    ms = jnp.mean(h1 * h1, axis=-1, keepdims=True)
    rows[slot, :, 0:D_MODEL] = h1 * lax.rsqrt(ms + EPS) * gffn_ref[...]
    erow = lax.broadcasted_iota(jnp.int32, (GATE_LANES, ts), 0)
    gt = (jnp.where(erow == GATE_LO_LANE, rg_ref[0:1, :], 0.0)
          + jnp.where(erow == GATE_HI_LANE, rg_ref[1:2, :], 0.0))
    rows[slot, :, D_MODEL:XG_WIDTH] = gt.T

    def row_copy(sl):
        return lambda t: pltpu.make_async_copy(
            rows.at[sl, pl.ds(t, 1)], xs_hbm.at[pl.ds(pos_ref[0, t], 1)], sem.at[sl])

    _start_rows(ts, row_copy(slot))

    @pl.when(i > 0)
    def _():
        _wait_rows(ts, row_copy(1 - slot))

    @pl.when(i == n - 1)
    def _():
        _wait_rows(ts, row_copy(slot))


def _scatter_rows(cinfo, pos, h1, g_ffn, rg, n_dst_rows):
    n_tiles_tok, _, ts = pos.shape
    return pl.pallas_call(
        _scatter_rows_body,
        grid_spec=pltpu.PrefetchScalarGridSpec(
            num_scalar_prefetch=3,
            grid=(n_tiles_tok,),
            in_specs=[
                pl.BlockSpec((None, 1, ts), lambda i, *_: (i, 0, 0), memory_space=pltpu.SMEM),
                pl.BlockSpec((ts, D_MODEL), lambda i, *_: (i, 0)),
                pl.BlockSpec((1, D_MODEL), lambda i, *_: (0, 0)),
                pl.BlockSpec((None, 8, ts), lambda i, *_: (i, 0, 0)),
            ],
            out_specs=pl.BlockSpec(memory_space=pl.ANY),
            scratch_shapes=[
                pltpu.VMEM((2, ts, XG_WIDTH), F32),
                pltpu.VMEM((MOE_TILE, XG_WIDTH), F32),
                pltpu.SemaphoreType.DMA((2,)),
                pltpu.SemaphoreType.DMA(()),
            ],
        ),
        out_shape=jax.ShapeDtypeStruct((n_dst_rows, XG_WIDTH), F32),
        compiler_params=pltpu.CompilerParams(dimension_semantics=("arbitrary",), vmem_limit_bytes=VMEM_LIMIT),
        name="scatter_rows",
    )(cinfo[:, 0], cinfo[:, 1], cinfo[0:1, 2], pos, h1, g_ffn, rg)


def _moe_body(elo_ref, ehi_ref, valid_ref, blk_ref, xs_ref, w1l_ref, w3l_ref, w2l_ref, w1h_ref, w3h_ref, w2h_ref,
              ys_ref):
    @pl.when(valid_ref[pl.program_id(0)] != 0)
    def _():
        x = xs_ref[:, 0:D_MODEL].astype(BF16)

        def expert(w1_ref, w3_ref, w2_ref):
            a = jnp.dot(x, w1_ref[...], preferred_element_type=F32)
            b = jnp.dot(x, w3_ref[...], preferred_element_type=F32)
            hdn = (a * (1.0 / (1.0 + jnp.exp(-a))) * b).astype(BF16)
            return jnp.dot(hdn, w2_ref[...], preferred_element_type=F32)

        g_lo = xs_ref[:, D_MODEL + GATE_LO_LANE:D_MODEL + GATE_LO_LANE + 1]
        g_hi = xs_ref[:, D_MODEL + GATE_HI_LANE:D_MODEL + GATE_HI_LANE + 1]
        ys_ref[...] = (g_lo * expert(w1l_ref, w3l_ref, w2l_ref)
                       + g_hi * expert(w1h_ref, w3h_ref, w2h_ref))

    @pl.when(valid_ref[pl.program_id(0)] == 0)
    def _():
        ys_ref[...] = jnp.zeros(ys_ref.shape, F32)


def _moe(sched, xs, w1, w3, w2):
    n_tiles = xs.shape[0] // MOE_TILE
    rows = lambda j, elo, ehi, valid, blk: (blk[j], 0)
    w_lo = lambda j, elo, ehi, valid, blk: (elo[j], 0, 0)
    w_hi = lambda j, elo, ehi, valid, blk: (ehi[j], 0, 0)
    up = (None, D_MODEL, D_FF)
    down = (None, D_FF, D_MODEL)
    return pl.pallas_call(
        _moe_body,
        grid_spec=pltpu.PrefetchScalarGridSpec(
            num_scalar_prefetch=4,
            grid=(n_tiles,),
            in_specs=[
                pl.BlockSpec((MOE_TILE, XG_WIDTH), rows),
                pl.BlockSpec(up, w_lo), pl.BlockSpec(up, w_lo), pl.BlockSpec(down, w_lo),
                pl.BlockSpec(up, w_hi), pl.BlockSpec(up, w_hi), pl.BlockSpec(down, w_hi),
            ],
            out_specs=pl.BlockSpec((MOE_TILE, D_MODEL), lambda j, *_: (j, 0)),
        ),
        out_shape=jax.ShapeDtypeStruct((xs.shape[0], D_MODEL), F32),
        compiler_params=pltpu.CompilerParams(dimension_semantics=("arbitrary",), vmem_limit_bytes=VMEM_LIMIT),
        name="moe",
    )(sched[0, :n_tiles], sched[1, :n_tiles], sched[2, :n_tiles], sched[3, :n_tiles], xs, w1, w3, w2, w1, w3, w2)


def _ple_body(pos_ref, posn_ref, h1_ref, ys_hbm, p_ref, gple_ref, wg_ref, wp_ref, gpost_ref, o_ref, ybuf, sem):
    i = pl.program_id(0)
    n = pl.num_programs(0)
    ts = h1_ref.shape[0]
    slot = i % 2

    def row_copy(pref, sl):
        return lambda t: pltpu.make_async_copy(
            ys_hbm.at[pl.ds(pref[0, t], 1)], ybuf.at[sl, pl.ds(t, 1)], sem.at[sl])

    @pl.when(i == 0)
    def _():
        _start_rows(ts, row_copy(pos_ref, 0))

    _wait_rows(ts, row_copy(pos_ref, slot))
    _start_rows(ts, row_copy(posn_ref, 1 - slot))

    h2 = h1_ref[...] + ybuf[slot]
    ms = jnp.mean(h2 * h2, axis=-1, keepdims=True)
    hn = (h2 * lax.rsqrt(ms + EPS) * gple_ref[...]).astype(BF16)
    z = jnp.dot(hn, wg_ref[...], preferred_element_type=F32)
    gate = 1.0 / (1.0 + jnp.exp(-z))
    pp = jnp.dot(p_ref[...].astype(BF16), wp_ref[...], preferred_element_type=F32)
    pms = jnp.mean(pp * pp, axis=-1, keepdims=True)
    pn = pp * lax.rsqrt(pms + EPS) * gpost_ref[...]
    o_ref[...] = h2 + gate * pn

    @pl.when(i == n - 1)
    def _():
        _wait_rows(ts, row_copy(posn_ref, 1 - slot))


def _ple(pos, h1, ys, p, g_ple, wg, wp, g_post):
    T = h1.shape[0]
    n_tiles_tok, _, ts = pos.shape
    row = lambda i: (i, 0)
    const = lambda i: (0, 0)
    return pl.pallas_call(
        _ple_body,
        grid=(n_tiles_tok,),
        in_specs=[
            pl.BlockSpec((None, 1, ts), lambda i: (i, 0, 0), memory_space=pltpu.SMEM),
            pl.BlockSpec((None, 1, ts), lambda i: (jnp.minimum(i + 1, n_tiles_tok - 1), 0, 0),
                         memory_space=pltpu.SMEM),
            pl.BlockSpec((ts, D_MODEL), row),
            pl.BlockSpec(memory_space=pl.ANY),
            pl.BlockSpec((ts, PLE_DIM), row),
            pl.BlockSpec((1, D_MODEL), const),
            pl.BlockSpec((D_MODEL, D_MODEL), const),
            pl.BlockSpec((PLE_DIM, D_MODEL), const),
            pl.BlockSpec((1, D_MODEL), const),
        ],
        out_specs=pl.BlockSpec((ts, D_MODEL), row),
        out_shape=jax.ShapeDtypeStruct((T, D_MODEL), F32),
        scratch_shapes=[pltpu.VMEM((2, ts, D_MODEL), F32), pltpu.SemaphoreType.DMA((2,))],
        compiler_params=pltpu.CompilerParams(
            dimension_semantics=("arbitrary",), vmem_limit_bytes=VMEM_LIMIT),
        name="ple",
    )(pos, pos, h1, ys, p, g_ple, wg, wp, g_post)


def _rope_tables(S):
    pos = jnp.arange(S)
    inv = ROPE_THETA ** (-jnp.arange(0, ROPE_DIM, 2, dtype=F32) / ROPE_DIM)
    ang = inv[:, None] * pos.astype(F32)[None, :]
    cos, sin = jnp.cos(ang), jnp.sin(ang)
    lane = np.arange(LANES) % HEAD_DIM
    freq = np.arange(ROPE_HALF)[:, None]
    first = (lane[None, :] == freq).astype(np.float32)
    second = (lane[None, :] == freq + ROPE_HALF).astype(np.float32)
    rest = (lane >= ROPE_DIM).astype(np.float32)[None, :]
    spread = lambda t, m: lax.dot_general(t, jnp.asarray(m), (((0,), (0,)), ((), ())),
                                          precision=lax.Precision.HIGHEST)
    return spread(cos, first + second) + rest, spread(sin, -first), spread(sin, second)


def _layer(h, p_i, g_mix, w_in, q_norm, k_norm, conv_w, g_attn_out, g_conv_out, w_out, g_ffn,
           w_rg, b_rg, w_re, b_re, w1, w3, w2, g_ple, w_ple_gate, w_ple_proj, g_ple_post):
    B, S, _ = h.shape
    T = B * S
    row = lambda g: g.reshape(1, -1).astype(F32)

    cos_t, sa_t, sb_t = _rope_tables(S)
    bd256 = _block_diag_ones(MXU_DIM, HEAD_DIM)
    bd128 = _block_diag_ones(LANES, HEAD_DIM)
    gq = row(jnp.tile(q_norm, N_HEADS) * (HEAD_DIM ** -0.5 * LOG2_E))
    gk = row(jnp.tile(k_norm, N_HEADS))
    cw = jnp.zeros((8, CONV_WIDTH), F32).at[0:CONV_K].set(conv_w)

    q, k, v, convn = _in_proj(h, row(g_mix), w_in.astype(BF16), gq, gk, cos_t, sa_t, sb_t, bd256, cw,
                              row(g_conv_out))
    attn = _attention(q, k, v, row(g_attn_out), bd128, _band_bias())

    w_r = jnp.zeros((D_MODEL, GATE_LANES), F32)
    w_r = w_r.at[:, ROUTER_GROUP_COL:ROUTER_GROUP_COL + N_GROUPS].set(w_rg)
    w_r = w_r.at[:, ROUTER_EXPERT_COL:ROUTER_EXPERT_COL + N_EXPERTS].set(w_re)
    b_r = jnp.zeros((1, GATE_LANES), F32)
    b_r = b_r.at[0, ROUTER_GROUP_COL:ROUTER_GROUP_COL + N_GROUPS].set(b_rg)
    b_r = b_r.at[0, ROUTER_EXPERT_COL:ROUTER_EXPERT_COL + N_EXPERTS].set(b_re)
    wr_hi = w_r.astype(BF16)
    wr_lo = (w_r - wr_hi.astype(F32)).astype(BF16)

    h1, rt, rg, cnt = _out_route(h, attn, convn, w_out.astype(BF16), row(g_ffn), wr_hi, wr_lo, b_r)
    n_tok_tiles = T // ROW_TILE
    n_sorted_tiles = T // MOE_TILE + N_CLASSES
    assert n_sorted_tiles <= SCHED_LANES
    pos, sched, cinfo = _route_plan(cnt, rt.reshape(n_tok_tiles, 8, ROW_TILE))
    h1 = h1.reshape(T, D_MODEL)
    xs = _scatter_rows(cinfo, pos, h1, row(g_ffn), rg.reshape(n_tok_tiles, 8, ROW_TILE),
                       n_sorted_tiles * MOE_TILE)
    ys = _moe(sched, xs, w1.astype(BF16), w3.astype(BF16), w2.astype(BF16))
    out = _ple(pos, h1, ys, p_i.reshape(T, PLE_DIM), row(g_ple),
               w_ple_gate.astype(BF16), w_ple_proj.astype(BF16), row(g_ple_post))
    return out.reshape(B, S, D_MODEL)


def kernel(x, p, g_mix, w_in, q_norm, k_norm, conv_w, g_attn_out, g_conv_out, w_out, g_ffn, w_router_group, b_router_group, w_router_expert, b_router_expert, w1, w3, w2, g_ple, w_ple_gate, w_ple_proj, g_ple_post):
    h = x
    for i in range(p.shape[0]):
        h = _layer(h, p[i], g_mix[i], w_in[i], q_norm[i], k_norm[i], conv_w[i], g_attn_out[i], g_conv_out[i],
                   w_out[i], g_ffn[i], w_router_group[i], b_router_group[i], w_router_expert[i],
                   b_router_expert[i], w1[i], w3[i], w2[i], g_ple[i], w_ple_gate[i], w_ple_proj[i],
                   g_ple_post[i])
    return h
```

```python
import functools

import numpy as np
import jax
import jax.numpy as jnp
from jax import lax
from jax.experimental import pallas as pl
from jax.experimental.pallas import tpu as pltpu

F32 = jnp.float32
BF16 = jnp.bfloat16

D_MODEL = 1024
PLE_DIM = 256
HEAD_DIM = 64
N_HEADS = 8
ATTN_WIDTH = N_HEADS * HEAD_DIM
CONV_WIDTH = D_MODEL - ATTN_WIDTH
CONV_K = 3
DILATIONS = (1, 4, 16)
N_BACK = 128
ATTN_BLOCK = 128
ROPE_THETA = 500000.0
ROPE_DIM = HEAD_DIM // 4
ROPE_HALF = ROPE_DIM // 2
N_GROUPS = 4
EXPERTS_PER_GROUP = 4
N_EXPERTS = N_GROUPS * EXPERTS_PER_GROUP
D_FF = 512
EPS = 1e-6

LANES = 128
MXU_DIM = 256
NEG_BIG = -1e30
LOG2_E = 1.4426950408889634

ROW_TILE = 512
ATTN_TILE = 2048
ATTN_UNROLL = 16
MOE_TILE = 256
ROW_SUBTILES = D_MODEL // LANES
GATE_LANES = LANES
SLAB_ROWS = 16
GATE_LO_LANE = 0
GATE_HI_LANE = 1
EXPERT_PAIRS = tuple((a, b) for a in range(EXPERTS_PER_GROUP) for b in range(a + 1, EXPERTS_PER_GROUP))
N_CLASSES = N_GROUPS * len(EXPERT_PAIRS)
CLASS_ROWS = 32
SCHED_LANES = 256
ROUTER_GROUP_COL = 0
ROUTER_EXPERT_COL = 8
VMEM_LIMIT = 48 * 1024 * 1024


def _block_diag_ones(n, seg):
    idx = np.arange(n) // seg
    return jnp.asarray((idx[:, None] == idx[None, :]).astype(np.float32), dtype=BF16)


def _segment_mean_sq(t, bd):
    t2 = (t * t).astype(BF16)
    parts = [jnp.dot(t2[:, i:i + MXU_DIM], bd, preferred_element_type=F32)
             for i in range(0, t.shape[1], MXU_DIM)]
    return jnp.concatenate(parts, axis=1) * (1.0 / HEAD_DIM)


def _in_proj_body(x_ref, gmix_ref, win_ref, gq_ref, gk_ref, cos_ref, sa_ref, sb_ref, bd_ref, cw_ref, gconv_ref,
                  q_ref, k_ref, v_ref, conv_ref, ubuf):
    ts = x_ref.shape[0]
    x = x_ref[...]
    ms = jnp.mean(x * x, axis=-1, keepdims=True)
    xn = (x * lax.rsqrt(ms + EPS) * gmix_ref[...]).astype(BF16)
    bd = bd_ref[...]

    def proj(c):
        return jnp.dot(xn, win_ref[:, c * ATTN_WIDTH:(c + 1) * ATTN_WIDTH], preferred_element_type=F32)

    reps = ATTN_WIDTH // LANES
    cos = jnp.concatenate([cos_ref[...]] * reps, axis=1)
    sa = jnp.concatenate([sa_ref[...]] * reps, axis=1)
    sb = jnp.concatenate([sb_ref[...]] * reps, axis=1)

    def norm_rope(t, g):
        tn = t * lax.rsqrt(_segment_mean_sq(t, bd) + EPS) * g
        up = pltpu.roll(tn, ATTN_WIDTH - ROPE_HALF, 1)
        dn = pltpu.roll(tn, ROPE_HALF, 1)
        return tn * cos + up * sa + dn * sb

    q_ref[...] = norm_rope(proj(0), gq_ref[...]).astype(BF16)
    k_ref[...] = norm_rope(proj(1), gk_ref[...]).astype(BF16)
    v_ref[...] = proj(2).astype(BF16)

    cb = proj(3)
    u = proj(4) * proj(5)

    @pl.when(pl.program_id(1) == 0)
    def _():
        ubuf[0:8, :] = jnp.zeros((8, CONV_WIDTH), F32)

    ubuf[8:8 + ts, :] = u
    u1 = ubuf[7:7 + ts, :]
    u2 = ubuf[6:6 + ts, :]
    y = cw_ref[0:1, :] * u2 + cw_ref[1:2, :] * u1 + cw_ref[2:3, :] * u
    ubuf[0:8, :] = ubuf[ts:ts + 8, :]
    conv = cb * y
    convn = conv * lax.rsqrt(_segment_mean_sq(conv, bd) + EPS) * gconv_ref[...]
    conv_ref[...] = convn.astype(BF16)


def _in_proj(x, g_mix, w_in, gq, gk, cos_t, sa_t, sb_t, bd, cw, g_conv):
    B, S, _ = x.shape
    ts = ROW_TILE
    row = lambda b, j: (b, j, 0)
    const2 = lambda b, j: (0, 0)
    tab = lambda b, j: (j, 0)
    out_sds = jax.ShapeDtypeStruct((B, S, ATTN_WIDTH), BF16)
    return pl.pallas_call(
        _in_proj_body,
        grid=(B, S // ts),
        in_specs=[
            pl.BlockSpec((None, ts, D_MODEL), row),
            pl.BlockSpec((1, D_MODEL), const2),
            pl.BlockSpec((D_MODEL, 6 * ATTN_WIDTH), const2),
            pl.BlockSpec((1, ATTN_WIDTH), const2),
            pl.BlockSpec((1, ATTN_WIDTH), const2),
            pl.BlockSpec((ts, LANES), tab),
            pl.BlockSpec((ts, LANES), tab),
            pl.BlockSpec((ts, LANES), tab),
            pl.BlockSpec((MXU_DIM, MXU_DIM), const2),
            pl.BlockSpec((8, CONV_WIDTH), const2),
            pl.BlockSpec((1, CONV_WIDTH), const2),
        ],
        out_specs=[pl.BlockSpec((None, ts, ATTN_WIDTH), row)] * 4,
        out_shape=[out_sds] * 4,
        scratch_shapes=[pltpu.VMEM((ts + 8, CONV_WIDTH), F32)],
        compiler_params=pltpu.CompilerParams(
            dimension_semantics=("arbitrary", "arbitrary"), vmem_limit_bytes=VMEM_LIMIT),
        name="in_proj",
    )(x, g_mix, w_in, gq, gk, cos_t, sa_t, sb_t, bd, cw, g_conv)


def _attn_body(q_ref, kp_ref, kc_ref, vp_ref, vc_ref, gat_ref, bd_ref, bias_ref, o_ref,
               qf, kf, vf, acc_a, acc_b, m_a, m_b):
    tq = q_ref.shape[0]
    blk = ATTN_BLOCK
    first_tile = pl.program_id(2) == 0

    qf[...] = q_ref[...].astype(F32)
    kf[0:tq, :] = kp_ref[...].astype(F32)
    kf[tq:2 * tq, :] = kc_ref[...].astype(F32)
    vf[0:tq, :] = vp_ref[...].astype(F32)
    vf[tq:2 * tq, :] = vc_ref[...].astype(F32)

    lane = lax.broadcasted_iota(jnp.int32, (1, LANES), 1)
    sel_a = (lane < HEAD_DIM).astype(F32).astype(BF16)
    sel_b = (lane >= HEAD_DIM).astype(F32).astype(BF16)

    order = DILATIONS[::-1]
    for d in order:
        nblk = tq // (blk * d)
        shift = nblk.bit_length() - 1
        first_visit = d == order[0]

        def block(idx, carry, d=d, nblk=nblk, shift=shift, first_visit=first_visit):
            r = idx >> shift
            n = idx & (nblk - 1)
            rows_q = pl.ds(r + d * blk * n, blk, stride=d)
            rows_k = pl.ds(tq + d * blk * (n - 1) + r, 2 * blk, stride=d)
            qw = qf[rows_q, :].astype(BF16)
            kw = kf[rows_k, :].astype(BF16)
            vw = vf[rows_k, :].astype(BF16)
            bias = bias_ref[jnp.where(jnp.logical_and(first_tile, n == 0), 1, 0)]

            def one_head(sel_q, sel_one, acc, mst):
                s = lax.dot_general(qw * sel_q, kw, (((1,), (1,)), ((), ())), preferred_element_type=F32) + bias
                mb = jnp.max(s, axis=-1, keepdims=True)
                if first_visit:
                    m_new = jnp.broadcast_to(mb, (blk, LANES))
                else:
                    m_old = mst[rows_q, :]
                    m_new = jnp.maximum(m_old, mb)
                e = jnp.exp2(s - jnp.concatenate([m_new, m_new], axis=1))
                pv = jnp.dot(e.astype(BF16), vw * sel_q + sel_one, preferred_element_type=F32)
                if first_visit:
                    acc[rows_q, :] = pv
                else:
                    acc[rows_q, :] = acc[rows_q, :] * jnp.exp2(m_old - m_new) + pv
                mst[rows_q, :] = m_new

            one_head(sel_a, sel_b, acc_a, m_a)
            one_head(sel_b, sel_a, acc_b, m_b)
            return carry

        lax.fori_loop(0, tq // blk, block, 0, unroll=ATTN_UNROLL)

    head_a = lax.broadcasted_iota(jnp.int32, (tq, LANES), 1) < HEAD_DIM
    aa = acc_a[...]
    ab = acc_b[...]
    num = jnp.where(head_a, aa, ab)
    den = pltpu.roll(jnp.where(head_a, ab, aa), HEAD_DIM, 1)
    o = num / den
    o2 = (o * o).astype(BF16)
    msq = jnp.dot(o2, bd_ref[...], preferred_element_type=F32) * (1.0 / HEAD_DIM)
    o_ref[...] = (o * lax.rsqrt(msq + EPS) * gat_ref[...]).astype(BF16)


def _attention(q, k, v, g_attn, bd, bias):
    B, S, _ = q.shape
    tq = ATTN_TILE
    n_pairs = ATTN_WIDTH // LANES
    cur = lambda b, hp, j: (b, j, hp)
    prev = lambda b, hp, j: (b, jnp.maximum(j - 1, 0), hp)
    blk = (None, tq, LANES)
    return pl.pallas_call(
        _attn_body,
        grid=(B, n_pairs, S // tq),
        in_specs=[
            pl.BlockSpec(blk, cur),
            pl.BlockSpec(blk, prev),
            pl.BlockSpec(blk, cur),
            pl.BlockSpec(blk, prev),
            pl.BlockSpec(blk, cur),
            pl.BlockSpec((1, LANES), lambda b, hp, j: (0, hp)),
            pl.BlockSpec((LANES, LANES), lambda b, hp, j: (0, 0)),
            pl.BlockSpec((2, ATTN_BLOCK, 2 * ATTN_BLOCK), lambda b, hp, j: (0, 0, 0)),
        ],
        out_specs=pl.BlockSpec(blk, cur),
        out_shape=jax.ShapeDtypeStruct((B, S, ATTN_WIDTH), BF16),
        scratch_shapes=[
            pltpu.VMEM((tq, LANES), F32),
            pltpu.VMEM((2 * tq, LANES), F32), pltpu.VMEM((2 * tq, LANES), F32),
            pltpu.VMEM((tq, LANES), F32), pltpu.VMEM((tq, LANES), F32),
            pltpu.VMEM((tq, LANES), F32), pltpu.VMEM((tq, LANES), F32),
        ],
        compiler_params=pltpu.CompilerParams(
            dimension_semantics=("arbitrary", "arbitrary", "arbitrary"), vmem_limit_bytes=VMEM_LIMIT),
        name="attention",
    )(q, k, k, v, v, g_attn, bd, bias)


def _band_bias():
    qi = np.arange(ATTN_BLOCK)[:, None]
    ki = np.arange(2 * ATTN_BLOCK)[None, :]
    dist = qi + ATTN_BLOCK - ki
    band = (dist >= 0) & (dist <= N_BACK)
    no_prev = band & (ki >= ATTN_BLOCK)
    tab = np.stack([band, no_prev]).astype(np.float32)
    return jnp.asarray((1.0 - tab) * NEG_BIG, dtype=F32)


def _first_argmax(vals):
    best = vals[0]
    idx = jnp.zeros(best.shape, jnp.int32)
    for i in range(1, len(vals)):
        upd = vals[i] > best
        idx = jnp.where(upd, i, idx)
        best = jnp.where(upd, vals[i], best)
    return best, idx


def _out_route_body(x_ref, attn_ref, conv_ref, wo_ref, gffn_ref, wr_ref, br_ref, upper_ref,
                    h1_ref, rt_ref, rg_ref, cnt_ref, cnt):
    ts = x_ref.shape[0]
    h1 = (x_ref[...]
          + jnp.dot(attn_ref[...], wo_ref[0:ATTN_WIDTH, :], preferred_element_type=F32)
          + jnp.dot(conv_ref[...], wo_ref[ATTN_WIDTH:, :], preferred_element_type=F32))
    h1_ref[...] = h1
    ms = jnp.mean(h1 * h1, axis=-1, keepdims=True)
    xn = h1 * lax.rsqrt(ms + EPS) * gffn_ref[...]

    logits = jnp.dot(xn.astype(BF16), wr_ref[...], preferred_element_type=F32) + br_ref[...]
    lt = logits.T

    lg = [lt[ROUTER_GROUP_COL + i:ROUTER_GROUP_COL + i + 1, :] for i in range(N_GROUPS)]
    gbest, gi = _first_argmax(lg)
    sumexp = lg[0] * 0.0
    for i in range(N_GROUPS):
        sumexp = sumexp + jnp.exp(lg[i] - gbest)
    pg_top = 1.0 / sumexp

    sel = []
    for jx in range(EXPERTS_PER_GROUP):
        cand = [lt[ROUTER_EXPERT_COL + EXPERTS_PER_GROUP * g + jx:ROUTER_EXPERT_COL + EXPERTS_PER_GROUP * g + jx + 1, :]
                for g in range(N_GROUPS)]
        vj = cand[N_GROUPS - 1]
        for g in range(N_GROUPS - 2, -1, -1):
            vj = jnp.where(gi == g, cand[g], vj)
        sel.append(vj)
    b1, i1 = _first_argmax(sel)
    masked = [jnp.where(i1 == jx, -jnp.inf, sel[jx]) for jx in range(EXPERTS_PER_GROUP)]
    b2, i2 = _first_argmax(masked)
    e2 = jnp.exp(b2 - b1)
    t1 = 1.0 / (1.0 + e2)
    ga = pg_top * t1
    gb = pg_top * (e2 * t1)
    lo = jnp.minimum(i1, i2)
    hi = jnp.maximum(i1, i2)
    g_lo = jnp.where(i1 < i2, ga, gb)
    g_hi = jnp.where(i1 < i2, gb, ga)
    pair = hi - lo - 1
    for a in range(1, EXPERTS_PER_GROUP - 1):
        pair = pair + jnp.where(lo >= a, EXPERT_PAIRS.index((a, a + 1)) - EXPERT_PAIRS.index((a - 1, a)), 0)
    cls = gi * len(EXPERT_PAIRS) + pair

    r8 = lax.broadcasted_iota(jnp.int32, (8, ts), 0)
    rg_ref[...] = jnp.where(r8 == 0, g_lo, jnp.where(r8 == 1, g_hi, 0.0))

    @pl.when(jnp.logical_and(pl.program_id(0) == 0, pl.program_id(1) == 0))
    def _():
        cnt[...] = jnp.zeros(cnt.shape, F32)

    crow = lax.broadcasted_iota(jnp.int32, (CLASS_ROWS, ts), 0)
    oh = (crow == cls).astype(F32)
    before = jnp.dot(oh.astype(BF16), upper_ref[...], preferred_element_type=F32)
    rank = jnp.sum(oh * (before + cnt[:, 0:1]), axis=0, keepdims=True)
    cnt[...] = cnt[...] + jnp.sum(oh, axis=1, keepdims=True)
    cnt_ref[...] = cnt[...]
    rt_ref[...] = jnp.where(r8 == 0, cls, jnp.where(r8 == 1, rank.astype(jnp.int32), 0))


def _out_route(x, attn, conv, w_out, g_ffn, wr, b_r):
    B, S, _ = x.shape
    ts = ROW_TILE
    row = lambda b, j: (b, j, 0)
    const2 = lambda b, j: (0, 0)
    tok = np.arange(ts)
    upper = jnp.asarray((tok[:, None] < tok[None, :]).astype(np.float32), dtype=BF16)
    return pl.pallas_call(
        _out_route_body,
        grid=(B, S // ts),
        in_specs=[
            pl.BlockSpec((None, ts, D_MODEL), row),
            pl.BlockSpec((None, ts, ATTN_WIDTH), row),
            pl.BlockSpec((None, ts, CONV_WIDTH), row),
            pl.BlockSpec((D_MODEL, D_MODEL), const2),
            pl.BlockSpec((1, D_MODEL), const2),
            pl.BlockSpec((D_MODEL, GATE_LANES), const2),
            pl.BlockSpec((1, GATE_LANES), const2),
            pl.BlockSpec((ts, ts), const2),
        ],
        out_specs=[
            pl.BlockSpec((None, ts, D_MODEL), row),
            pl.BlockSpec((None, None, 8, ts), lambda b, j: (b, j, 0, 0)),
            pl.BlockSpec((None, None, 8, ts), lambda b, j: (b, j, 0, 0)),
            pl.BlockSpec((CLASS_ROWS, LANES), const2),
        ],
        out_shape=[
            jax.ShapeDtypeStruct((B, S, D_MODEL), F32),
            jax.ShapeDtypeStruct((B, S // ts, 8, ts), jnp.int32),
            jax.ShapeDtypeStruct((B, S // ts, 8, ts), F32),
            jax.ShapeDtypeStruct((CLASS_ROWS, LANES), F32),
        ],
        scratch_shapes=[pltpu.VMEM((CLASS_ROWS, LANES), F32)],
        compiler_params=pltpu.CompilerParams(
            dimension_semantics=("arbitrary", "arbitrary"), vmem_limit_bytes=VMEM_LIMIT),
        name="out_route",
    )(x, attn, conv, w_out, g_ffn, wr, b_r, upper)


def _route_plan_body(cnt_ref, rt_ref, elo_ref, ehi_ref, pos_ref, sched_ref, cinfo_ref):
    n_tiles_tok, _, ts = pos_ref.shape
    shift = MOE_TILE.bit_length() - 1
    ntile = (cnt_ref[...].astype(jnp.int32) + (MOE_TILE - 1)) >> shift
    crow = lax.broadcasted_iota(jnp.int32, (CLASS_ROWS, LANES), 0)
    incl = ntile
    step = 1
    while step < CLASS_ROWS:
        incl = incl + jnp.where(crow >= step, pltpu.roll(incl, step, 0), 0)
        step *= 2
    tstart = incl - ntile
    total = incl[CLASS_ROWS - 1:CLASS_ROWS, 0:1]

    row_base = jnp.broadcast_to(tstart[:, 0:1] * MOE_TILE, (CLASS_ROWS, ts))
    crow_t = lax.broadcasted_iota(jnp.int32, (CLASS_ROWS, ts), 0)

    def token_tile(i, carry):
        cls = rt_ref[i, 0:1, :]
        rank = rt_ref[i, 1:2, :]
        base = jnp.sum(jnp.where(crow_t == cls, row_base, 0), axis=0, keepdims=True)
        pos_ref[i] = base + rank
        return carry

    lax.fori_loop(0, n_tiles_tok, token_tile, 0)

    lane_j = lax.broadcasted_iota(jnp.int32, (CLASS_ROWS, SCHED_LANES), 1)
    start_b = jnp.broadcast_to(tstart[:, 0:1], (CLASS_ROWS, SCHED_LANES))
    ntile_b = jnp.broadcast_to(ntile[:, 0:1], (CLASS_ROWS, SCHED_LANES))
    member = jnp.logical_and(lane_j >= start_b, lane_j < start_b + ntile_b)
    pick = lambda tab: jnp.sum(jnp.where(member, jnp.broadcast_to(tab[:, 0:1], member.shape), 0),
                               axis=0, keepdims=True)
    valid = jnp.sum(member.astype(jnp.int32), axis=0, keepdims=True)
    elo = pick(elo_ref[...])
    ehi = pick(ehi_ref[...])
    j1 = lane_j[0:1, :]
    last = total - 1
    at_last = j1 == last
    elo_last = jnp.sum(jnp.where(at_last, elo, 0), axis=1, keepdims=True)
    ehi_last = jnp.sum(jnp.where(at_last, ehi, 0), axis=1, keepdims=True)
    in_use = valid > 0
    elo = jnp.where(in_use, elo, elo_last)
    ehi = jnp.where(in_use, ehi, ehi_last)
    blk = jnp.minimum(j1, last)
    r8 = lax.broadcasted_iota(jnp.int32, (8, SCHED_LANES), 0)
    sched_ref[...] = jnp.where(r8 == 0, elo, jnp.where(r8 == 1, ehi, jnp.where(r8 == 2, valid,
                               jnp.where(r8 == 3, blk, 0))))
    lane_c = lax.broadcasted_iota(jnp.int32, (CLASS_ROWS, LANES), 1)
    cinfo_ref[...] = jnp.where(lane_c == 0, tstart + ntile - 1,
                               jnp.where(lane_c == 1, (ntile > 0).astype(jnp.int32), total))


def _route_plan(cnt, rt):
    n_tiles_tok, _, ts = rt.shape
    elo_tab = np.zeros((CLASS_ROWS, LANES), np.int32)
    ehi_tab = np.zeros((CLASS_ROWS, LANES), np.int32)
    for g in range(N_GROUPS):
        for p, (a, b) in enumerate(EXPERT_PAIRS):
            elo_tab[g * len(EXPERT_PAIRS) + p, :] = g * EXPERTS_PER_GROUP + a
            ehi_tab[g * len(EXPERT_PAIRS) + p, :] = g * EXPERTS_PER_GROUP + b
    full = lambda shape: pl.BlockSpec(shape, lambda i: (0,) * len(shape))
    return pl.pallas_call(
        _route_plan_body,
        grid=(1,),
        in_specs=[full((CLASS_ROWS, LANES)), full(rt.shape), full((CLASS_ROWS, LANES)), full((CLASS_ROWS, LANES))],
        out_specs=[full((n_tiles_tok, 1, ts)), full((8, SCHED_LANES)), full((CLASS_ROWS, LANES))],
        out_shape=[
            jax.ShapeDtypeStruct((n_tiles_tok, 1, ts), jnp.int32),
            jax.ShapeDtypeStruct((8, SCHED_LANES), jnp.int32),
            jax.ShapeDtypeStruct((CLASS_ROWS, LANES), jnp.int32),
        ],
        compiler_params=pltpu.CompilerParams(dimension_semantics=("arbitrary",)),
        name="route_plan",
    )(cnt, rt, jnp.asarray(elo_tab), jnp.asarray(ehi_tab))


DMA_UNROLL = 8


def _start_rows(n_rows, make_copy):
    for t in range(n_rows):
        make_copy(t).start(priority=t % 2)


def _wait_rows(n_rows, make_copy):
    def trip(t8, carry):
        for _ in range(DMA_UNROLL):
            make_copy(0).wait()
        return carry
    lax.fori_loop(0, n_rows // DMA_UNROLL, trip, 0)


def _scatter_rows_body(last_ref, has_ref, total_ref, pos_ref, h1_ref, gffn_ref, rg_ref, xs_hbm,
                       rows, zbuf, sem, zsem):
    i = pl.program_id(0)
    n = pl.num_programs(0)
    ts = h1_ref.shape[0]
    slot = i % 2
    tile_rows = MOE_TILE * SLAB_ROWS

    n_dst_tiles = xs_hbm.shape[0] // tile_rows

    def pad_copies():
        for c in range(N_CLASSES):
            yield has_ref[c] != 0, last_ref[c]
            yield total_ref[0] + c < n_dst_tiles, total_ref[0] + c

    def tile_copy(tile):
        return pltpu.make_async_copy(zbuf, xs_hbm.at[pl.ds(tile * tile_rows, tile_rows)], zsem)

    @pl.when(i == 0)
    def _():
        rows[...] = jnp.zeros(rows.shape, rows.dtype)
        zbuf[...] = jnp.zeros(zbuf.shape, zbuf.dtype)
        for cond, tile in pad_copies():
            @pl.when(cond)
            def _():
                tile_copy(tile).start()
        for cond, tile in pad_copies():
            @pl.when(cond)
            def _():
                tile_copy(tile).wait()

    h1 = h1_ref[...]
    ms = jnp.mean(h1 * h1, axis=-1, keepdims=True)
    xn = h1 * lax.rsqrt(ms + EPS) * gffn_ref[...]
    for k in range(ROW_SUBTILES):
        rows[slot, pl.ds(k, ts, stride=SLAB_ROWS), :] = xn[:, k * LANES:(k + 1) * LANES]
    erow = lax.broadcasted_iota(jnp.int32, (GATE_LANES, ts), 0)
    gt = (jnp.where(erow == GATE_LO_LANE, rg_ref[0:1, :], 0.0)
          + jnp.where(erow == GATE_HI_LANE, rg_ref[1:2, :], 0.0))
    rows[slot, pl.ds(ROW_SUBTILES, ts, stride=SLAB_ROWS), :] = gt.T

    def row_copy(sl):
        return lambda t: pltpu.make_async_copy(
            rows.at[sl, pl.ds(t * SLAB_ROWS, SLAB_ROWS)],
            xs_hbm.at[pl.ds(pl.multiple_of(pos_ref[0, t] * SLAB_ROWS, SLAB_ROWS), SLAB_ROWS)], sem.at[sl])

    _start_rows(ts, row_copy(slot))

    @pl.when(i > 0)
    def _():
        _wait_rows(ts, row_copy(1 - slot))

    @pl.when(i == n - 1)
    def _():
        _wait_rows(ts, row_copy(slot))


def _scatter_rows(cinfo, pos, h1, g_ffn, rg, n_dst_rows):
    n_tiles_tok, _, ts = pos.shape
    return pl.pallas_call(
        _scatter_rows_body,
        grid_spec=pltpu.PrefetchScalarGridSpec(
            num_scalar_prefetch=3,
            grid=(n_tiles_tok,),
            in_specs=[
                pl.BlockSpec((None, 1, ts), lambda i, *_: (i, 0, 0), memory_space=pltpu.SMEM),
                pl.BlockSpec((ts, D_MODEL), lambda i, *_: (i, 0)),
                pl.BlockSpec((1, D_MODEL), lambda i, *_: (0, 0)),
                pl.BlockSpec((None, 8, ts), lambda i, *_: (i, 0, 0)),
            ],
            out_specs=pl.BlockSpec(memory_space=pl.ANY),
            scratch_shapes=[
                pltpu.VMEM((2, ts * SLAB_ROWS, LANES), F32),
                pltpu.VMEM((MOE_TILE * SLAB_ROWS, LANES), F32),
                pltpu.SemaphoreType.DMA((2,)),
                pltpu.SemaphoreType.DMA(()),
            ],
        ),
        out_shape=jax.ShapeDtypeStruct((n_dst_rows * SLAB_ROWS, LANES), F32),
        compiler_params=pltpu.CompilerParams(dimension_semantics=("arbitrary",), vmem_limit_bytes=VMEM_LIMIT),
        name="scatter_rows",
    )(cinfo[:, 0], cinfo[:, 1], cinfo[0:1, 2], pos, h1, g_ffn, rg)


def _moe_body(elo_ref, ehi_ref, valid_ref, blk_ref, xs_ref, w1l_ref, w3l_ref, w2l_ref, w1h_ref, w3h_ref, w2h_ref,
              ys_ref):
    @pl.when(valid_ref[pl.program_id(0)] != 0)
    def _():
        slab_row = lambda k: xs_ref[pl.ds(k, MOE_TILE, stride=SLAB_ROWS), :]
        x = jnp.concatenate([slab_row(k).astype(BF16) for k in range(ROW_SUBTILES)], axis=1)
        gates = slab_row(ROW_SUBTILES)

        def expert(w1_ref, w3_ref, w2_ref):
            a = jnp.dot(x, w1_ref[...], preferred_element_type=F32)
            b = jnp.dot(x, w3_ref[...], preferred_element_type=F32)
            hdn = (a * (1.0 / (1.0 + jnp.exp(-a))) * b).astype(BF16)
            return jnp.dot(hdn, w2_ref[...], preferred_element_type=F32)

        g_lo = gates[:, GATE_LO_LANE:GATE_LO_LANE + 1]
        g_hi = gates[:, GATE_HI_LANE:GATE_HI_LANE + 1]
        y = (g_lo * expert(w1l_ref, w3l_ref, w2l_ref)
             + g_hi * expert(w1h_ref, w3h_ref, w2h_ref))
        for k in range(ROW_SUBTILES):
            ys_ref[pl.ds(k, MOE_TILE, stride=ROW_SUBTILES), :] = y[:, k * LANES:(k + 1) * LANES]

    @pl.when(valid_ref[pl.program_id(0)] == 0)
    def _():
        ys_ref[...] = jnp.zeros(ys_ref.shape, F32)


def _moe(sched, xs, w1, w3, w2):
    n_tiles = xs.shape[0] // (MOE_TILE * SLAB_ROWS)
    rows = lambda j, elo, ehi, valid, blk: (blk[j], 0)
    w_lo = lambda j, elo, ehi, valid, blk: (elo[j], 0, 0)
    w_hi = lambda j, elo, ehi, valid, blk: (ehi[j], 0, 0)
    up = (None, D_MODEL, D_FF)
    down = (None, D_FF, D_MODEL)
    return pl.pallas_call(
        _moe_body,
        grid_spec=pltpu.PrefetchScalarGridSpec(
            num_scalar_prefetch=4,
            grid=(n_tiles,),
            in_specs=[
                pl.BlockSpec((MOE_TILE * SLAB_ROWS, LANES), rows),
                pl.BlockSpec(up, w_lo), pl.BlockSpec(up, w_lo), pl.BlockSpec(down, w_lo),
                pl.BlockSpec(up, w_hi), pl.BlockSpec(up, w_hi), pl.BlockSpec(down, w_hi),
            ],
            out_specs=pl.BlockSpec((MOE_TILE * ROW_SUBTILES, LANES), lambda j, *_: (j, 0)),
        ),
        out_shape=jax.ShapeDtypeStruct((n_tiles * MOE_TILE * ROW_SUBTILES, LANES), F32),
        compiler_params=pltpu.CompilerParams(dimension_semantics=("arbitrary",), vmem_limit_bytes=VMEM_LIMIT),
        name="moe",
    )(sched[0, :n_tiles], sched[1, :n_tiles], sched[2, :n_tiles], sched[3, :n_tiles], xs, w1, w3, w2, w1, w3, w2)


def _ple_body(pos_ref, posn_ref, h1_ref, ys_hbm, p_ref, gple_ref, wg_ref, wp_ref, gpost_ref, o_ref, ybuf, sem):
    i = pl.program_id(0)
    n = pl.num_programs(0)
    ts = h1_ref.shape[0]
    slot = i % 2

    def row_copy(pref, sl):
        return lambda t: pltpu.make_async_copy(
            ys_hbm.at[pl.ds(pl.multiple_of(pref[0, t] * ROW_SUBTILES, ROW_SUBTILES), ROW_SUBTILES)],
            ybuf.at[sl, pl.ds(t * ROW_SUBTILES, ROW_SUBTILES)], sem.at[sl])

    @pl.when(i == 0)
    def _():
        _start_rows(ts, row_copy(pos_ref, 0))

    _wait_rows(ts, row_copy(pos_ref, slot))
    _start_rows(ts, row_copy(posn_ref, 1 - slot))

    y = jnp.concatenate([ybuf[slot, pl.ds(k, ts, stride=ROW_SUBTILES), :] for k in range(ROW_SUBTILES)], axis=1)
    h2 = h1_ref[...] + y
    ms = jnp.mean(h2 * h2, axis=-1, keepdims=True)
    hn = (h2 * lax.rsqrt(ms + EPS) * gple_ref[...]).astype(BF16)
    z = jnp.dot(hn, wg_ref[...], preferred_element_type=F32)
    gate = 1.0 / (1.0 + jnp.exp(-z))
    pp = jnp.dot(p_ref[...].astype(BF16), wp_ref[...], preferred_element_type=F32)
    pms = jnp.mean(pp * pp, axis=-1, keepdims=True)
    pn = pp * lax.rsqrt(pms + EPS) * gpost_ref[...]
    o_ref[...] = h2 + gate * pn

    @pl.when(i == n - 1)
    def _():
        _wait_rows(ts, row_copy(posn_ref, 1 - slot))


def _ple(pos, h1, ys, p, g_ple, wg, wp, g_post):
    T = h1.shape[0]
    n_tiles_tok, _, ts = pos.shape
    row = lambda i: (i, 0)
    const = lambda i: (0, 0)
    return pl.pallas_call(
        _ple_body,
        grid=(n_tiles_tok,),
        in_specs=[
            pl.BlockSpec((None, 1, ts), lambda i: (i, 0, 0), memory_space=pltpu.SMEM),
            pl.BlockSpec((None, 1, ts), lambda i: (jnp.minimum(i + 1, n_tiles_tok - 1), 0, 0),
                         memory_space=pltpu.SMEM),
            pl.BlockSpec((ts, D_MODEL), row),
            pl.BlockSpec(memory_space=pl.ANY),
            pl.BlockSpec((ts, PLE_DIM), row),
            pl.BlockSpec((1, D_MODEL), const),
            pl.BlockSpec((D_MODEL, D_MODEL), const),
            pl.BlockSpec((PLE_DIM, D_MODEL), const),
            pl.BlockSpec((1, D_MODEL), const),
        ],
        out_specs=pl.BlockSpec((ts, D_MODEL), row),
        out_shape=jax.ShapeDtypeStruct((T, D_MODEL), F32),
        scratch_shapes=[pltpu.VMEM((2, ts * ROW_SUBTILES, LANES), F32), pltpu.SemaphoreType.DMA((2,))],
        compiler_params=pltpu.CompilerParams(
            dimension_semantics=("arbitrary",), vmem_limit_bytes=VMEM_LIMIT),
        name="ple",
    )(pos, pos, h1, ys, p, g_ple, wg, wp, g_post)


def _rope_tables(S):
    pos = jnp.arange(S)
    inv = ROPE_THETA ** (-jnp.arange(0, ROPE_DIM, 2, dtype=F32) / ROPE_DIM)
    ang = inv[:, None] * pos.astype(F32)[None, :]
    cos, sin = jnp.cos(ang), jnp.sin(ang)
    lane = np.arange(LANES) % HEAD_DIM
    freq = np.arange(ROPE_HALF)[:, None]
    first = (lane[None, :] == freq).astype(np.float32)
    second = (lane[None, :] == freq + ROPE_HALF).astype(np.float32)
    rest = (lane >= ROPE_DIM).astype(np.float32)[None, :]
    spread = lambda t, m: lax.dot_general(t, jnp.asarray(m), (((0,), (0,)), ((), ())),
                                          precision=lax.Precision.HIGHEST)
    return spread(cos, first + second) + rest, spread(sin, -first), spread(sin, second)


def _layer(h, p_i, g_mix, w_in, q_norm, k_norm, conv_w, g_attn_out, g_conv_out, w_out, g_ffn,
           w_rg, b_rg, w_re, b_re, w1, w3, w2, g_ple, w_ple_gate, w_ple_proj, g_ple_post):
    B, S, _ = h.shape
    T = B * S
    row = lambda g: g.reshape(1, -1).astype(F32)

    cos_t, sa_t, sb_t = _rope_tables(S)
    bd256 = _block_diag_ones(MXU_DIM, HEAD_DIM)
    bd128 = _block_diag_ones(LANES, HEAD_DIM)
    gq = row(jnp.tile(q_norm, N_HEADS) * (HEAD_DIM ** -0.5 * LOG2_E))
    gk = row(jnp.tile(k_norm, N_HEADS))
    cw = jnp.zeros((8, CONV_WIDTH), F32).at[0:CONV_K].set(conv_w)

    q, k, v, convn = _in_proj(h, row(g_mix), w_in.astype(BF16), gq, gk, cos_t, sa_t, sb_t, bd256, cw,
                              row(g_conv_out))
    attn = _attention(q, k, v, row(g_attn_out), bd128, _band_bias())

    w_r = jnp.zeros((D_MODEL, GATE_LANES), F32)
    w_r = w_r.at[:, ROUTER_GROUP_COL:ROUTER_GROUP_COL + N_GROUPS].set(w_rg)
    w_r = w_r.at[:, ROUTER_EXPERT_COL:ROUTER_EXPERT_COL + N_EXPERTS].set(w_re)
    b_r = jnp.zeros((1, GATE_LANES), F32)
    b_r = b_r.at[0, ROUTER_GROUP_COL:ROUTER_GROUP_COL + N_GROUPS].set(b_rg)
    b_r = b_r.at[0, ROUTER_EXPERT_COL:ROUTER_EXPERT_COL + N_EXPERTS].set(b_re)
    wr = w_r.astype(BF16)

    h1, rt, rg, cnt = _out_route(h, attn, convn, w_out.astype(BF16), row(g_ffn), wr, b_r)
    n_tok_tiles = T // ROW_TILE
    n_sorted_tiles = T // MOE_TILE + N_CLASSES
    assert n_sorted_tiles <= SCHED_LANES
    pos, sched, cinfo = _route_plan(cnt, rt.reshape(n_tok_tiles, 8, ROW_TILE))
    h1 = h1.reshape(T, D_MODEL)
    xs = _scatter_rows(cinfo, pos, h1, row(g_ffn), rg.reshape(n_tok_tiles, 8, ROW_TILE),
                       n_sorted_tiles * MOE_TILE)
    ys = _moe(sched, xs, w1.astype(BF16), w3.astype(BF16), w2.astype(BF16))
    out = _ple(pos, h1, ys, p_i.reshape(T, PLE_DIM), row(g_ple),
               w_ple_gate.astype(BF16), w_ple_proj.astype(BF16), row(g_ple_post))
    return out.reshape(B, S, D_MODEL)


def kernel(x, p, g_mix, w_in, q_norm, k_norm, conv_w, g_attn_out, g_conv_out, w_out, g_ffn, w_router_group, b_router_group, w_router_expert, b_router_expert, w1, w3, w2, g_ple, w_ple_gate, w_ple_proj, g_ple_post):
    h = x
    for i in range(p.shape[0]):
        h = _layer(h, p[i], g_mix[i], w_in[i], q_norm[i], k_norm[i], conv_w[i], g_attn_out[i], g_conv_out[i],
                   w_out[i], g_ffn[i], w_router_group[i], b_router_group[i], w_router_expert[i],
                   b_router_expert[i], w1[i], w3[i], w2[i], g_ple[i], w_ple_gate[i], w_ple_proj[i],
                   g_ple_post[i])
    return h
```

```python
import functools

import numpy as np
import jax
import jax.numpy as jnp
from jax import lax
from jax.experimental import pallas as pl
from jax.experimental.pallas import tpu as pltpu

F32 = jnp.float32
BF16 = jnp.bfloat16

D_MODEL = 1024
PLE_DIM = 256
HEAD_DIM = 64
N_HEADS = 8
ATTN_WIDTH = N_HEADS * HEAD_DIM
CONV_WIDTH = D_MODEL - ATTN_WIDTH
CONV_K = 3
DILATIONS = (1, 4, 16)
N_BACK = 128
ATTN_BLOCK = 128
ROPE_THETA = 500000.0
ROPE_DIM = HEAD_DIM // 4
ROPE_HALF = ROPE_DIM // 2
N_GROUPS = 4
EXPERTS_PER_GROUP = 4
N_EXPERTS = N_GROUPS * EXPERTS_PER_GROUP
D_FF = 512
EPS = 1e-6

LANES = 128
MXU_DIM = 256
NEG_BIG = -1e30
LOG2_E = 1.4426950408889634

ROW_TILE = 512
IN_PROJ_TILE = 1024
ROW_SUBSTEPS = 1
ATTN_TILE = 2048
ATTN_UNROLL = 16
MOE_TILE = 256
ROW_SUBTILES = D_MODEL // LANES
GATE_LANES = LANES
SLAB_ROWS = 16
GATE_LO_LANE = 0
GATE_HI_LANE = 1
EXPERT_PAIRS = tuple((a, b) for a in range(EXPERTS_PER_GROUP) for b in range(a + 1, EXPERTS_PER_GROUP))
N_CLASSES = N_GROUPS * len(EXPERT_PAIRS)
CLASS_ROWS = 32
SCHED_LANES = 256
ROUTER_GROUP_COL = 0
ROUTER_EXPERT_COL = 8
VMEM_LIMIT = 48 * 1024 * 1024


def _block_diag_ones(n, seg):
    idx = np.arange(n) // seg
    return jnp.asarray((idx[:, None] == idx[None, :]).astype(np.float32), dtype=BF16)


def _segment_mean_sq(t, bd):
    t2 = (t * t).astype(BF16)
    parts = [jnp.dot(t2[:, i:i + MXU_DIM], bd, preferred_element_type=F32)
             for i in range(0, t.shape[1], MXU_DIM)]
    return jnp.concatenate(parts, axis=1) * (1.0 / HEAD_DIM)


def _in_proj_body(x_ref, gmix_ref, win_ref, gq_ref, gk_ref, cos_ref, sa_ref, sb_ref, bd_ref, cw_ref, gconv_ref,
                  q_ref, k_ref, v_ref, conv_ref, ubuf):
    ts = x_ref.shape[0]
    sub = ts // ROW_SUBSTEPS
    bd = bd_ref[...]
    reps = ATTN_WIDTH // LANES

    @pl.when(pl.program_id(1) == 0)
    def _():
        ubuf[0:8, :] = jnp.zeros((8, CONV_WIDTH), F32)

    for r0 in range(0, ts, sub):
        rows = pl.ds(r0, sub)
        x = x_ref[rows, :]
        ms = jnp.mean(x * x, axis=-1, keepdims=True)
        xn = (x * lax.rsqrt(ms + EPS) * gmix_ref[...]).astype(BF16)

        def proj(c):
            return jnp.dot(xn, win_ref[:, c * ATTN_WIDTH:(c + 1) * ATTN_WIDTH], preferred_element_type=F32)

        cos = jnp.concatenate([cos_ref[rows, :]] * reps, axis=1)
        sa = jnp.concatenate([sa_ref[rows, :]] * reps, axis=1)
        sb = jnp.concatenate([sb_ref[rows, :]] * reps, axis=1)

        def norm_rope(t, g):
            tn = t * lax.rsqrt(_segment_mean_sq(t, bd) + EPS) * g
            up = pltpu.roll(tn, ATTN_WIDTH - ROPE_HALF, 1)
            dn = pltpu.roll(tn, ROPE_HALF, 1)
            return tn * cos + up * sa + dn * sb

        q_ref[rows, :] = norm_rope(proj(0), gq_ref[...]).astype(BF16)
        k_ref[rows, :] = norm_rope(proj(1), gk_ref[...]).astype(BF16)
        v_ref[rows, :] = proj(2).astype(BF16)

        cb = proj(3)
        u = proj(4) * proj(5)
        ubuf[8 + r0:8 + r0 + sub, :] = u
        u1 = ubuf[7 + r0:7 + r0 + sub, :]
        u2 = ubuf[6 + r0:6 + r0 + sub, :]
        y = cw_ref[0:1, :] * u2 + cw_ref[1:2, :] * u1 + cw_ref[2:3, :] * u
        conv = cb * y
        convn = conv * lax.rsqrt(_segment_mean_sq(conv, bd) + EPS) * gconv_ref[...]
        conv_ref[rows, :] = convn.astype(BF16)

    ubuf[0:8, :] = ubuf[ts:ts + 8, :]


def _in_proj(x, g_mix, w_in, gq, gk, cos_t, sa_t, sb_t, bd, cw, g_conv):
    B, S, _ = x.shape
    ts = IN_PROJ_TILE
    row = lambda b, j: (b, j, 0)
    const2 = lambda b, j: (0, 0)
    tab = lambda b, j: (j, 0)
    out_sds = jax.ShapeDtypeStruct((B, S, ATTN_WIDTH), BF16)
    return pl.pallas_call(
        _in_proj_body,
        grid=(B, S // ts),
        in_specs=[
            pl.BlockSpec((None, ts, D_MODEL), row),
            pl.BlockSpec((1, D_MODEL), const2),
            pl.BlockSpec((D_MODEL, 6 * ATTN_WIDTH), const2),
            pl.BlockSpec((1, ATTN_WIDTH), const2),
            pl.BlockSpec((1, ATTN_WIDTH), const2),
            pl.BlockSpec((ts, LANES), tab),
            pl.BlockSpec((ts, LANES), tab),
            pl.BlockSpec((ts, LANES), tab),
            pl.BlockSpec((MXU_DIM, MXU_DIM), const2),
            pl.BlockSpec((8, CONV_WIDTH), const2),
            pl.BlockSpec((1, CONV_WIDTH), const2),
        ],
        out_specs=[pl.BlockSpec((None, ts, ATTN_WIDTH), row)] * 4,
        out_shape=[out_sds] * 4,
        scratch_shapes=[pltpu.VMEM((ts + 8, CONV_WIDTH), F32)],
        compiler_params=pltpu.CompilerParams(
            dimension_semantics=("arbitrary", "arbitrary"), vmem_limit_bytes=VMEM_LIMIT),
        name="in_proj",
    )(x, g_mix, w_in, gq, gk, cos_t, sa_t, sb_t, bd, cw, g_conv)


def _attn_body(q_ref, kp_ref, kc_ref, vp_ref, vc_ref, gat_ref, bd_ref, bias_ref, o_ref,
               qf, kf, vf, acc_a, acc_b, m_a, m_b):
    tq = q_ref.shape[0]
    blk = ATTN_BLOCK
    first_tile = pl.program_id(2) == 0

    qf[...] = q_ref[...].astype(F32)
    kf[0:tq, :] = kp_ref[...].astype(F32)
    kf[tq:2 * tq, :] = kc_ref[...].astype(F32)
    vf[0:tq, :] = vp_ref[...].astype(F32)
    vf[tq:2 * tq, :] = vc_ref[...].astype(F32)

    lane = lax.broadcasted_iota(jnp.int32, (1, LANES), 1)
    sel_a = (lane < HEAD_DIM).astype(F32).astype(BF16)
    sel_b = (lane >= HEAD_DIM).astype(F32).astype(BF16)

    order = DILATIONS[::-1]
    for d in order:
        nblk = tq // (blk * d)
        shift = nblk.bit_length() - 1
        first_visit = d == order[0]

        def block(idx, carry, d=d, nblk=nblk, shift=shift, first_visit=first_visit):
            r = idx >> shift
            n = idx & (nblk - 1)
            rows_q = pl.ds(r + d * blk * n, blk, stride=d)
            rows_k = pl.ds(tq + d * blk * (n - 1) + r, 2 * blk, stride=d)
            qw = qf[rows_q, :].astype(BF16)
            kw = kf[rows_k, :].astype(BF16)
            vw = vf[rows_k, :].astype(BF16)
            bias = bias_ref[jnp.where(jnp.logical_and(first_tile, n == 0), 1, 0)]

            def one_head(sel_q, sel_one, acc, mst):
                s = lax.dot_general(qw * sel_q, kw, (((1,), (1,)), ((), ())), preferred_element_type=F32) + bias
                mb = jnp.max(s, axis=-1, keepdims=True)
                if first_visit:
                    m_new = jnp.broadcast_to(mb, (blk, LANES))
                else:
                    m_old = mst[rows_q, :]
                    m_new = jnp.maximum(m_old, mb)
                e = jnp.exp2(s - jnp.concatenate([m_new, m_new], axis=1))
                pv = jnp.dot(e.astype(BF16), vw * sel_q + sel_one, preferred_element_type=F32)
                if first_visit:
                    acc[rows_q, :] = pv
                else:
                    acc[rows_q, :] = acc[rows_q, :] * jnp.exp2(m_old - m_new) + pv
                mst[rows_q, :] = m_new

            one_head(sel_a, sel_b, acc_a, m_a)
            one_head(sel_b, sel_a, acc_b, m_b)
            return carry

        lax.fori_loop(0, tq // blk, block, 0, unroll=ATTN_UNROLL)

    head_a = lax.broadcasted_iota(jnp.int32, (tq, LANES), 1) < HEAD_DIM
    aa = acc_a[...]
    ab = acc_b[...]
    num = jnp.where(head_a, aa, ab)
    den = pltpu.roll(jnp.where(head_a, ab, aa), HEAD_DIM, 1)
    o = num / den
    o2 = (o * o).astype(BF16)
    msq = jnp.dot(o2, bd_ref[...], preferred_element_type=F32) * (1.0 / HEAD_DIM)
    o_ref[...] = (o * lax.rsqrt(msq + EPS) * gat_ref[...]).astype(BF16)


def _attention(q, k, v, g_attn, bd, bias):
    B, S, _ = q.shape
    tq = ATTN_TILE
    n_pairs = ATTN_WIDTH // LANES
    cur = lambda b, hp, j: (b, j, hp)
    prev = lambda b, hp, j: (b, jnp.maximum(j - 1, 0), hp)
    blk = (None, tq, LANES)
    return pl.pallas_call(
        _attn_body,
        grid=(B, n_pairs, S // tq),
        in_specs=[
            pl.BlockSpec(blk, cur),
            pl.BlockSpec(blk, prev),
            pl.BlockSpec(blk, cur),
            pl.BlockSpec(blk, prev),
            pl.BlockSpec(blk, cur),
            pl.BlockSpec((1, LANES), lambda b, hp, j: (0, hp)),
            pl.BlockSpec((LANES, LANES), lambda b, hp, j: (0, 0)),
            pl.BlockSpec((2, ATTN_BLOCK, 2 * ATTN_BLOCK), lambda b, hp, j: (0, 0, 0)),
        ],
        out_specs=pl.BlockSpec(blk, cur),
        out_shape=jax.ShapeDtypeStruct((B, S, ATTN_WIDTH), BF16),
        scratch_shapes=[
            pltpu.VMEM((tq, LANES), F32),
            pltpu.VMEM((2 * tq, LANES), F32), pltpu.VMEM((2 * tq, LANES), F32),
            pltpu.VMEM((tq, LANES), F32), pltpu.VMEM((tq, LANES), F32),
            pltpu.VMEM((tq, LANES), F32), pltpu.VMEM((tq, LANES), F32),
        ],
        compiler_params=pltpu.CompilerParams(
            dimension_semantics=("arbitrary", "arbitrary", "arbitrary"), vmem_limit_bytes=VMEM_LIMIT),
        name="attention",
    )(q, k, k, v, v, g_attn, bd, bias)


def _band_bias():
    qi = np.arange(ATTN_BLOCK)[:, None]
    ki = np.arange(2 * ATTN_BLOCK)[None, :]
    dist = qi + ATTN_BLOCK - ki
    band = (dist >= 0) & (dist <= N_BACK)
    no_prev = band & (ki >= ATTN_BLOCK)
    tab = np.stack([band, no_prev]).astype(np.float32)
    return jnp.asarray((1.0 - tab) * NEG_BIG, dtype=F32)


def _first_argmax(vals):
    best = vals[0]
    idx = jnp.zeros(best.shape, jnp.int32)
    for i in range(1, len(vals)):
        upd = vals[i] > best
        idx = jnp.where(upd, i, idx)
        best = jnp.where(upd, vals[i], best)
    return best, idx


def _out_route_body(x_ref, attn_ref, conv_ref, wo_ref, gffn_ref, wr_ref, br_ref, upper_ref,
                    h1_ref, rt_ref, rg_ref, cnt_ref, cnt):
    ts = x_ref.shape[0]
    h1 = (x_ref[...]
          + jnp.dot(attn_ref[...], wo_ref[0:ATTN_WIDTH, :], preferred_element_type=F32)
          + jnp.dot(conv_ref[...], wo_ref[ATTN_WIDTH:, :], preferred_element_type=F32))
    h1_ref[...] = h1
    ms = jnp.mean(h1 * h1, axis=-1, keepdims=True)
    xn = h1 * lax.rsqrt(ms + EPS) * gffn_ref[...]

    logits = jnp.dot(xn.astype(BF16), wr_ref[...], preferred_element_type=F32) + br_ref[...]
    lt = logits.T

    lg = [lt[ROUTER_GROUP_COL + i:ROUTER_GROUP_COL + i + 1, :] for i in range(N_GROUPS)]
    gbest, gi = _first_argmax(lg)
    sumexp = lg[0] * 0.0
    for i in range(N_GROUPS):
        sumexp = sumexp + jnp.exp(lg[i] - gbest)
    pg_top = 1.0 / sumexp

    sel = []
    for jx in range(EXPERTS_PER_GROUP):
        cand = [lt[ROUTER_EXPERT_COL + EXPERTS_PER_GROUP * g + jx:ROUTER_EXPERT_COL + EXPERTS_PER_GROUP * g + jx + 1, :]
                for g in range(N_GROUPS)]
        vj = cand[N_GROUPS - 1]
        for g in range(N_GROUPS - 2, -1, -1):
            vj = jnp.where(gi == g, cand[g], vj)
        sel.append(vj)
    b1, i1 = _first_argmax(sel)
    masked = [jnp.where(i1 == jx, -jnp.inf, sel[jx]) for jx in range(EXPERTS_PER_GROUP)]
    b2, i2 = _first_argmax(masked)
    e2 = jnp.exp(b2 - b1)
    t1 = 1.0 / (1.0 + e2)
    ga = pg_top * t1
    gb = pg_top * (e2 * t1)
    lo = jnp.minimum(i1, i2)
    hi = jnp.maximum(i1, i2)
    g_lo = jnp.where(i1 < i2, ga, gb)
    g_hi = jnp.where(i1 < i2, gb, ga)
    pair = hi - lo - 1
    for a in range(1, EXPERTS_PER_GROUP - 1):
        pair = pair + jnp.where(lo >= a, EXPERT_PAIRS.index((a, a + 1)) - EXPERT_PAIRS.index((a - 1, a)), 0)
    cls = gi * len(EXPERT_PAIRS) + pair

    r8 = lax.broadcasted_iota(jnp.int32, (8, ts), 0)
    rg_ref[...] = jnp.where(r8 == 0, g_lo, jnp.where(r8 == 1, g_hi, 0.0))

    @pl.when(jnp.logical_and(pl.program_id(0) == 0, pl.program_id(1) == 0))
    def _():
        cnt[...] = jnp.zeros(cnt.shape, F32)

    crow = lax.broadcasted_iota(jnp.int32, (CLASS_ROWS, ts), 0)
    oh = (crow == cls).astype(F32)
    before = jnp.dot(oh.astype(BF16), upper_ref[...], preferred_element_type=F32)
    rank = jnp.sum(oh * (before + cnt[:, 0:1]), axis=0, keepdims=True)
    cnt[...] = cnt[...] + jnp.sum(oh, axis=1, keepdims=True)
    cnt_ref[...] = cnt[...]
    rt_ref[...] = jnp.where(r8 == 0, cls, jnp.where(r8 == 1, rank.astype(jnp.int32), 0))


def _out_route(x, attn, conv, w_out, g_ffn, wr, b_r):
    B, S, _ = x.shape
    ts = ROW_TILE
    row = lambda b, j: (b, j, 0)
    const2 = lambda b, j: (0, 0)
    tok = np.arange(ts)
    upper = jnp.asarray((tok[:, None] < tok[None, :]).astype(np.float32), dtype=BF16)
    return pl.pallas_call(
        _out_route_body,
        grid=(B, S // ts),
        in_specs=[
            pl.BlockSpec((None, ts, D_MODEL), row),
            pl.BlockSpec((None, ts, ATTN_WIDTH), row),
            pl.BlockSpec((None, ts, CONV_WIDTH), row),
            pl.BlockSpec((D_MODEL, D_MODEL), const2),
            pl.BlockSpec((1, D_MODEL), const2),
            pl.BlockSpec((D_MODEL, GATE_LANES), const2),
            pl.BlockSpec((1, GATE_LANES), const2),
            pl.BlockSpec((ts, ts), const2),
        ],
        out_specs=[
            pl.BlockSpec((None, ts, D_MODEL), row),
            pl.BlockSpec((None, None, 8, ts), lambda b, j: (b, j, 0, 0)),
            pl.BlockSpec((None, None, 8, ts), lambda b, j: (b, j, 0, 0)),
            pl.BlockSpec((CLASS_ROWS, LANES), const2),
        ],
        out_shape=[
            jax.ShapeDtypeStruct((B, S, D_MODEL), F32),
            jax.ShapeDtypeStruct((B, S // ts, 8, ts), jnp.int32),
            jax.ShapeDtypeStruct((B, S // ts, 8, ts), F32),
            jax.ShapeDtypeStruct((CLASS_ROWS, LANES), F32),
        ],
        scratch_shapes=[pltpu.VMEM((CLASS_ROWS, LANES), F32)],
        compiler_params=pltpu.CompilerParams(
            dimension_semantics=("arbitrary", "arbitrary"), vmem_limit_bytes=VMEM_LIMIT),
        name="out_route",
    )(x, attn, conv, w_out, g_ffn, wr, b_r, upper)


def _route_plan_body(cnt_ref, rt_ref, elo_ref, ehi_ref, pos_ref, sched_ref, cinfo_ref):
    n_tiles_tok, _, ts = pos_ref.shape
    shift = MOE_TILE.bit_length() - 1
    ntile = (cnt_ref[...].astype(jnp.int32) + (MOE_TILE - 1)) >> shift
    crow = lax.broadcasted_iota(jnp.int32, (CLASS_ROWS, LANES), 0)
    incl = ntile
    step = 1
    while step < CLASS_ROWS:
        incl = incl + jnp.where(crow >= step, pltpu.roll(incl, step, 0), 0)
        step *= 2
    tstart = incl - ntile
    total = incl[CLASS_ROWS - 1:CLASS_ROWS, 0:1]

    row_base = jnp.broadcast_to(tstart[:, 0:1] * MOE_TILE, (CLASS_ROWS, ts))
    crow_t = lax.broadcasted_iota(jnp.int32, (CLASS_ROWS, ts), 0)

    def token_tile(i, carry):
        cls = rt_ref[i, 0:1, :]
        rank = rt_ref[i, 1:2, :]
        base = jnp.sum(jnp.where(crow_t == cls, row_base, 0), axis=0, keepdims=True)
        pos_ref[i] = base + rank
        return carry

    lax.fori_loop(0, n_tiles_tok, token_tile, 0)

    lane_j = lax.broadcasted_iota(jnp.int32, (CLASS_ROWS, SCHED_LANES), 1)
    start_b = jnp.broadcast_to(tstart[:, 0:1], (CLASS_ROWS, SCHED_LANES))
    ntile_b = jnp.broadcast_to(ntile[:, 0:1], (CLASS_ROWS, SCHED_LANES))
    member = jnp.logical_and(lane_j >= start_b, lane_j < start_b + ntile_b)
    pick = lambda tab: jnp.sum(jnp.where(member, jnp.broadcast_to(tab[:, 0:1], member.shape), 0),
                               axis=0, keepdims=True)
    valid = jnp.sum(member.astype(jnp.int32), axis=0, keepdims=True)
    elo = pick(elo_ref[...])
    ehi = pick(ehi_ref[...])
    j1 = lane_j[0:1, :]
    last = total - 1
    at_last = j1 == last
    elo_last = jnp.sum(jnp.where(at_last, elo, 0), axis=1, keepdims=True)
    ehi_last = jnp.sum(jnp.where(at_last, ehi, 0), axis=1, keepdims=True)
    in_use = valid > 0
    elo = jnp.where(in_use, elo, elo_last)
    ehi = jnp.where(in_use, ehi, ehi_last)
    blk = jnp.minimum(j1, last)
    r8 = lax.broadcasted_iota(jnp.int32, (8, SCHED_LANES), 0)
    sched_ref[...] = jnp.where(r8 == 0, elo, jnp.where(r8 == 1, ehi, jnp.where(r8 == 2, valid,
                               jnp.where(r8 == 3, blk, 0))))
    lane_c = lax.broadcasted_iota(jnp.int32, (CLASS_ROWS, LANES), 1)
    cinfo_ref[...] = jnp.where(lane_c == 0, tstart + ntile - 1,
                               jnp.where(lane_c == 1, (ntile > 0).astype(jnp.int32), total))


def _route_plan(cnt, rt):
    n_tiles_tok, _, ts = rt.shape
    elo_tab = np.zeros((CLASS_ROWS, LANES), np.int32)
    ehi_tab = np.zeros((CLASS_ROWS, LANES), np.int32)
    for g in range(N_GROUPS):
        for p, (a, b) in enumerate(EXPERT_PAIRS):
            elo_tab[g * len(EXPERT_PAIRS) + p, :] = g * EXPERTS_PER_GROUP + a
            ehi_tab[g * len(EXPERT_PAIRS) + p, :] = g * EXPERTS_PER_GROUP + b
    full = lambda shape: pl.BlockSpec(shape, lambda i: (0,) * len(shape))
    return pl.pallas_call(
        _route_plan_body,
        grid=(1,),
        in_specs=[full((CLASS_ROWS, LANES)), full(rt.shape), full((CLASS_ROWS, LANES)), full((CLASS_ROWS, LANES))],
        out_specs=[full((n_tiles_tok, 1, ts)), full((8, SCHED_LANES)), full((CLASS_ROWS, LANES))],
        out_shape=[
            jax.ShapeDtypeStruct((n_tiles_tok, 1, ts), jnp.int32),
            jax.ShapeDtypeStruct((8, SCHED_LANES), jnp.int32),
            jax.ShapeDtypeStruct((CLASS_ROWS, LANES), jnp.int32),
        ],
        compiler_params=pltpu.CompilerParams(dimension_semantics=("arbitrary",)),
        name="route_plan",
    )(cnt, rt, jnp.asarray(elo_tab), jnp.asarray(ehi_tab))


DMA_UNROLL = 8


def _start_rows(n_rows, make_copy):
    for t in range(n_rows):
        make_copy(t).start(priority=t % 2)


def _wait_rows(n_rows, make_copy):
    def trip(t8, carry):
        for _ in range(DMA_UNROLL):
            make_copy(0).wait()
        return carry
    lax.fori_loop(0, n_rows // DMA_UNROLL, trip, 0)


def _scatter_rows_body(last_ref, has_ref, total_ref, pos_ref, h1_ref, gffn_ref, rg_ref, xs_hbm,
                       rows, zbuf, sem, zsem):
    i = pl.program_id(0)
    n = pl.num_programs(0)
    ts = h1_ref.shape[0]
    slot = i % 2
    tile_rows = MOE_TILE * SLAB_ROWS

    n_dst_tiles = xs_hbm.shape[0] // tile_rows

    def pad_copies():
        for c in range(N_CLASSES):
            yield has_ref[c] != 0, last_ref[c]
            yield total_ref[0] + c < n_dst_tiles, total_ref[0] + c

    def tile_copy(tile):
        return pltpu.make_async_copy(zbuf, xs_hbm.at[pl.ds(tile * tile_rows, tile_rows)], zsem)

    @pl.when(i == 0)
    def _():
        rows[...] = jnp.zeros(rows.shape, rows.dtype)
        zbuf[...] = jnp.zeros(zbuf.shape, zbuf.dtype)
        for cond, tile in pad_copies():
            @pl.when(cond)
            def _():
                tile_copy(tile).start()
        for cond, tile in pad_copies():
            @pl.when(cond)
            def _():
                tile_copy(tile).wait()

    h1 = h1_ref[...]
    ms = jnp.mean(h1 * h1, axis=-1, keepdims=True)
    xn = h1 * lax.rsqrt(ms + EPS) * gffn_ref[...]
    for k in range(ROW_SUBTILES):
        rows[slot, pl.ds(k, ts, stride=SLAB_ROWS), :] = xn[:, k * LANES:(k + 1) * LANES]
    erow = lax.broadcasted_iota(jnp.int32, (GATE_LANES, ts), 0)
    gt = (jnp.where(erow == GATE_LO_LANE, rg_ref[0:1, :], 0.0)
          + jnp.where(erow == GATE_HI_LANE, rg_ref[1:2, :], 0.0))
    rows[slot, pl.ds(ROW_SUBTILES, ts, stride=SLAB_ROWS), :] = gt.T

    def row_copy(sl):
        return lambda t: pltpu.make_async_copy(
            rows.at[sl, pl.ds(t * SLAB_ROWS, SLAB_ROWS)],
            xs_hbm.at[pl.ds(pl.multiple_of(pos_ref[0, t] * SLAB_ROWS, SLAB_ROWS), SLAB_ROWS)], sem.at[sl])

    _start_rows(ts, row_copy(slot))

    @pl.when(i > 0)
    def _():
        _wait_rows(ts, row_copy(1 - slot))

    @pl.when(i == n - 1)
    def _():
        _wait_rows(ts, row_copy(slot))


def _scatter_rows(cinfo, pos, h1, g_ffn, rg, n_dst_rows):
    n_tiles_tok, _, ts = pos.shape
    return pl.pallas_call(
        _scatter_rows_body,
        grid_spec=pltpu.PrefetchScalarGridSpec(
            num_scalar_prefetch=3,
            grid=(n_tiles_tok,),
            in_specs=[
                pl.BlockSpec((None, 1, ts), lambda i, *_: (i, 0, 0), memory_space=pltpu.SMEM),
                pl.BlockSpec((ts, D_MODEL), lambda i, *_: (i, 0)),
                pl.BlockSpec((1, D_MODEL), lambda i, *_: (0, 0)),
                pl.BlockSpec((None, 8, ts), lambda i, *_: (i, 0, 0)),
            ],
            out_specs=pl.BlockSpec(memory_space=pl.ANY),
            scratch_shapes=[
                pltpu.VMEM((2, ts * SLAB_ROWS, LANES), F32),
                pltpu.VMEM((MOE_TILE * SLAB_ROWS, LANES), F32),
                pltpu.SemaphoreType.DMA((2,)),
                pltpu.SemaphoreType.DMA(()),
            ],
        ),
        out_shape=jax.ShapeDtypeStruct((n_dst_rows * SLAB_ROWS, LANES), F32),
        compiler_params=pltpu.CompilerParams(dimension_semantics=("arbitrary",), vmem_limit_bytes=VMEM_LIMIT),
        name="scatter_rows",
    )(cinfo[:, 0], cinfo[:, 1], cinfo[0:1, 2], pos, h1, g_ffn, rg)


def _moe_body(elo_ref, ehi_ref, valid_ref, blk_ref, xs_ref, w1l_ref, w3l_ref, w2l_ref, w1h_ref, w3h_ref, w2h_ref,
              ys_ref, w1l, w3l, w2l, w1h, w3h, w2h):
    j = pl.program_id(0)
    prev = jnp.maximum(j - 1, 0)
    for ids, pairs in ((elo_ref, ((w1l_ref, w1l), (w3l_ref, w3l), (w2l_ref, w2l))),
                       (ehi_ref, ((w1h_ref, w1h), (w3h_ref, w3h), (w2h_ref, w2h)))):
        @pl.when(jnp.logical_or(j == 0, ids[j] != ids[prev]))
        def _():
            for src, dst in pairs:
                dst[...] = src[...].astype(BF16)

    @pl.when(valid_ref[pl.program_id(0)] != 0)
    def _():
        slab_row = lambda k: xs_ref[pl.ds(k, MOE_TILE, stride=SLAB_ROWS), :]
        x = jnp.concatenate([slab_row(k).astype(BF16) for k in range(ROW_SUBTILES)], axis=1)
        gates = slab_row(ROW_SUBTILES)

        def expert(w1_ref, w3_ref, w2_ref):
            a = jnp.dot(x, w1_ref[...], preferred_element_type=F32)
            b = jnp.dot(x, w3_ref[...], preferred_element_type=F32)
            hdn = (a * (1.0 / (1.0 + jnp.exp(-a))) * b).astype(BF16)
            return jnp.dot(hdn, w2_ref[...], preferred_element_type=F32)

        g_lo = gates[:, GATE_LO_LANE:GATE_LO_LANE + 1]
        g_hi = gates[:, GATE_HI_LANE:GATE_HI_LANE + 1]
        y = g_lo * expert(w1l, w3l, w2l) + g_hi * expert(w1h, w3h, w2h)
        for k in range(ROW_SUBTILES):
            ys_ref[pl.ds(k, MOE_TILE, stride=ROW_SUBTILES), :] = y[:, k * LANES:(k + 1) * LANES]

    @pl.when(valid_ref[pl.program_id(0)] == 0)
    def _():
        ys_ref[...] = jnp.zeros(ys_ref.shape, F32)


def _moe(sched, xs, w1, w3, w2):
    n_tiles = xs.shape[0] // (MOE_TILE * SLAB_ROWS)
    rows = lambda j, elo, ehi, valid, blk: (blk[j], 0)
    w_lo = lambda j, elo, ehi, valid, blk: (elo[j], 0, 0)
    w_hi = lambda j, elo, ehi, valid, blk: (ehi[j], 0, 0)
    up = (None, D_MODEL, D_FF)
    down = (None, D_FF, D_MODEL)
    return pl.pallas_call(
        _moe_body,
        grid_spec=pltpu.PrefetchScalarGridSpec(
            num_scalar_prefetch=4,
            grid=(n_tiles,),
            in_specs=[
                pl.BlockSpec((MOE_TILE * SLAB_ROWS, LANES), rows),
                pl.BlockSpec(up, w_lo), pl.BlockSpec(up, w_lo), pl.BlockSpec(down, w_lo),
                pl.BlockSpec(up, w_hi), pl.BlockSpec(up, w_hi), pl.BlockSpec(down, w_hi),
            ],
            out_specs=pl.BlockSpec((MOE_TILE * ROW_SUBTILES, LANES), lambda j, *_: (j, 0)),
            scratch_shapes=[pltpu.VMEM(up[1:], BF16), pltpu.VMEM(up[1:], BF16), pltpu.VMEM(down[1:], BF16)] * 2,
        ),
        out_shape=jax.ShapeDtypeStruct((n_tiles * MOE_TILE * ROW_SUBTILES, LANES), F32),
        compiler_params=pltpu.CompilerParams(dimension_semantics=("arbitrary",), vmem_limit_bytes=VMEM_LIMIT),
        name="moe",
    )(sched[0, :n_tiles], sched[1, :n_tiles], sched[2, :n_tiles], sched[3, :n_tiles], xs, w1, w3, w2, w1, w3, w2)


def _ple_body(pos_ref, posn_ref, h1_ref, ys_hbm, p_ref, gple_ref, wg_ref, wp_ref, gpost_ref, o_ref, ybuf, sem):
    i = pl.program_id(0)
    n = pl.num_programs(0)
    ts = h1_ref.shape[0]
    slot = i % 2

    def row_copy(pref, sl):
        return lambda t: pltpu.make_async_copy(
            ys_hbm.at[pl.ds(pl.multiple_of(pref[0, t] * ROW_SUBTILES, ROW_SUBTILES), ROW_SUBTILES)],
            ybuf.at[sl, pl.ds(t * ROW_SUBTILES, ROW_SUBTILES)], sem.at[sl])

    @pl.when(i == 0)
    def _():
        _start_rows(ts, row_copy(pos_ref, 0))

    _wait_rows(ts, row_copy(pos_ref, slot))
    _start_rows(ts, row_copy(posn_ref, 1 - slot))

    y = jnp.concatenate([ybuf[slot, pl.ds(k, ts, stride=ROW_SUBTILES), :] for k in range(ROW_SUBTILES)], axis=1)
    h2 = h1_ref[...] + y
    ms = jnp.mean(h2 * h2, axis=-1, keepdims=True)
    hn = (h2 * lax.rsqrt(ms + EPS) * gple_ref[...]).astype(BF16)
    z = jnp.dot(hn, wg_ref[...], preferred_element_type=F32)
    gate = 1.0 / (1.0 + jnp.exp(-z))
    pp = jnp.dot(p_ref[...].astype(BF16), wp_ref[...], preferred_element_type=F32)
    pms = jnp.mean(pp * pp, axis=-1, keepdims=True)
    pn = pp * lax.rsqrt(pms + EPS) * gpost_ref[...]
    o_ref[...] = h2 + gate * pn

    @pl.when(i == n - 1)
    def _():
        _wait_rows(ts, row_copy(posn_ref, 1 - slot))


def _ple(pos, h1, ys, p, g_ple, wg, wp, g_post):
    T = h1.shape[0]
    n_tiles_tok, _, ts = pos.shape
    row = lambda i: (i, 0)
    const = lambda i: (0, 0)
    return pl.pallas_call(
        _ple_body,
        grid=(n_tiles_tok,),
        in_specs=[
            pl.BlockSpec((None, 1, ts), lambda i: (i, 0, 0), memory_space=pltpu.SMEM),
            pl.BlockSpec((None, 1, ts), lambda i: (jnp.minimum(i + 1, n_tiles_tok - 1), 0, 0),
                         memory_space=pltpu.SMEM),
            pl.BlockSpec((ts, D_MODEL), row),
            pl.BlockSpec(memory_space=pl.ANY),
            pl.BlockSpec((ts, PLE_DIM), row),
            pl.BlockSpec((1, D_MODEL), const),
            pl.BlockSpec((D_MODEL, D_MODEL), const),
            pl.BlockSpec((PLE_DIM, D_MODEL), const),
            pl.BlockSpec((1, D_MODEL), const),
        ],
        out_specs=pl.BlockSpec((ts, D_MODEL), row),
        out_shape=jax.ShapeDtypeStruct((T, D_MODEL), F32),
        scratch_shapes=[pltpu.VMEM((2, ts * ROW_SUBTILES, LANES), F32), pltpu.SemaphoreType.DMA((2,))],
        compiler_params=pltpu.CompilerParams(
            dimension_semantics=("arbitrary",), vmem_limit_bytes=VMEM_LIMIT),
        name="ple",
    )(pos, pos, h1, ys, p, g_ple, wg, wp, g_post)


def _rope_tables(S):
    pos = jnp.arange(S)
    inv = ROPE_THETA ** (-jnp.arange(0, ROPE_DIM, 2, dtype=F32) / ROPE_DIM)
    ang = inv[:, None] * pos.astype(F32)[None, :]
    cos, sin = jnp.cos(ang), jnp.sin(ang)
    lane = np.arange(LANES) % HEAD_DIM
    freq = np.arange(ROPE_HALF)[:, None]
    first = (lane[None, :] == freq).astype(np.float32)
    second = (lane[None, :] == freq + ROPE_HALF).astype(np.float32)
    rest = (lane >= ROPE_DIM).astype(np.float32)[None, :]
    spread = lambda t, m: lax.dot_general(t, jnp.asarray(m), (((0,), (0,)), ((), ())),
                                          precision=lax.Precision.HIGHEST)
    return spread(cos, first + second) + rest, spread(sin, -first), spread(sin, second)


def _layer(h, p_i, g_mix, w_in, q_norm, k_norm, conv_w, g_attn_out, g_conv_out, w_out, g_ffn,
           w_rg, b_rg, w_re, b_re, w1, w3, w2, g_ple, w_ple_gate, w_ple_proj, g_ple_post):
    B, S, _ = h.shape
    T = B * S
    row = lambda g: g.reshape(1, -1).astype(F32)

    cos_t, sa_t, sb_t = _rope_tables(S)
    bd256 = _block_diag_ones(MXU_DIM, HEAD_DIM)
    bd128 = _block_diag_ones(LANES, HEAD_DIM)
    gq = row(jnp.tile(q_norm, N_HEADS) * (HEAD_DIM ** -0.5 * LOG2_E))
    gk = row(jnp.tile(k_norm, N_HEADS))
    cw = jnp.zeros((8, CONV_WIDTH), F32).at[0:CONV_K].set(conv_w)

    q, k, v, convn = _in_proj(h, row(g_mix), w_in.astype(BF16), gq, gk, cos_t, sa_t, sb_t, bd256, cw,
                              row(g_conv_out))
    attn = _attention(q, k, v, row(g_attn_out), bd128, _band_bias())

    w_r = jnp.zeros((D_MODEL, GATE_LANES), F32)
    w_r = w_r.at[:, ROUTER_GROUP_COL:ROUTER_GROUP_COL + N_GROUPS].set(w_rg)
    w_r = w_r.at[:, ROUTER_EXPERT_COL:ROUTER_EXPERT_COL + N_EXPERTS].set(w_re)
    b_r = jnp.zeros((1, GATE_LANES), F32)
    b_r = b_r.at[0, ROUTER_GROUP_COL:ROUTER_GROUP_COL + N_GROUPS].set(b_rg)
    b_r = b_r.at[0, ROUTER_EXPERT_COL:ROUTER_EXPERT_COL + N_EXPERTS].set(b_re)
    wr = w_r.astype(BF16)

    h1, rt, rg, cnt = _out_route(h, attn, convn, w_out.astype(BF16), row(g_ffn), wr, b_r)
    n_tok_tiles = T // ROW_TILE
    n_sorted_tiles = T // MOE_TILE + N_CLASSES
    assert n_sorted_tiles <= SCHED_LANES
    pos, sched, cinfo = _route_plan(cnt, rt.reshape(n_tok_tiles, 8, ROW_TILE))
    h1 = h1.reshape(T, D_MODEL)
    xs = _scatter_rows(cinfo, pos, h1, row(g_ffn), rg.reshape(n_tok_tiles, 8, ROW_TILE),
                       n_sorted_tiles * MOE_TILE)
    ys = _moe(sched, xs, w1, w3, w2)
    out = _ple(pos, h1, ys, p_i.reshape(T, PLE_DIM), row(g_ple),
               w_ple_gate.astype(BF16), w_ple_proj.astype(BF16), row(g_ple_post))
    return out.reshape(B, S, D_MODEL)


def kernel(x, p, g_mix, w_in, q_norm, k_norm, conv_w, g_attn_out, g_conv_out, w_out, g_ffn, w_router_group, b_router_group, w_router_expert, b_router_expert, w1, w3, w2, g_ple, w_ple_gate, w_ple_proj, g_ple_post):
    h = x
    for i in range(p.shape[0]):
        h = _layer(h, p[i], g_mix[i], w_in[i], q_norm[i], k_norm[i], conv_w[i], g_attn_out[i], g_conv_out[i],
                   w_out[i], g_ffn[i], w_router_group[i], b_router_group[i], w_router_expert[i],
                   b_router_expert[i], w1[i], w3[i], w2[i], g_ple[i], w_ple_gate[i], w_ple_proj[i],
                   g_ple_post[i])
    return h
```

```python
import functools

import numpy as np
import jax
import jax.numpy as jnp
from jax import lax
from jax.experimental import pallas as pl
from jax.experimental.pallas import tpu as pltpu

F32 = jnp.float32
BF16 = jnp.bfloat16

D_MODEL = 1024
PLE_DIM = 256
HEAD_DIM = 64
N_HEADS = 8
ATTN_WIDTH = N_HEADS * HEAD_DIM
CONV_WIDTH = D_MODEL - ATTN_WIDTH
CONV_K = 3
DILATIONS = (1, 4, 16)
N_BACK = 128
ATTN_BLOCK = 128
ROPE_THETA = 500000.0
ROPE_DIM = HEAD_DIM // 4
ROPE_HALF = ROPE_DIM // 2
N_GROUPS = 4
EXPERTS_PER_GROUP = 4
N_EXPERTS = N_GROUPS * EXPERTS_PER_GROUP
D_FF = 512
EPS = 1e-6

LANES = 128
MXU_DIM = 256
NEG_BIG = -1e30
LOG2_E = 1.4426950408889634

ROW_TILE = 512
IN_PROJ_TILE = 1024
ROW_SUBSTEPS = 1
ATTN_TILE = 2048
ATTN_UNROLL = 16
MOE_TILE = 256
ROW_SUBTILES = D_MODEL // LANES
GATE_LANES = LANES
SLAB_ROWS = 16
GATE_LO_LANE = 0
GATE_HI_LANE = 1
EXPERT_PAIRS = tuple((a, b) for a in range(EXPERTS_PER_GROUP) for b in range(a + 1, EXPERTS_PER_GROUP))
N_CLASSES = N_GROUPS * len(EXPERT_PAIRS)
CLASS_ROWS = 32
SCHED_LANES = 256
ROUTER_GROUP_COL = 0
ROUTER_EXPERT_COL = 8
VMEM_LIMIT = 48 * 1024 * 1024


def _block_diag_ones(n, seg):
    idx = np.arange(n) // seg
    return jnp.asarray((idx[:, None] == idx[None, :]).astype(np.float32), dtype=BF16)


def _segment_mean_sq(t, bd):
    t2 = (t * t).astype(BF16)
    parts = [jnp.dot(t2[:, i:i + MXU_DIM], bd, preferred_element_type=F32)
             for i in range(0, t.shape[1], MXU_DIM)]
    return jnp.concatenate(parts, axis=1) * (1.0 / HEAD_DIM)


def _in_proj_body(x_ref, gmix_ref, win_ref, gq_ref, gk_ref, cos_ref, sa_ref, sb_ref, bd_ref, cw_ref, gconv_ref,
                  q_ref, k_ref, v_ref, conv_ref, ubuf):
    ts = x_ref.shape[0]
    sub = ts // ROW_SUBSTEPS
    bd = bd_ref[...]
    reps = ATTN_WIDTH // LANES

    @pl.when(pl.program_id(1) == 0)
    def _():
        ubuf[0:8, :] = jnp.zeros((8, CONV_WIDTH), F32)

    for r0 in range(0, ts, sub):
        rows = pl.ds(r0, sub)
        x = x_ref[rows, :]
        ms = jnp.mean(x * x, axis=-1, keepdims=True)
        xn = (x * lax.rsqrt(ms + EPS) * gmix_ref[...]).astype(BF16)

        def proj(c):
            return jnp.dot(xn, win_ref[:, c * ATTN_WIDTH:(c + 1) * ATTN_WIDTH], preferred_element_type=F32)

        cos = jnp.concatenate([cos_ref[rows, :]] * reps, axis=1)
        sa = jnp.concatenate([sa_ref[rows, :]] * reps, axis=1)
        sb = jnp.concatenate([sb_ref[rows, :]] * reps, axis=1)

        def norm_rope(t, g):
            tn = t * lax.rsqrt(_segment_mean_sq(t, bd) + EPS) * g
            up = pltpu.roll(tn, ATTN_WIDTH - ROPE_HALF, 1)
            dn = pltpu.roll(tn, ROPE_HALF, 1)
            return tn * cos + up * sa + dn * sb

        q_ref[rows, :] = norm_rope(proj(0), gq_ref[...]).astype(BF16)
        k_ref[rows, :] = norm_rope(proj(1), gk_ref[...]).astype(BF16)
        v_ref[rows, :] = proj(2).astype(BF16)

        cb = proj(3)
        u = proj(4) * proj(5)
        ubuf[8 + r0:8 + r0 + sub, :] = u
        u1 = ubuf[7 + r0:7 + r0 + sub, :]
        u2 = ubuf[6 + r0:6 + r0 + sub, :]
        y = cw_ref[0:1, :] * u2 + cw_ref[1:2, :] * u1 + cw_ref[2:3, :] * u
        conv = cb * y
        convn = conv * lax.rsqrt(_segment_mean_sq(conv, bd) + EPS) * gconv_ref[...]
        conv_ref[rows, :] = convn.astype(BF16)

    ubuf[0:8, :] = ubuf[ts:ts + 8, :]


def _in_proj(x, g_mix, w_in, gq, gk, cos_t, sa_t, sb_t, bd, cw, g_conv):
    B, S, _ = x.shape
    ts = IN_PROJ_TILE
    row = lambda b, j: (b, j, 0)
    const2 = lambda b, j: (0, 0)
    tab = lambda b, j: (j, 0)
    out_sds = jax.ShapeDtypeStruct((B, S, ATTN_WIDTH), BF16)
    return pl.pallas_call(
        _in_proj_body,
        grid=(B, S // ts),
        in_specs=[
            pl.BlockSpec((None, ts, D_MODEL), row),
            pl.BlockSpec((1, D_MODEL), const2),
            pl.BlockSpec((D_MODEL, 6 * ATTN_WIDTH), const2),
            pl.BlockSpec((1, ATTN_WIDTH), const2),
            pl.BlockSpec((1, ATTN_WIDTH), const2),
            pl.BlockSpec((ts, LANES), tab),
            pl.BlockSpec((ts, LANES), tab),
            pl.BlockSpec((ts, LANES), tab),
            pl.BlockSpec((MXU_DIM, MXU_DIM), const2),
            pl.BlockSpec((8, CONV_WIDTH), const2),
            pl.BlockSpec((1, CONV_WIDTH), const2),
        ],
        out_specs=[pl.BlockSpec((None, ts, ATTN_WIDTH), row)] * 4,
        out_shape=[out_sds] * 4,
        scratch_shapes=[pltpu.VMEM((ts + 8, CONV_WIDTH), F32)],
        compiler_params=pltpu.CompilerParams(
            dimension_semantics=("arbitrary", "arbitrary"), vmem_limit_bytes=VMEM_LIMIT),
        name="in_proj",
    )(x, g_mix, w_in, gq, gk, cos_t, sa_t, sb_t, bd, cw, g_conv)


def _attn_body(q_ref, kp_ref, kc_ref, vp_ref, vc_ref, gat_ref, bd_ref, bias_ref, o_ref,
               qf, kf, vf, acc_a, acc_b, m_a, m_b):
    tq = q_ref.shape[0]
    blk = ATTN_BLOCK
    first_tile = pl.program_id(2) == 0

    qf[...] = q_ref[...].astype(F32)
    kf[0:tq, :] = kp_ref[...].astype(F32)
    kf[tq:2 * tq, :] = kc_ref[...].astype(F32)
    vf[0:tq, :] = vp_ref[...].astype(F32)
    vf[tq:2 * tq, :] = vc_ref[...].astype(F32)

    lane = lax.broadcasted_iota(jnp.int32, (1, LANES), 1)
    sel_a = (lane < HEAD_DIM).astype(F32).astype(BF16)
    sel_b = (lane >= HEAD_DIM).astype(F32).astype(BF16)

    order = DILATIONS[::-1]
    for d in order:
        nblk = tq // (blk * d)
        shift = nblk.bit_length() - 1
        first_visit = d == order[0]

        def block(idx, carry, d=d, nblk=nblk, shift=shift, first_visit=first_visit):
            r = idx >> shift
            n = idx & (nblk - 1)
            rows_q = pl.ds(r + d * blk * n, blk, stride=d)
            rows_k = pl.ds(tq + d * blk * (n - 1) + r, 2 * blk, stride=d)
            qw = qf[rows_q, :].astype(BF16)
            kw = kf[rows_k, :].astype(BF16)
            vw = vf[rows_k, :].astype(BF16)
            bias = bias_ref[jnp.where(jnp.logical_and(first_tile, n == 0), 1, 0)]

            def one_head(sel_q, sel_one, acc, mst):
                s = lax.dot_general(qw * sel_q, kw, (((1,), (1,)), ((), ())), preferred_element_type=F32) + bias
                mb = jnp.max(s, axis=-1, keepdims=True)
                if first_visit:
                    m_new = jnp.broadcast_to(mb, (blk, LANES))
                else:
                    m_old = mst[rows_q, :]
                    m_new = jnp.maximum(m_old, mb)
                e = jnp.exp2(s - jnp.concatenate([m_new, m_new], axis=1))
                pv = jnp.dot(e.astype(BF16), vw * sel_q + sel_one, preferred_element_type=F32)
                if first_visit:
                    acc[rows_q, :] = pv
                else:
                    acc[rows_q, :] = acc[rows_q, :] * jnp.exp2(m_old - m_new) + pv
                mst[rows_q, :] = m_new

            one_head(sel_a, sel_b, acc_a, m_a)
            one_head(sel_b, sel_a, acc_b, m_b)
            return carry

        lax.fori_loop(0, tq // blk, block, 0, unroll=ATTN_UNROLL)

    head_a = lax.broadcasted_iota(jnp.int32, (tq, LANES), 1) < HEAD_DIM
    aa = acc_a[...]
    ab = acc_b[...]
    num = jnp.where(head_a, aa, ab)
    den = pltpu.roll(jnp.where(head_a, ab, aa), HEAD_DIM, 1)
    o = num / den
    o2 = (o * o).astype(BF16)
    msq = jnp.dot(o2, bd_ref[...], preferred_element_type=F32) * (1.0 / HEAD_DIM)
    o_ref[...] = (o * lax.rsqrt(msq + EPS) * gat_ref[...]).astype(BF16)


def _attention(q, k, v, g_attn, bd, bias):
    B, S, _ = q.shape
    tq = ATTN_TILE
    n_pairs = ATTN_WIDTH // LANES
    cur = lambda b, hp, j: (b, j, hp)
    prev = lambda b, hp, j: (b, jnp.maximum(j - 1, 0), hp)
    blk = (None, tq, LANES)
    return pl.pallas_call(
        _attn_body,
        grid=(B, n_pairs, S // tq),
        in_specs=[
            pl.BlockSpec(blk, cur),
            pl.BlockSpec(blk, prev),
            pl.BlockSpec(blk, cur),
            pl.BlockSpec(blk, prev),
            pl.BlockSpec(blk, cur),
            pl.BlockSpec((1, LANES), lambda b, hp, j: (0, hp)),
            pl.BlockSpec((LANES, LANES), lambda b, hp, j: (0, 0)),
            pl.BlockSpec((2, ATTN_BLOCK, 2 * ATTN_BLOCK), lambda b, hp, j: (0, 0, 0)),
        ],
        out_specs=pl.BlockSpec(blk, cur),
        out_shape=jax.ShapeDtypeStruct((B, S, ATTN_WIDTH), BF16),
        scratch_shapes=[
            pltpu.VMEM((tq, LANES), F32),
            pltpu.VMEM((2 * tq, LANES), F32), pltpu.VMEM((2 * tq, LANES), F32),
            pltpu.VMEM((tq, LANES), F32), pltpu.VMEM((tq, LANES), F32),
            pltpu.VMEM((tq, LANES), F32), pltpu.VMEM((tq, LANES), F32),
        ],
        compiler_params=pltpu.CompilerParams(
            dimension_semantics=("arbitrary", "arbitrary", "arbitrary"), vmem_limit_bytes=VMEM_LIMIT),
        name="attention",
    )(q, k, k, v, v, g_attn, bd, bias)


def _band_bias():
    qi = np.arange(ATTN_BLOCK)[:, None]
    ki = np.arange(2 * ATTN_BLOCK)[None, :]
    dist = qi + ATTN_BLOCK - ki
    band = (dist >= 0) & (dist <= N_BACK)
    no_prev = band & (ki >= ATTN_BLOCK)
    tab = np.stack([band, no_prev]).astype(np.float32)
    return jnp.asarray((1.0 - tab) * NEG_BIG, dtype=F32)


def _first_argmax(vals):
    best = vals[0]
    idx = jnp.zeros(best.shape, jnp.int32)
    for i in range(1, len(vals)):
        upd = vals[i] > best
        idx = jnp.where(upd, i, idx)
        best = jnp.where(upd, vals[i], best)
    return best, idx


def _out_route_body(x_ref, attn_ref, conv_ref, wo_ref, gffn_ref, wr_ref, br_ref, upper_ref,
                    h1_ref, rt_ref, rg_ref, cnt_ref, cnt):
    ts = x_ref.shape[0]
    h1 = (x_ref[...]
          + jnp.dot(attn_ref[...], wo_ref[0:ATTN_WIDTH, :], preferred_element_type=F32)
          + jnp.dot(conv_ref[...], wo_ref[ATTN_WIDTH:, :], preferred_element_type=F32))
    h1_ref[...] = h1
    ms = jnp.mean(h1 * h1, axis=-1, keepdims=True)
    xn = h1 * lax.rsqrt(ms + EPS) * gffn_ref[...]

    logits = jnp.dot(xn.astype(BF16), wr_ref[...], preferred_element_type=F32) + br_ref[...]
    lt = logits.T

    lg = [lt[ROUTER_GROUP_COL + i:ROUTER_GROUP_COL + i + 1, :] for i in range(N_GROUPS)]
    gbest, gi = _first_argmax(lg)
    sumexp = lg[0] * 0.0
    for i in range(N_GROUPS):
        sumexp = sumexp + jnp.exp(lg[i] - gbest)
    pg_top = 1.0 / sumexp

    sel = []
    for jx in range(EXPERTS_PER_GROUP):
        cand = [lt[ROUTER_EXPERT_COL + EXPERTS_PER_GROUP * g + jx:ROUTER_EXPERT_COL + EXPERTS_PER_GROUP * g + jx + 1, :]
                for g in range(N_GROUPS)]
        vj = cand[N_GROUPS - 1]
        for g in range(N_GROUPS - 2, -1, -1):
            vj = jnp.where(gi == g, cand[g], vj)
        sel.append(vj)
    b1, i1 = _first_argmax(sel)
    masked = [jnp.where(i1 == jx, -jnp.inf, sel[jx]) for jx in range(EXPERTS_PER_GROUP)]
    b2, i2 = _first_argmax(masked)
    e2 = jnp.exp(b2 - b1)
    t1 = 1.0 / (1.0 + e2)
    ga = pg_top * t1
    gb = pg_top * (e2 * t1)
    lo = jnp.minimum(i1, i2)
    hi = jnp.maximum(i1, i2)
    g_lo = jnp.where(i1 < i2, ga, gb)
    g_hi = jnp.where(i1 < i2, gb, ga)
    pair = hi - lo - 1
    for a in range(1, EXPERTS_PER_GROUP - 1):
        pair = pair + jnp.where(lo >= a, EXPERT_PAIRS.index((a, a + 1)) - EXPERT_PAIRS.index((a - 1, a)), 0)
    cls = gi * len(EXPERT_PAIRS) + pair

    r8 = lax.broadcasted_iota(jnp.int32, (8, ts), 0)
    rg_ref[...] = jnp.where(r8 == 0, g_lo, jnp.where(r8 == 1, g_hi, 0.0))

    @pl.when(jnp.logical_and(pl.program_id(0) == 0, pl.program_id(1) == 0))
    def _():
        cnt[...] = jnp.zeros(cnt.shape, F32)

    crow = lax.broadcasted_iota(jnp.int32, (CLASS_ROWS, ts), 0)
    oh = (crow == cls).astype(F32)
    before = jnp.dot(oh.astype(BF16), upper_ref[...], preferred_element_type=F32)
    rank = jnp.sum(oh * (before + cnt[:, 0:1]), axis=0, keepdims=True)
    cnt[...] = cnt[...] + jnp.sum(oh, axis=1, keepdims=True)
    cnt_ref[...] = cnt[...]
    rt_ref[...] = jnp.where(r8 == 0, cls, jnp.where(r8 == 1, rank.astype(jnp.int32), 0))


def _out_route(x, attn, conv, w_out, g_ffn, wr, b_r):
    B, S, _ = x.shape
    ts = ROW_TILE
    row = lambda b, j: (b, j, 0)
    const2 = lambda b, j: (0, 0)
    tok = np.arange(ts)
    upper = jnp.asarray((tok[:, None] < tok[None, :]).astype(np.float32), dtype=BF16)
    return pl.pallas_call(
        _out_route_body,
        grid=(B, S // ts),
        in_specs=[
            pl.BlockSpec((None, ts, D_MODEL), row),
            pl.BlockSpec((None, ts, ATTN_WIDTH), row),
            pl.BlockSpec((None, ts, CONV_WIDTH), row),
            pl.BlockSpec((D_MODEL, D_MODEL), const2),
            pl.BlockSpec((1, D_MODEL), const2),
            pl.BlockSpec((D_MODEL, GATE_LANES), const2),
            pl.BlockSpec((1, GATE_LANES), const2),
            pl.BlockSpec((ts, ts), const2),
        ],
        out_specs=[
            pl.BlockSpec((None, ts, D_MODEL), row),
            pl.BlockSpec((None, None, 8, ts), lambda b, j: (b, j, 0, 0)),
            pl.BlockSpec((None, None, 8, ts), lambda b, j: (b, j, 0, 0)),
            pl.BlockSpec((CLASS_ROWS, LANES), const2),
        ],
        out_shape=[
            jax.ShapeDtypeStruct((B, S, D_MODEL), F32),
            jax.ShapeDtypeStruct((B, S // ts, 8, ts), jnp.int32),
            jax.ShapeDtypeStruct((B, S // ts, 8, ts), F32),
            jax.ShapeDtypeStruct((CLASS_ROWS, LANES), F32),
        ],
        scratch_shapes=[pltpu.VMEM((CLASS_ROWS, LANES), F32)],
        compiler_params=pltpu.CompilerParams(
            dimension_semantics=("arbitrary", "arbitrary"), vmem_limit_bytes=VMEM_LIMIT),
        name="out_route",
    )(x, attn, conv, w_out, g_ffn, wr, b_r, upper)


def _route_plan_body(cnt_ref, rt_ref, elo_ref, ehi_ref, pos_ref, sched_ref, cinfo_ref):
    n_tiles_tok, _, ts = pos_ref.shape
    shift = MOE_TILE.bit_length() - 1
    ntile = (cnt_ref[...].astype(jnp.int32) + (MOE_TILE - 1)) >> shift
    crow = lax.broadcasted_iota(jnp.int32, (CLASS_ROWS, LANES), 0)
    incl = ntile
    step = 1
    while step < CLASS_ROWS:
        incl = incl + jnp.where(crow >= step, pltpu.roll(incl, step, 0), 0)
        step *= 2
    tstart = incl - ntile
    total = incl[CLASS_ROWS - 1:CLASS_ROWS, 0:1]

    row_base = jnp.broadcast_to(tstart[:, 0:1] * MOE_TILE, (CLASS_ROWS, ts))
    crow_t = lax.broadcasted_iota(jnp.int32, (CLASS_ROWS, ts), 0)

    def token_tile(i, carry):
        cls = rt_ref[i, 0:1, :]
        rank = rt_ref[i, 1:2, :]
        base = jnp.sum(jnp.where(crow_t == cls, row_base, 0), axis=0, keepdims=True)
        pos_ref[i] = base + rank
        return carry

    lax.fori_loop(0, n_tiles_tok, token_tile, 0)

    lane_j = lax.broadcasted_iota(jnp.int32, (CLASS_ROWS, SCHED_LANES), 1)
    start_b = jnp.broadcast_to(tstart[:, 0:1], (CLASS_ROWS, SCHED_LANES))
    ntile_b = jnp.broadcast_to(ntile[:, 0:1], (CLASS_ROWS, SCHED_LANES))
    member = jnp.logical_and(lane_j >= start_b, lane_j < start_b + ntile_b)
    pick = lambda tab: jnp.sum(jnp.where(member, jnp.broadcast_to(tab[:, 0:1], member.shape), 0),
                               axis=0, keepdims=True)
    valid = jnp.sum(member.astype(jnp.int32), axis=0, keepdims=True)
    elo = pick(elo_ref[...])
    ehi = pick(ehi_ref[...])
    j1 = lane_j[0:1, :]
    last = total - 1
    at_last = j1 == last
    elo_last = jnp.sum(jnp.where(at_last, elo, 0), axis=1, keepdims=True)
    ehi_last = jnp.sum(jnp.where(at_last, ehi, 0), axis=1, keepdims=True)
    in_use = valid > 0
    elo = jnp.where(in_use, elo, elo_last)
    ehi = jnp.where(in_use, ehi, ehi_last)
    blk = jnp.minimum(j1, last)
    r8 = lax.broadcasted_iota(jnp.int32, (8, SCHED_LANES), 0)
    sched_ref[...] = jnp.where(r8 == 0, elo, jnp.where(r8 == 1, ehi, jnp.where(r8 == 2, valid,
                               jnp.where(r8 == 3, blk, 0))))
    lane_c = lax.broadcasted_iota(jnp.int32, (CLASS_ROWS, LANES), 1)
    cinfo_ref[...] = jnp.where(lane_c == 0, tstart + ntile - 1,
                               jnp.where(lane_c == 1, (ntile > 0).astype(jnp.int32), total))


def _route_plan(cnt, rt):
    n_tiles_tok, _, ts = rt.shape
    elo_tab = np.zeros((CLASS_ROWS, LANES), np.int32)
    ehi_tab = np.zeros((CLASS_ROWS, LANES), np.int32)
    for g in range(N_GROUPS):
        for p, (a, b) in enumerate(EXPERT_PAIRS):
            elo_tab[g * len(EXPERT_PAIRS) + p, :] = g * EXPERTS_PER_GROUP + a
            ehi_tab[g * len(EXPERT_PAIRS) + p, :] = g * EXPERTS_PER_GROUP + b
    full = lambda shape: pl.BlockSpec(shape, lambda i: (0,) * len(shape))
    return pl.pallas_call(
        _route_plan_body,
        grid=(1,),
        in_specs=[full((CLASS_ROWS, LANES)), full(rt.shape), full((CLASS_ROWS, LANES)), full((CLASS_ROWS, LANES))],
        out_specs=[full((n_tiles_tok, 1, ts)), full((8, SCHED_LANES)), full((CLASS_ROWS, LANES))],
        out_shape=[
            jax.ShapeDtypeStruct((n_tiles_tok, 1, ts), jnp.int32),
            jax.ShapeDtypeStruct((8, SCHED_LANES), jnp.int32),
            jax.ShapeDtypeStruct((CLASS_ROWS, LANES), jnp.int32),
        ],
        compiler_params=pltpu.CompilerParams(dimension_semantics=("arbitrary",)),
        name="route_plan",
    )(cnt, rt, jnp.asarray(elo_tab), jnp.asarray(ehi_tab))


DMA_UNROLL = 8


def _start_rows(n_rows, make_copy):
    for t in range(n_rows):
        make_copy(t).start(priority=t % 2)


def _wait_rows(n_rows, make_copy):
    def trip(t8, carry):
        for _ in range(DMA_UNROLL):
            make_copy(0).wait()
        return carry
    lax.fori_loop(0, n_rows // DMA_UNROLL, trip, 0)


def _scatter_rows_body(last_ref, has_ref, total_ref, pos_ref, h1_ref, gffn_ref, rg_ref, xs_hbm,
                       rows, zbuf, sem, zsem):
    i = pl.program_id(0)
    n = pl.num_programs(0)
    ts = h1_ref.shape[0]
    slot = i % 2
    tile_rows = MOE_TILE * SLAB_ROWS

    n_dst_tiles = xs_hbm.shape[0] // tile_rows

    def pad_copies():
        for c in range(N_CLASSES):
            yield has_ref[c] != 0, last_ref[c]
            yield total_ref[0] + c < n_dst_tiles, total_ref[0] + c

    def tile_copy(tile):
        return pltpu.make_async_copy(zbuf, xs_hbm.at[pl.ds(tile * tile_rows, tile_rows)], zsem)

    @pl.when(i == 0)
    def _():
        rows[...] = jnp.zeros(rows.shape, rows.dtype)
        zbuf[...] = jnp.zeros(zbuf.shape, zbuf.dtype)
        for cond, tile in pad_copies():
            @pl.when(cond)
            def _():
                tile_copy(tile).start()
        for cond, tile in pad_copies():
            @pl.when(cond)
            def _():
                tile_copy(tile).wait()

    h1 = h1_ref[...]
    ms = jnp.mean(h1 * h1, axis=-1, keepdims=True)
    xn = h1 * lax.rsqrt(ms + EPS) * gffn_ref[...]
    for k in range(ROW_SUBTILES):
        rows[slot, pl.ds(k, ts, stride=SLAB_ROWS), :] = xn[:, k * LANES:(k + 1) * LANES]
    erow = lax.broadcasted_iota(jnp.int32, (GATE_LANES, ts), 0)
    gt = (jnp.where(erow == GATE_LO_LANE, rg_ref[0:1, :], 0.0)
          + jnp.where(erow == GATE_HI_LANE, rg_ref[1:2, :], 0.0))
    rows[slot, pl.ds(ROW_SUBTILES, ts, stride=SLAB_ROWS), :] = gt.T

    def row_copy(sl):
        return lambda t: pltpu.make_async_copy(
            rows.at[sl, pl.ds(t * SLAB_ROWS, SLAB_ROWS)],
            xs_hbm.at[pl.ds(pl.multiple_of(pos_ref[0, t] * SLAB_ROWS, SLAB_ROWS), SLAB_ROWS)], sem.at[sl])

    _start_rows(ts, row_copy(slot))

    @pl.when(i > 0)
    def _():
        _wait_rows(ts, row_copy(1 - slot))

    @pl.when(i == n - 1)
    def _():
        _wait_rows(ts, row_copy(slot))


def _scatter_rows(cinfo, pos, h1, g_ffn, rg, n_dst_rows):
    n_tiles_tok, _, ts = pos.shape
    return pl.pallas_call(
        _scatter_rows_body,
        grid_spec=pltpu.PrefetchScalarGridSpec(
            num_scalar_prefetch=3,
            grid=(n_tiles_tok,),
            in_specs=[
                pl.BlockSpec((None, 1, ts), lambda i, *_: (i, 0, 0), memory_space=pltpu.SMEM),
                pl.BlockSpec((ts, D_MODEL), lambda i, *_: (i, 0)),
                pl.BlockSpec((1, D_MODEL), lambda i, *_: (0, 0)),
                pl.BlockSpec((None, 8, ts), lambda i, *_: (i, 0, 0)),
            ],
            out_specs=pl.BlockSpec(memory_space=pl.ANY),
            scratch_shapes=[
                pltpu.VMEM((2, ts * SLAB_ROWS, LANES), F32),
                pltpu.VMEM((MOE_TILE * SLAB_ROWS, LANES), F32),
                pltpu.SemaphoreType.DMA((2,)),
                pltpu.SemaphoreType.DMA(()),
            ],
        ),
        out_shape=jax.ShapeDtypeStruct((n_dst_rows * SLAB_ROWS, LANES), F32),
        compiler_params=pltpu.CompilerParams(dimension_semantics=("arbitrary",), vmem_limit_bytes=VMEM_LIMIT),
        name="scatter_rows",
    )(cinfo[:, 0], cinfo[:, 1], cinfo[0:1, 2], pos, h1, g_ffn, rg)


def _moe_body(elo_ref, ehi_ref, valid_ref, blk_ref, xs_ref, w1l_ref, w3l_ref, w2l_ref, w1h_ref, w3h_ref, w2h_ref,
              ys_ref, w1l, w3l, w2l, w1h, w3h, w2h):
    j = pl.program_id(0)
    prev = jnp.maximum(j - 1, 0)
    for ids, pairs in ((elo_ref, ((w1l_ref, w1l), (w3l_ref, w3l), (w2l_ref, w2l))),
                       (ehi_ref, ((w1h_ref, w1h), (w3h_ref, w3h), (w2h_ref, w2h)))):
        @pl.when(jnp.logical_or(j == 0, ids[j] != ids[prev]))
        def _():
            for src, dst in pairs:
                dst[...] = src[...]

    @pl.when(valid_ref[pl.program_id(0)] != 0)
    def _():
        slab_row = lambda k: xs_ref[pl.ds(k, MOE_TILE, stride=SLAB_ROWS), :]
        x = jnp.concatenate([slab_row(k).astype(BF16) for k in range(ROW_SUBTILES)], axis=1)
        gates = slab_row(ROW_SUBTILES)

        def expert(w1_ref, w3_ref, w2_ref):
            a = jnp.dot(x, w1_ref[...], preferred_element_type=F32)
            b = jnp.dot(x, w3_ref[...], preferred_element_type=F32)
            hdn = (a * (1.0 / (1.0 + jnp.exp(-a))) * b).astype(BF16)
            return jnp.dot(hdn, w2_ref[...], preferred_element_type=F32)

        g_lo = gates[:, GATE_LO_LANE:GATE_LO_LANE + 1]
        g_hi = gates[:, GATE_HI_LANE:GATE_HI_LANE + 1]
        y = g_lo * expert(w1l, w3l, w2l) + g_hi * expert(w1h, w3h, w2h)
        for k in range(ROW_SUBTILES):
            ys_ref[pl.ds(k, MOE_TILE, stride=ROW_SUBTILES), :] = y[:, k * LANES:(k + 1) * LANES]

    @pl.when(valid_ref[pl.program_id(0)] == 0)
    def _():
        ys_ref[...] = jnp.zeros(ys_ref.shape, F32)


def _moe(sched, xs, w1, w3, w2):
    n_tiles = xs.shape[0] // (MOE_TILE * SLAB_ROWS)
    rows = lambda j, elo, ehi, valid, blk: (blk[j], 0)
    w_lo = lambda j, elo, ehi, valid, blk: (elo[j], 0, 0)
    w_hi = lambda j, elo, ehi, valid, blk: (ehi[j], 0, 0)
    up = (None, D_MODEL, D_FF)
    down = (None, D_FF, D_MODEL)
    return pl.pallas_call(
        _moe_body,
        grid_spec=pltpu.PrefetchScalarGridSpec(
            num_scalar_prefetch=4,
            grid=(n_tiles,),
            in_specs=[
                pl.BlockSpec((MOE_TILE * SLAB_ROWS, LANES), rows),
                pl.BlockSpec(up, w_lo), pl.BlockSpec(up, w_lo), pl.BlockSpec(down, w_lo),
                pl.BlockSpec(up, w_hi), pl.BlockSpec(up, w_hi), pl.BlockSpec(down, w_hi),
            ],
            out_specs=pl.BlockSpec((MOE_TILE * ROW_SUBTILES, LANES), lambda j, *_: (j, 0)),
            scratch_shapes=[pltpu.VMEM(up[1:], BF16), pltpu.VMEM(up[1:], BF16), pltpu.VMEM(down[1:], BF16)] * 2,
        ),
        out_shape=jax.ShapeDtypeStruct((n_tiles * MOE_TILE * ROW_SUBTILES, LANES), F32),
        compiler_params=pltpu.CompilerParams(dimension_semantics=("arbitrary",), vmem_limit_bytes=VMEM_LIMIT),
        name="moe",
    )(sched[0, :n_tiles], sched[1, :n_tiles], sched[2, :n_tiles], sched[3, :n_tiles], xs, w1, w3, w2, w1, w3, w2)


def _ple_body(pos_ref, posn_ref, h1_ref, ys_hbm, p_ref, gple_ref, wg_ref, wp_ref, gpost_ref, o_ref, ybuf, sem):
    i = pl.program_id(0)
    n = pl.num_programs(0)
    ts = h1_ref.shape[0]
    slot = i % 2

    def row_copy(pref, sl):
        return lambda t: pltpu.make_async_copy(
            ys_hbm.at[pl.ds(pl.multiple_of(pref[0, t] * ROW_SUBTILES, ROW_SUBTILES), ROW_SUBTILES)],
            ybuf.at[sl, pl.ds(t * ROW_SUBTILES, ROW_SUBTILES)], sem.at[sl])

    @pl.when(i == 0)
    def _():
        _start_rows(ts, row_copy(pos_ref, 0))

    _wait_rows(ts, row_copy(pos_ref, slot))
    _start_rows(ts, row_copy(posn_ref, 1 - slot))

    y = jnp.concatenate([ybuf[slot, pl.ds(k, ts, stride=ROW_SUBTILES), :] for k in range(ROW_SUBTILES)], axis=1)
    h2 = h1_ref[...] + y
    ms = jnp.mean(h2 * h2, axis=-1, keepdims=True)
    hn = (h2 * lax.rsqrt(ms + EPS) * gple_ref[...]).astype(BF16)
    z = jnp.dot(hn, wg_ref[...], preferred_element_type=F32)
    gate = 1.0 / (1.0 + jnp.exp(-z))
    pp = jnp.dot(p_ref[...].astype(BF16), wp_ref[...], preferred_element_type=F32)
    pms = jnp.mean(pp * pp, axis=-1, keepdims=True)
    pn = pp * lax.rsqrt(pms + EPS) * gpost_ref[...]
    o_ref[...] = h2 + gate * pn

    @pl.when(i == n - 1)
    def _():
        _wait_rows(ts, row_copy(posn_ref, 1 - slot))


def _ple(pos, h1, ys, p, g_ple, wg, wp, g_post):
    T = h1.shape[0]
    n_tiles_tok, _, ts = pos.shape
    row = lambda i: (i, 0)
    const = lambda i: (0, 0)
    return pl.pallas_call(
        _ple_body,
        grid=(n_tiles_tok,),
        in_specs=[
            pl.BlockSpec((None, 1, ts), lambda i: (i, 0, 0), memory_space=pltpu.SMEM),
            pl.BlockSpec((None, 1, ts), lambda i: (jnp.minimum(i + 1, n_tiles_tok - 1), 0, 0),
                         memory_space=pltpu.SMEM),
            pl.BlockSpec((ts, D_MODEL), row),
            pl.BlockSpec(memory_space=pl.ANY),
            pl.BlockSpec((ts, PLE_DIM), row),
            pl.BlockSpec((1, D_MODEL), const),
            pl.BlockSpec((D_MODEL, D_MODEL), const),
            pl.BlockSpec((PLE_DIM, D_MODEL), const),
            pl.BlockSpec((1, D_MODEL), const),
        ],
        out_specs=pl.BlockSpec((ts, D_MODEL), row),
        out_shape=jax.ShapeDtypeStruct((T, D_MODEL), F32),
        scratch_shapes=[pltpu.VMEM((2, ts * ROW_SUBTILES, LANES), F32), pltpu.SemaphoreType.DMA((2,))],
        compiler_params=pltpu.CompilerParams(
            dimension_semantics=("arbitrary",), vmem_limit_bytes=VMEM_LIMIT),
        name="ple",
    )(pos, pos, h1, ys, p, g_ple, wg, wp, g_post)


def _rope_tables(S):
    pos = jnp.arange(S)
    inv = ROPE_THETA ** (-jnp.arange(0, ROPE_DIM, 2, dtype=F32) / ROPE_DIM)
    ang = inv[:, None] * pos.astype(F32)[None, :]
    cos, sin = jnp.cos(ang), jnp.sin(ang)
    lane = np.arange(LANES) % HEAD_DIM
    freq = np.arange(ROPE_HALF)[:, None]
    first = (lane[None, :] == freq).astype(np.float32)
    second = (lane[None, :] == freq + ROPE_HALF).astype(np.float32)
    rest = (lane >= ROPE_DIM).astype(np.float32)[None, :]
    spread = lambda t, m: lax.dot_general(t, jnp.asarray(m), (((0,), (0,)), ((), ())),
                                          precision=lax.Precision.HIGHEST)
    return spread(cos, first + second) + rest, spread(sin, -first), spread(sin, second)


def _layer(h, p_i, g_mix, w_in, q_norm, k_norm, conv_w, g_attn_out, g_conv_out, w_out, g_ffn,
           w_rg, b_rg, w_re, b_re, w1, w3, w2, g_ple, w_ple_gate, w_ple_proj, g_ple_post):
    B, S, _ = h.shape
    T = B * S
    row = lambda g: g.reshape(1, -1).astype(F32)

    cos_t, sa_t, sb_t = _rope_tables(S)
    bd256 = _block_diag_ones(MXU_DIM, HEAD_DIM)
    bd128 = _block_diag_ones(LANES, HEAD_DIM)
    gq = row(jnp.tile(q_norm, N_HEADS) * (HEAD_DIM ** -0.5 * LOG2_E))
    gk = row(jnp.tile(k_norm, N_HEADS))
    cw = jnp.zeros((8, CONV_WIDTH), F32).at[0:CONV_K].set(conv_w)

    q, k, v, convn = _in_proj(h, row(g_mix), w_in.astype(BF16), gq, gk, cos_t, sa_t, sb_t, bd256, cw,
                              row(g_conv_out))
    attn = _attention(q, k, v, row(g_attn_out), bd128, _band_bias())

    w_r = jnp.zeros((D_MODEL, GATE_LANES), F32)
    w_r = w_r.at[:, ROUTER_GROUP_COL:ROUTER_GROUP_COL + N_GROUPS].set(w_rg)
    w_r = w_r.at[:, ROUTER_EXPERT_COL:ROUTER_EXPERT_COL + N_EXPERTS].set(w_re)
    b_r = jnp.zeros((1, GATE_LANES), F32)
    b_r = b_r.at[0, ROUTER_GROUP_COL:ROUTER_GROUP_COL + N_GROUPS].set(b_rg)
    b_r = b_r.at[0, ROUTER_EXPERT_COL:ROUTER_EXPERT_COL + N_EXPERTS].set(b_re)
    wr = w_r.astype(BF16)

    h1, rt, rg, cnt = _out_route(h, attn, convn, w_out.astype(BF16), row(g_ffn), wr, b_r)
    n_tok_tiles = T // ROW_TILE
    n_sorted_tiles = T // MOE_TILE + N_CLASSES
    assert n_sorted_tiles <= SCHED_LANES
    pos, sched, cinfo = _route_plan(cnt, rt.reshape(n_tok_tiles, 8, ROW_TILE))
    h1 = h1.reshape(T, D_MODEL)
    xs = _scatter_rows(cinfo, pos, h1, row(g_ffn), rg.reshape(n_tok_tiles, 8, ROW_TILE),
                       n_sorted_tiles * MOE_TILE)
    ys = _moe(sched, xs, w1.astype(BF16), w3.astype(BF16), w2.astype(BF16))
    out = _ple(pos, h1, ys, p_i.reshape(T, PLE_DIM), row(g_ple),
               w_ple_gate.astype(BF16), w_ple_proj.astype(BF16), row(g_ple_post))
    return out.reshape(B, S, D_MODEL)


def kernel(x, p, g_mix, w_in, q_norm, k_norm, conv_w, g_attn_out, g_conv_out, w_out, g_ffn, w_router_group, b_router_group, w_router_expert, b_router_expert, w1, w3, w2, g_ple, w_ple_gate, w_ple_proj, g_ple_post):
    h = x
    for i in range(p.shape[0]):
        h = _layer(h, p[i], g_mix[i], w_in[i], q_norm[i], k_norm[i], conv_w[i], g_attn_out[i], g_conv_out[i],
                   w_out[i], g_ffn[i], w_router_group[i], b_router_group[i], w_router_expert[i],
                   b_router_expert[i], w1[i], w3[i], w2[i], g_ple[i], w_ple_gate[i], w_ple_proj[i],
                   g_ple_post[i])
    return h
```

```python
import functools

import numpy as np
import jax
import jax.numpy as jnp
from jax import lax
from jax.experimental import pallas as pl
from jax.experimental.pallas import tpu as pltpu

F32 = jnp.float32
BF16 = jnp.bfloat16

D_MODEL = 1024
PLE_DIM = 256
HEAD_DIM = 64
N_HEADS = 8
ATTN_WIDTH = N_HEADS * HEAD_DIM
CONV_WIDTH = D_MODEL - ATTN_WIDTH
CONV_K = 3
DILATIONS = (1, 4, 16)
N_BACK = 128
ATTN_BLOCK = 128
ROPE_THETA = 500000.0
ROPE_DIM = HEAD_DIM // 4
ROPE_HALF = ROPE_DIM // 2
N_GROUPS = 4
EXPERTS_PER_GROUP = 4
N_EXPERTS = N_GROUPS * EXPERTS_PER_GROUP
D_FF = 512
EPS = 1e-6

LANES = 128
MXU_DIM = 256
NEG_BIG = -1e30
LOG2_E = 1.4426950408889634

ROW_TILE = 1024
IN_PROJ_TILE = 1024
ROW_SUBSTEPS = 1
ATTN_TILE = 2048
ATTN_UNROLL = 16
MOE_TILE = 256
ROW_SUBTILES = D_MODEL // LANES
GATE_LANES = LANES
SLAB_ROWS = 16
GATE_LO_LANE = 0
GATE_HI_LANE = 1
EXPERT_PAIRS = tuple((a, b) for a in range(EXPERTS_PER_GROUP) for b in range(a + 1, EXPERTS_PER_GROUP))
N_CLASSES = N_GROUPS * len(EXPERT_PAIRS)
CLASS_ROWS = 32
SCHED_LANES = 256
ROUTER_GROUP_COL = 0
ROUTER_EXPERT_COL = 8
VMEM_LIMIT = 48 * 1024 * 1024


def _block_diag_ones(n, seg):
    idx = np.arange(n) // seg
    return jnp.asarray((idx[:, None] == idx[None, :]).astype(np.float32), dtype=BF16)


def _segment_mean_sq(t, bd):
    t2 = (t * t).astype(BF16)
    parts = [jnp.dot(t2[:, i:i + MXU_DIM], bd, preferred_element_type=F32)
             for i in range(0, t.shape[1], MXU_DIM)]
    return jnp.concatenate(parts, axis=1) * (1.0 / HEAD_DIM)


def _in_proj_body(x_ref, gmix_ref, win_ref, gq_ref, gk_ref, cos_ref, sa_ref, sb_ref, bd_ref, cw_ref, gconv_ref,
                  q_ref, k_ref, v_ref, conv_ref, ubuf):
    ts = x_ref.shape[0]
    sub = ts // ROW_SUBSTEPS
    bd = bd_ref[...]
    reps = ATTN_WIDTH // LANES

    @pl.when(pl.program_id(1) == 0)
    def _():
        ubuf[0:8, :] = jnp.zeros((8, CONV_WIDTH), F32)

    for r0 in range(0, ts, sub):
        rows = pl.ds(r0, sub)
        x = x_ref[rows, :]
        ms = jnp.mean(x * x, axis=-1, keepdims=True)
        xn = (x * lax.rsqrt(ms + EPS) * gmix_ref[...]).astype(BF16)

        def proj(c):
            return jnp.dot(xn, win_ref[:, c * ATTN_WIDTH:(c + 1) * ATTN_WIDTH], preferred_element_type=F32)

        cos = jnp.concatenate([cos_ref[rows, :]] * reps, axis=1)
        sa = jnp.concatenate([sa_ref[rows, :]] * reps, axis=1)
        sb = jnp.concatenate([sb_ref[rows, :]] * reps, axis=1)

        def norm_rope(t, g):
            tn = t * lax.rsqrt(_segment_mean_sq(t, bd) + EPS) * g
            up = pltpu.roll(tn, ATTN_WIDTH - ROPE_HALF, 1)
            dn = pltpu.roll(tn, ROPE_HALF, 1)
            return tn * cos + up * sa + dn * sb

        q_ref[rows, :] = norm_rope(proj(0), gq_ref[...]).astype(BF16)
        k_ref[rows, :] = norm_rope(proj(1), gk_ref[...]).astype(BF16)
        v_ref[rows, :] = proj(2).astype(BF16)

        cb = proj(3)
        u = proj(4) * proj(5)
        ubuf[8 + r0:8 + r0 + sub, :] = u
        u1 = ubuf[7 + r0:7 + r0 + sub, :]
        u2 = ubuf[6 + r0:6 + r0 + sub, :]
        y = cw_ref[0:1, :] * u2 + cw_ref[1:2, :] * u1 + cw_ref[2:3, :] * u
        conv = cb * y
        convn = conv * lax.rsqrt(_segment_mean_sq(conv, bd) + EPS) * gconv_ref[...]
        conv_ref[rows, :] = convn.astype(BF16)

    ubuf[0:8, :] = ubuf[ts:ts + 8, :]


def _in_proj(x, g_mix, w_in, gq, gk, cos_t, sa_t, sb_t, bd, cw, g_conv):
    B, S, _ = x.shape
    ts = IN_PROJ_TILE
    row = lambda b, j: (b, j, 0)
    const2 = lambda b, j: (0, 0)
    tab = lambda b, j: (j, 0)
    out_sds = jax.ShapeDtypeStruct((B, S, ATTN_WIDTH), BF16)
    return pl.pallas_call(
        _in_proj_body,
        grid=(B, S // ts),
        in_specs=[
            pl.BlockSpec((None, ts, D_MODEL), row),
            pl.BlockSpec((1, D_MODEL), const2),
            pl.BlockSpec((D_MODEL, 6 * ATTN_WIDTH), const2),
            pl.BlockSpec((1, ATTN_WIDTH), const2),
            pl.BlockSpec((1, ATTN_WIDTH), const2),
            pl.BlockSpec((ts, LANES), tab),
            pl.BlockSpec((ts, LANES), tab),
            pl.BlockSpec((ts, LANES), tab),
            pl.BlockSpec((MXU_DIM, MXU_DIM), const2),
            pl.BlockSpec((8, CONV_WIDTH), const2),
            pl.BlockSpec((1, CONV_WIDTH), const2),
        ],
        out_specs=[pl.BlockSpec((None, ts, ATTN_WIDTH), row)] * 4,
        out_shape=[out_sds] * 4,
        scratch_shapes=[pltpu.VMEM((ts + 8, CONV_WIDTH), F32)],
        compiler_params=pltpu.CompilerParams(
            dimension_semantics=("arbitrary", "arbitrary"), vmem_limit_bytes=VMEM_LIMIT),
        name="in_proj",
    )(x, g_mix, w_in, gq, gk, cos_t, sa_t, sb_t, bd, cw, g_conv)


def _attn_body(q_ref, kp_ref, kc_ref, vp_ref, vc_ref, gat_ref, bd_ref, bias_ref, o_ref,
               qf, kf, vf, acc_a, acc_b, m_a, m_b):
    tq = q_ref.shape[0]
    blk = ATTN_BLOCK
    first_tile = pl.program_id(2) == 0

    qf[...] = q_ref[...].astype(F32)
    kf[0:tq, :] = kp_ref[...].astype(F32)
    kf[tq:2 * tq, :] = kc_ref[...].astype(F32)
    vf[0:tq, :] = vp_ref[...].astype(F32)
    vf[tq:2 * tq, :] = vc_ref[...].astype(F32)

    lane = lax.broadcasted_iota(jnp.int32, (1, LANES), 1)
    sel_a = (lane < HEAD_DIM).astype(F32).astype(BF16)
    sel_b = (lane >= HEAD_DIM).astype(F32).astype(BF16)

    order = DILATIONS[::-1]
    for d in order:
        nblk = tq // (blk * d)
        shift = nblk.bit_length() - 1
        first_visit = d == order[0]

        def block(idx, carry, d=d, nblk=nblk, shift=shift, first_visit=first_visit):
            r = idx >> shift
            n = idx & (nblk - 1)
            rows_q = pl.ds(r + d * blk * n, blk, stride=d)
            rows_k = pl.ds(tq + d * blk * (n - 1) + r, 2 * blk, stride=d)
            qw = qf[rows_q, :].astype(BF16)
            kw = kf[rows_k, :].astype(BF16)
            vw = vf[rows_k, :].astype(BF16)
            bias = bias_ref[jnp.where(jnp.logical_and(first_tile, n == 0), 1, 0)]

            def one_head(sel_q, sel_one, acc, mst):
                s = lax.dot_general(qw * sel_q, kw, (((1,), (1,)), ((), ())), preferred_element_type=F32) + bias
                mb = jnp.max(s, axis=-1, keepdims=True)
                if first_visit:
                    m_new = jnp.broadcast_to(mb, (blk, LANES))
                else:
                    m_old = mst[rows_q, :]
                    m_new = jnp.maximum(m_old, mb)
                e = jnp.exp2(s - jnp.concatenate([m_new, m_new], axis=1))
                pv = jnp.dot(e.astype(BF16), vw * sel_q + sel_one, preferred_element_type=F32)
                if first_visit:
                    acc[rows_q, :] = pv
                else:
                    acc[rows_q, :] = acc[rows_q, :] * jnp.exp2(m_old - m_new) + pv
                mst[rows_q, :] = m_new

            one_head(sel_a, sel_b, acc_a, m_a)
            one_head(sel_b, sel_a, acc_b, m_b)
            return carry

        lax.fori_loop(0, tq // blk, block, 0, unroll=ATTN_UNROLL)

    head_a = lax.broadcasted_iota(jnp.int32, (tq, LANES), 1) < HEAD_DIM
    aa = acc_a[...]
    ab = acc_b[...]
    num = jnp.where(head_a, aa, ab)
    den = pltpu.roll(jnp.where(head_a, ab, aa), HEAD_DIM, 1)
    o = num / den
    o2 = (o * o).astype(BF16)
    msq = jnp.dot(o2, bd_ref[...], preferred_element_type=F32) * (1.0 / HEAD_DIM)
    o_ref[...] = (o * lax.rsqrt(msq + EPS) * gat_ref[...]).astype(BF16)


def _attention(q, k, v, g_attn, bd, bias):
    B, S, _ = q.shape
    tq = ATTN_TILE
    n_pairs = ATTN_WIDTH // LANES
    cur = lambda b, hp, j: (b, j, hp)
    prev = lambda b, hp, j: (b, jnp.maximum(j - 1, 0), hp)
    blk = (None, tq, LANES)
    return pl.pallas_call(
        _attn_body,
        grid=(B, n_pairs, S // tq),
        in_specs=[
            pl.BlockSpec(blk, cur),
            pl.BlockSpec(blk, prev),
            pl.BlockSpec(blk, cur),
            pl.BlockSpec(blk, prev),
            pl.BlockSpec(blk, cur),
            pl.BlockSpec((1, LANES), lambda b, hp, j: (0, hp)),
            pl.BlockSpec((LANES, LANES), lambda b, hp, j: (0, 0)),
            pl.BlockSpec((2, ATTN_BLOCK, 2 * ATTN_BLOCK), lambda b, hp, j: (0, 0, 0)),
        ],
        out_specs=pl.BlockSpec(blk, cur),
        out_shape=jax.ShapeDtypeStruct((B, S, ATTN_WIDTH), BF16),
        scratch_shapes=[
            pltpu.VMEM((tq, LANES), F32),
            pltpu.VMEM((2 * tq, LANES), F32), pltpu.VMEM((2 * tq, LANES), F32),
            pltpu.VMEM((tq, LANES), F32), pltpu.VMEM((tq, LANES), F32),
            pltpu.VMEM((tq, LANES), F32), pltpu.VMEM((tq, LANES), F32),
        ],
        compiler_params=pltpu.CompilerParams(
            dimension_semantics=("arbitrary", "arbitrary", "arbitrary"), vmem_limit_bytes=VMEM_LIMIT),
        name="attention",
    )(q, k, k, v, v, g_attn, bd, bias)


def _band_bias():
    qi = np.arange(ATTN_BLOCK)[:, None]
    ki = np.arange(2 * ATTN_BLOCK)[None, :]
    dist = qi + ATTN_BLOCK - ki
    band = (dist >= 0) & (dist <= N_BACK)
    no_prev = band & (ki >= ATTN_BLOCK)
    tab = np.stack([band, no_prev]).astype(np.float32)
    return jnp.asarray((1.0 - tab) * NEG_BIG, dtype=F32)


def _first_argmax(vals):
    best = vals[0]
    idx = jnp.zeros(best.shape, jnp.int32)
    for i in range(1, len(vals)):
        upd = vals[i] > best
        idx = jnp.where(upd, i, idx)
        best = jnp.where(upd, vals[i], best)
    return best, idx


def _out_route_body(x_ref, attn_ref, conv_ref, wo_ref, gffn_ref, wr_ref, br_ref, upper_ref,
                    h1_ref, rt_ref, rg_ref, cnt_ref, cnt):
    ts = x_ref.shape[0]
    h1 = (x_ref[...]
          + jnp.dot(attn_ref[...], wo_ref[0:ATTN_WIDTH, :], preferred_element_type=F32)
          + jnp.dot(conv_ref[...], wo_ref[ATTN_WIDTH:, :], preferred_element_type=F32))
    h1_ref[...] = h1
    ms = jnp.mean(h1 * h1, axis=-1, keepdims=True)
    xn = h1 * lax.rsqrt(ms + EPS) * gffn_ref[...]

    logits = jnp.dot(xn.astype(BF16), wr_ref[...], preferred_element_type=F32) + br_ref[...]
    lt = logits.T

    lg = [lt[ROUTER_GROUP_COL + i:ROUTER_GROUP_COL + i + 1, :] for i in range(N_GROUPS)]
    gbest, gi = _first_argmax(lg)
    sumexp = lg[0] * 0.0
    for i in range(N_GROUPS):
        sumexp = sumexp + jnp.exp(lg[i] - gbest)
    pg_top = 1.0 / sumexp

    sel = []
    for jx in range(EXPERTS_PER_GROUP):
        cand = [lt[ROUTER_EXPERT_COL + EXPERTS_PER_GROUP * g + jx:ROUTER_EXPERT_COL + EXPERTS_PER_GROUP * g + jx + 1, :]
                for g in range(N_GROUPS)]
        vj = cand[N_GROUPS - 1]
        for g in range(N_GROUPS - 2, -1, -1):
            vj = jnp.where(gi == g, cand[g], vj)
        sel.append(vj)
    b1, i1 = _first_argmax(sel)
    masked = [jnp.where(i1 == jx, -jnp.inf, sel[jx]) for jx in range(EXPERTS_PER_GROUP)]
    b2, i2 = _first_argmax(masked)
    e2 = jnp.exp(b2 - b1)
    t1 = 1.0 / (1.0 + e2)
    ga = pg_top * t1
    gb = pg_top * (e2 * t1)
    lo = jnp.minimum(i1, i2)
    hi = jnp.maximum(i1, i2)
    g_lo = jnp.where(i1 < i2, ga, gb)
    g_hi = jnp.where(i1 < i2, gb, ga)
    pair = hi - lo - 1
    for a in range(1, EXPERTS_PER_GROUP - 1):
        pair = pair + jnp.where(lo >= a, EXPERT_PAIRS.index((a, a + 1)) - EXPERT_PAIRS.index((a - 1, a)), 0)
    cls = gi * len(EXPERT_PAIRS) + pair

    r8 = lax.broadcasted_iota(jnp.int32, (8, ts), 0)
    rg_ref[...] = jnp.where(r8 == 0, g_lo, jnp.where(r8 == 1, g_hi, 0.0))

    @pl.when(jnp.logical_and(pl.program_id(0) == 0, pl.program_id(1) == 0))
    def _():
        cnt[...] = jnp.zeros(cnt.shape, F32)

    crow = lax.broadcasted_iota(jnp.int32, (CLASS_ROWS, ts), 0)
    oh = (crow == cls).astype(F32)
    before = jnp.dot(oh.astype(BF16), upper_ref[...], preferred_element_type=F32)
    rank = jnp.sum(oh * (before + cnt[:, 0:1]), axis=0, keepdims=True)
    cnt[...] = cnt[...] + jnp.sum(oh, axis=1, keepdims=True)
    cnt_ref[...] = cnt[...]
    rt_ref[...] = jnp.where(r8 == 0, cls, jnp.where(r8 == 1, rank.astype(jnp.int32), 0))


def _out_route(x, attn, conv, w_out, g_ffn, wr, b_r):
    B, S, _ = x.shape
    ts = ROW_TILE
    row = lambda b, j: (b, j, 0)
    const2 = lambda b, j: (0, 0)
    tok = np.arange(ts)
    upper = jnp.asarray((tok[:, None] < tok[None, :]).astype(np.float32), dtype=BF16)
    return pl.pallas_call(
        _out_route_body,
        grid=(B, S // ts),
        in_specs=[
            pl.BlockSpec((None, ts, D_MODEL), row),
            pl.BlockSpec((None, ts, ATTN_WIDTH), row),
            pl.BlockSpec((None, ts, CONV_WIDTH), row),
            pl.BlockSpec((D_MODEL, D_MODEL), const2),
            pl.BlockSpec((1, D_MODEL), const2),
            pl.BlockSpec((D_MODEL, GATE_LANES), const2),
            pl.BlockSpec((1, GATE_LANES), const2),
            pl.BlockSpec((ts, ts), const2),
        ],
        out_specs=[
            pl.BlockSpec((None, ts, D_MODEL), row),
            pl.BlockSpec((None, None, 8, ts), lambda b, j: (b, j, 0, 0)),
            pl.BlockSpec((None, None, 8, ts), lambda b, j: (b, j, 0, 0)),
            pl.BlockSpec((CLASS_ROWS, LANES), const2),
        ],
        out_shape=[
            jax.ShapeDtypeStruct((B, S, D_MODEL), F32),
            jax.ShapeDtypeStruct((B, S // ts, 8, ts), jnp.int32),
            jax.ShapeDtypeStruct((B, S // ts, 8, ts), F32),
            jax.ShapeDtypeStruct((CLASS_ROWS, LANES), F32),
        ],
        scratch_shapes=[pltpu.VMEM((CLASS_ROWS, LANES), F32)],
        compiler_params=pltpu.CompilerParams(
            dimension_semantics=("arbitrary", "arbitrary"), vmem_limit_bytes=VMEM_LIMIT),
        name="out_route",
    )(x, attn, conv, w_out, g_ffn, wr, b_r, upper)


def _route_plan_body(cnt_ref, rt_ref, elo_ref, ehi_ref, pos_ref, sched_ref, cinfo_ref):
    n_tiles_tok, _, ts = pos_ref.shape
    shift = MOE_TILE.bit_length() - 1
    ntile = (cnt_ref[...].astype(jnp.int32) + (MOE_TILE - 1)) >> shift
    crow = lax.broadcasted_iota(jnp.int32, (CLASS_ROWS, LANES), 0)
    incl = ntile
    step = 1
    while step < CLASS_ROWS:
        incl = incl + jnp.where(crow >= step, pltpu.roll(incl, step, 0), 0)
        step *= 2
    tstart = incl - ntile
    total = incl[CLASS_ROWS - 1:CLASS_ROWS, 0:1]

    row_base = jnp.broadcast_to(tstart[:, 0:1] * MOE_TILE, (CLASS_ROWS, ts))
    crow_t = lax.broadcasted_iota(jnp.int32, (CLASS_ROWS, ts), 0)

    def token_tile(i, carry):
        cls = rt_ref[i, 0:1, :]
        rank = rt_ref[i, 1:2, :]
        base = jnp.sum(jnp.where(crow_t == cls, row_base, 0), axis=0, keepdims=True)
        pos_ref[i] = base + rank
        return carry

    lax.fori_loop(0, n_tiles_tok, token_tile, 0)

    lane_j = lax.broadcasted_iota(jnp.int32, (CLASS_ROWS, SCHED_LANES), 1)
    start_b = jnp.broadcast_to(tstart[:, 0:1], (CLASS_ROWS, SCHED_LANES))
    ntile_b = jnp.broadcast_to(ntile[:, 0:1], (CLASS_ROWS, SCHED_LANES))
    member = jnp.logical_and(lane_j >= start_b, lane_j < start_b + ntile_b)
    pick = lambda tab: jnp.sum(jnp.where(member, jnp.broadcast_to(tab[:, 0:1], member.shape), 0),
                               axis=0, keepdims=True)
    valid = jnp.sum(member.astype(jnp.int32), axis=0, keepdims=True)
    elo = pick(elo_ref[...])
    ehi = pick(ehi_ref[...])
    j1 = lane_j[0:1, :]
    last = total - 1
    at_last = j1 == last
    elo_last = jnp.sum(jnp.where(at_last, elo, 0), axis=1, keepdims=True)
    ehi_last = jnp.sum(jnp.where(at_last, ehi, 0), axis=1, keepdims=True)
    in_use = valid > 0
    elo = jnp.where(in_use, elo, elo_last)
    ehi = jnp.where(in_use, ehi, ehi_last)
    blk = jnp.minimum(j1, last)
    r8 = lax.broadcasted_iota(jnp.int32, (8, SCHED_LANES), 0)
    sched_ref[...] = jnp.where(r8 == 0, elo, jnp.where(r8 == 1, ehi, jnp.where(r8 == 2, valid,
                               jnp.where(r8 == 3, blk, 0))))
    lane_c = lax.broadcasted_iota(jnp.int32, (CLASS_ROWS, LANES), 1)
    cinfo_ref[...] = jnp.where(lane_c == 0, tstart + ntile - 1,
                               jnp.where(lane_c == 1, (ntile > 0).astype(jnp.int32), total))


def _route_plan(cnt, rt):
    n_tiles_tok, _, ts = rt.shape
    elo_tab = np.zeros((CLASS_ROWS, LANES), np.int32)
    ehi_tab = np.zeros((CLASS_ROWS, LANES), np.int32)
    for g in range(N_GROUPS):
        for p, (a, b) in enumerate(EXPERT_PAIRS):
            elo_tab[g * len(EXPERT_PAIRS) + p, :] = g * EXPERTS_PER_GROUP + a
            ehi_tab[g * len(EXPERT_PAIRS) + p, :] = g * EXPERTS_PER_GROUP + b
    full = lambda shape: pl.BlockSpec(shape, lambda i: (0,) * len(shape))
    return pl.pallas_call(
        _route_plan_body,
        grid=(1,),
        in_specs=[full((CLASS_ROWS, LANES)), full(rt.shape), full((CLASS_ROWS, LANES)), full((CLASS_ROWS, LANES))],
        out_specs=[full((n_tiles_tok, 1, ts)), full((8, SCHED_LANES)), full((CLASS_ROWS, LANES))],
        out_shape=[
            jax.ShapeDtypeStruct((n_tiles_tok, 1, ts), jnp.int32),
            jax.ShapeDtypeStruct((8, SCHED_LANES), jnp.int32),
            jax.ShapeDtypeStruct((CLASS_ROWS, LANES), jnp.int32),
        ],
        compiler_params=pltpu.CompilerParams(dimension_semantics=("arbitrary",)),
        name="route_plan",
    )(cnt, rt, jnp.asarray(elo_tab), jnp.asarray(ehi_tab))


DMA_UNROLL = 8


def _start_rows(n_rows, make_copy):
    for t in range(n_rows):
        make_copy(t).start(priority=t % 2)


def _wait_rows(n_rows, make_copy):
    def trip(t8, carry):
        for _ in range(DMA_UNROLL):
            make_copy(0).wait()
        return carry
    lax.fori_loop(0, n_rows // DMA_UNROLL, trip, 0)


def _scatter_rows_body(last_ref, has_ref, total_ref, pos_ref, h1_ref, gffn_ref, rg_ref, xs_hbm,
                       rows, zbuf, sem, zsem):
    i = pl.program_id(0)
    n = pl.num_programs(0)
    ts = h1_ref.shape[0]
    slot = i % 2
    tile_rows = MOE_TILE * SLAB_ROWS

    n_dst_tiles = xs_hbm.shape[0] // tile_rows

    def pad_copies():
        for c in range(N_CLASSES):
            yield has_ref[c] != 0, last_ref[c]
            yield total_ref[0] + c < n_dst_tiles, total_ref[0] + c

    def tile_copy(tile):
        return pltpu.make_async_copy(zbuf, xs_hbm.at[pl.ds(tile * tile_rows, tile_rows)], zsem)

    @pl.when(i == 0)
    def _():
        rows[...] = jnp.zeros(rows.shape, rows.dtype)
        zbuf[...] = jnp.zeros(zbuf.shape, zbuf.dtype)
        for cond, tile in pad_copies():
            @pl.when(cond)
            def _():
                tile_copy(tile).start()
        for cond, tile in pad_copies():
            @pl.when(cond)
            def _():
                tile_copy(tile).wait()

    h1 = h1_ref[...]
    ms = jnp.mean(h1 * h1, axis=-1, keepdims=True)
    xn = h1 * lax.rsqrt(ms + EPS) * gffn_ref[...]
    for k in range(ROW_SUBTILES):
        rows[slot, pl.ds(k, ts, stride=SLAB_ROWS), :] = xn[:, k * LANES:(k + 1) * LANES]
    erow = lax.broadcasted_iota(jnp.int32, (GATE_LANES, ts), 0)
    gt = (jnp.where(erow == GATE_LO_LANE, rg_ref[0:1, :], 0.0)
          + jnp.where(erow == GATE_HI_LANE, rg_ref[1:2, :], 0.0))
    rows[slot, pl.ds(ROW_SUBTILES, ts, stride=SLAB_ROWS), :] = gt.T

    def row_copy(sl):
        return lambda t: pltpu.make_async_copy(
            rows.at[sl, pl.ds(t * SLAB_ROWS, SLAB_ROWS)],
            xs_hbm.at[pl.ds(pl.multiple_of(pos_ref[0, t] * SLAB_ROWS, SLAB_ROWS), SLAB_ROWS)], sem.at[sl])

    _start_rows(ts, row_copy(slot))

    @pl.when(i > 0)
    def _():
        _wait_rows(ts, row_copy(1 - slot))

    @pl.when(i == n - 1)
    def _():
        _wait_rows(ts, row_copy(slot))


def _scatter_rows(cinfo, pos, h1, g_ffn, rg, n_dst_rows):
    n_tiles_tok, _, ts = pos.shape
    return pl.pallas_call(
        _scatter_rows_body,
        grid_spec=pltpu.PrefetchScalarGridSpec(
            num_scalar_prefetch=3,
            grid=(n_tiles_tok,),
            in_specs=[
                pl.BlockSpec((None, 1, ts), lambda i, *_: (i, 0, 0), memory_space=pltpu.SMEM),
                pl.BlockSpec((ts, D_MODEL), lambda i, *_: (i, 0)),
                pl.BlockSpec((1, D_MODEL), lambda i, *_: (0, 0)),
                pl.BlockSpec((None, 8, ts), lambda i, *_: (i, 0, 0)),
            ],
            out_specs=pl.BlockSpec(memory_space=pl.ANY),
            scratch_shapes=[
                pltpu.VMEM((2, ts * SLAB_ROWS, LANES), F32),
                pltpu.VMEM((MOE_TILE * SLAB_ROWS, LANES), F32),
                pltpu.SemaphoreType.DMA((2,)),
                pltpu.SemaphoreType.DMA(()),
            ],
        ),
        out_shape=jax.ShapeDtypeStruct((n_dst_rows * SLAB_ROWS, LANES), F32),
        compiler_params=pltpu.CompilerParams(dimension_semantics=("arbitrary",), vmem_limit_bytes=VMEM_LIMIT),
        name="scatter_rows",
    )(cinfo[:, 0], cinfo[:, 1], cinfo[0:1, 2], pos, h1, g_ffn, rg)


def _moe_body(elo_ref, ehi_ref, valid_ref, blk_ref, xs_ref, w1l_ref, w3l_ref, w2l_ref, w1h_ref, w3h_ref, w2h_ref,
              ys_ref, w1l, w3l, w2l, w1h, w3h, w2h):
    j = pl.program_id(0)
    prev = jnp.maximum(j - 1, 0)
    for ids, pairs in ((elo_ref, ((w1l_ref, w1l), (w3l_ref, w3l), (w2l_ref, w2l))),
                       (ehi_ref, ((w1h_ref, w1h), (w3h_ref, w3h), (w2h_ref, w2h)))):
        @pl.when(jnp.logical_or(j == 0, ids[j] != ids[prev]))
        def _():
            for src, dst in pairs:
                dst[...] = src[...]

    @pl.when(valid_ref[pl.program_id(0)] != 0)
    def _():
        slab_row = lambda k: xs_ref[pl.ds(k, MOE_TILE, stride=SLAB_ROWS), :]
        x = jnp.concatenate([slab_row(k).astype(BF16) for k in range(ROW_SUBTILES)], axis=1)
        gates = slab_row(ROW_SUBTILES)

        def expert(w1_ref, w3_ref, w2_ref):
            a = jnp.dot(x, w1_ref[...], preferred_element_type=F32)
            b = jnp.dot(x, w3_ref[...], preferred_element_type=F32)
            hdn = (a * (1.0 / (1.0 + jnp.exp(-a))) * b).astype(BF16)
            return jnp.dot(hdn, w2_ref[...], preferred_element_type=F32)

        g_lo = gates[:, GATE_LO_LANE:GATE_LO_LANE + 1]
        g_hi = gates[:, GATE_HI_LANE:GATE_HI_LANE + 1]
        y = g_lo * expert(w1l, w3l, w2l) + g_hi * expert(w1h, w3h, w2h)
        for k in range(ROW_SUBTILES):
            ys_ref[pl.ds(k, MOE_TILE, stride=ROW_SUBTILES), :] = y[:, k * LANES:(k + 1) * LANES]

    @pl.when(valid_ref[pl.program_id(0)] == 0)
    def _():
        ys_ref[...] = jnp.zeros(ys_ref.shape, F32)


def _moe(sched, xs, w1, w3, w2):
    n_tiles = xs.shape[0] // (MOE_TILE * SLAB_ROWS)
    rows = lambda j, elo, ehi, valid, blk: (blk[j], 0)
    w_lo = lambda j, elo, ehi, valid, blk: (elo[j], 0, 0)
    w_hi = lambda j, elo, ehi, valid, blk: (ehi[j], 0, 0)
    up = (None, D_MODEL, D_FF)
    down = (None, D_FF, D_MODEL)
    return pl.pallas_call(
        _moe_body,
        grid_spec=pltpu.PrefetchScalarGridSpec(
            num_scalar_prefetch=4,
            grid=(n_tiles,),
            in_specs=[
                pl.BlockSpec((MOE_TILE * SLAB_ROWS, LANES), rows),
                pl.BlockSpec(up, w_lo), pl.BlockSpec(up, w_lo), pl.BlockSpec(down, w_lo),
                pl.BlockSpec(up, w_hi), pl.BlockSpec(up, w_hi), pl.BlockSpec(down, w_hi),
            ],
            out_specs=pl.BlockSpec((MOE_TILE * ROW_SUBTILES, LANES), lambda j, *_: (j, 0)),
            scratch_shapes=[pltpu.VMEM(up[1:], BF16), pltpu.VMEM(up[1:], BF16), pltpu.VMEM(down[1:], BF16)] * 2,
        ),
        out_shape=jax.ShapeDtypeStruct((n_tiles * MOE_TILE * ROW_SUBTILES, LANES), F32),
        compiler_params=pltpu.CompilerParams(dimension_semantics=("arbitrary",), vmem_limit_bytes=VMEM_LIMIT),
        name="moe",
    )(sched[0, :n_tiles], sched[1, :n_tiles], sched[2, :n_tiles], sched[3, :n_tiles], xs, w1, w3, w2, w1, w3, w2)


def _ple_body(pos_ref, posn_ref, h1_ref, ys_hbm, p_ref, gple_ref, wg_ref, wp_ref, gpost_ref, o_ref, ybuf, sem):
    i = pl.program_id(0)
    n = pl.num_programs(0)
    ts = h1_ref.shape[0]
    slot = i % 2

    def row_copy(pref, sl):
        return lambda t: pltpu.make_async_copy(
            ys_hbm.at[pl.ds(pl.multiple_of(pref[0, t] * ROW_SUBTILES, ROW_SUBTILES), ROW_SUBTILES)],
            ybuf.at[sl, pl.ds(t * ROW_SUBTILES, ROW_SUBTILES)], sem.at[sl])

    @pl.when(i == 0)
    def _():
        _start_rows(ts, row_copy(pos_ref, 0))

    _wait_rows(ts, row_copy(pos_ref, slot))
    _start_rows(ts, row_copy(posn_ref, 1 - slot))

    y = jnp.concatenate([ybuf[slot, pl.ds(k, ts, stride=ROW_SUBTILES), :] for k in range(ROW_SUBTILES)], axis=1)
    h2 = h1_ref[...] + y
    ms = jnp.mean(h2 * h2, axis=-1, keepdims=True)
    hn = (h2 * lax.rsqrt(ms + EPS) * gple_ref[...]).astype(BF16)
    z = jnp.dot(hn, wg_ref[...], preferred_element_type=F32)
    gate = 1.0 / (1.0 + jnp.exp(-z))
    pp = jnp.dot(p_ref[...].astype(BF16), wp_ref[...], preferred_element_type=F32)
    pms = jnp.mean(pp * pp, axis=-1, keepdims=True)
    pn = pp * lax.rsqrt(pms + EPS) * gpost_ref[...]
    o_ref[...] = h2 + gate * pn

    @pl.when(i == n - 1)
    def _():
        _wait_rows(ts, row_copy(posn_ref, 1 - slot))


def _ple(pos, h1, ys, p, g_ple, wg, wp, g_post):
    T = h1.shape[0]
    n_tiles_tok, _, ts = pos.shape
    row = lambda i: (i, 0)
    const = lambda i: (0, 0)
    return pl.pallas_call(
        _ple_body,
        grid=(n_tiles_tok,),
        in_specs=[
            pl.BlockSpec((None, 1, ts), lambda i: (i, 0, 0), memory_space=pltpu.SMEM),
            pl.BlockSpec((None, 1, ts), lambda i: (jnp.minimum(i + 1, n_tiles_tok - 1), 0, 0),
                         memory_space=pltpu.SMEM),
            pl.BlockSpec((ts, D_MODEL), row),
            pl.BlockSpec(memory_space=pl.ANY),
            pl.BlockSpec((ts, PLE_DIM), row),
            pl.BlockSpec((1, D_MODEL), const),
            pl.BlockSpec((D_MODEL, D_MODEL), const),
            pl.BlockSpec((PLE_DIM, D_MODEL), const),
            pl.BlockSpec((1, D_MODEL), const),
        ],
        out_specs=pl.BlockSpec((ts, D_MODEL), row),
        out_shape=jax.ShapeDtypeStruct((T, D_MODEL), F32),
        scratch_shapes=[pltpu.VMEM((2, ts * ROW_SUBTILES, LANES), F32), pltpu.SemaphoreType.DMA((2,))],
        compiler_params=pltpu.CompilerParams(
            dimension_semantics=("arbitrary",), vmem_limit_bytes=VMEM_LIMIT),
        name="ple",
    )(pos, pos, h1, ys, p, g_ple, wg, wp, g_post)


def _rope_tables(S):
    pos = jnp.arange(S)
    inv = ROPE_THETA ** (-jnp.arange(0, ROPE_DIM, 2, dtype=F32) / ROPE_DIM)
    ang = inv[:, None] * pos.astype(F32)[None, :]
    cos, sin = jnp.cos(ang), jnp.sin(ang)
    lane = np.arange(LANES) % HEAD_DIM
    freq = np.arange(ROPE_HALF)[:, None]
    first = (lane[None, :] == freq).astype(np.float32)
    second = (lane[None, :] == freq + ROPE_HALF).astype(np.float32)
    rest = (lane >= ROPE_DIM).astype(np.float32)[None, :]
    spread = lambda t, m: lax.dot_general(t, jnp.asarray(m), (((0,), (0,)), ((), ())),
                                          precision=lax.Precision.HIGHEST)
    return spread(cos, first + second) + rest, spread(sin, -first), spread(sin, second)


def _layer(h, p_i, g_mix, w_in, q_norm, k_norm, conv_w, g_attn_out, g_conv_out, w_out, g_ffn,
           w_rg, b_rg, w_re, b_re, w1, w3, w2, g_ple, w_ple_gate, w_ple_proj, g_ple_post):
    B, S, _ = h.shape
    T = B * S
    row = lambda g: g.reshape(1, -1).astype(F32)

    cos_t, sa_t, sb_t = _rope_tables(S)
    bd256 = _block_diag_ones(MXU_DIM, HEAD_DIM)
    bd128 = _block_diag_ones(LANES, HEAD_DIM)
    gq = row(jnp.tile(q_norm, N_HEADS) * (HEAD_DIM ** -0.5 * LOG2_E))
    gk = row(jnp.tile(k_norm, N_HEADS))
    cw = jnp.zeros((8, CONV_WIDTH), F32).at[0:CONV_K].set(conv_w)

    q, k, v, convn = _in_proj(h, row(g_mix), w_in.astype(BF16), gq, gk, cos_t, sa_t, sb_t, bd256, cw,
                              row(g_conv_out))
    attn = _attention(q, k, v, row(g_attn_out), bd128, _band_bias())

    w_r = jnp.zeros((D_MODEL, GATE_LANES), F32)
    w_r = w_r.at[:, ROUTER_GROUP_COL:ROUTER_GROUP_COL + N_GROUPS].set(w_rg)
    w_r = w_r.at[:, ROUTER_EXPERT_COL:ROUTER_EXPERT_COL + N_EXPERTS].set(w_re)
    b_r = jnp.zeros((1, GATE_LANES), F32)
    b_r = b_r.at[0, ROUTER_GROUP_COL:ROUTER_GROUP_COL + N_GROUPS].set(b_rg)
    b_r = b_r.at[0, ROUTER_EXPERT_COL:ROUTER_EXPERT_COL + N_EXPERTS].set(b_re)
    wr = w_r.astype(BF16)

    h1, rt, rg, cnt = _out_route(h, attn, convn, w_out.astype(BF16), row(g_ffn), wr, b_r)
    n_tok_tiles = T // ROW_TILE
    n_sorted_tiles = T // MOE_TILE + N_CLASSES
    assert n_sorted_tiles <= SCHED_LANES
    pos, sched, cinfo = _route_plan(cnt, rt.reshape(n_tok_tiles, 8, ROW_TILE))
    h1 = h1.reshape(T, D_MODEL)
    xs = _scatter_rows(cinfo, pos, h1, row(g_ffn), rg.reshape(n_tok_tiles, 8, ROW_TILE),
                       n_sorted_tiles * MOE_TILE)
    ys = _moe(sched, xs, w1.astype(BF16), w3.astype(BF16), w2.astype(BF16))
    out = _ple(pos, h1, ys, p_i.reshape(T, PLE_DIM), row(g_ple),
               w_ple_gate.astype(BF16), w_ple_proj.astype(BF16), row(g_ple_post))
    return out.reshape(B, S, D_MODEL)


def kernel(x, p, g_mix, w_in, q_norm, k_norm, conv_w, g_attn_out, g_conv_out, w_out, g_ffn, w_router_group, b_router_group, w_router_expert, b_router_expert, w1, w3, w2, g_ple, w_ple_gate, w_ple_proj, g_ple_post):
    h = x
    for i in range(p.shape[0]):
        h = _layer(h, p[i], g_mix[i], w_in[i], q_norm[i], k_norm[i], conv_w[i], g_attn_out[i], g_conv_out[i],
                   w_out[i], g_ffn[i], w_router_group[i], b_router_group[i], w_router_expert[i],
                   b_router_expert[i], w1[i], w3[i], w2[i], g_ple[i], w_ple_gate[i], w_ple_proj[i],
                   g_ple_post[i])
    return h
```

```python
import functools

import numpy as np
import jax
import jax.numpy as jnp
from jax import lax
from jax.experimental import pallas as pl
from jax.experimental.pallas import tpu as pltpu

F32 = jnp.float32
BF16 = jnp.bfloat16

D_MODEL = 1024
PLE_DIM = 256
HEAD_DIM = 64
N_HEADS = 8
ATTN_WIDTH = N_HEADS * HEAD_DIM
CONV_WIDTH = D_MODEL - ATTN_WIDTH
CONV_K = 3
DILATIONS = (1, 4, 16)
N_BACK = 128
ATTN_BLOCK = 128
ROPE_THETA = 500000.0
ROPE_DIM = HEAD_DIM // 4
ROPE_HALF = ROPE_DIM // 2
N_GROUPS = 4
EXPERTS_PER_GROUP = 4
N_EXPERTS = N_GROUPS * EXPERTS_PER_GROUP
D_FF = 512
EPS = 1e-6

LANES = 128
MXU_DIM = 256
NEG_BIG = -1e30
LOG2_E = 1.4426950408889634

ROW_TILE = 512
OUT_ROUTE_TILE = 1024
IN_PROJ_TILE = 1024
ROW_SUBSTEPS = 1
ATTN_TILE = 2048
ATTN_UNROLL = 16
MOE_TILE = 256
ROW_SUBTILES = D_MODEL // LANES
GATE_LANES = LANES
SLAB_ROWS = 16
GATE_LO_LANE = 0
GATE_HI_LANE = 1
EXPERT_PAIRS = tuple((a, b) for a in range(EXPERTS_PER_GROUP) for b in range(a + 1, EXPERTS_PER_GROUP))
N_CLASSES = N_GROUPS * len(EXPERT_PAIRS)
CLASS_ROWS = 32
SCHED_LANES = 256
ROUTER_GROUP_COL = 0
ROUTER_EXPERT_COL = 8
VMEM_LIMIT = 48 * 1024 * 1024


def _block_diag_ones(n, seg):
    idx = np.arange(n) // seg
    return jnp.asarray((idx[:, None] == idx[None, :]).astype(np.float32), dtype=BF16)


def _segment_mean_sq(t, bd):
    t2 = (t * t).astype(BF16)
    parts = [jnp.dot(t2[:, i:i + MXU_DIM], bd, preferred_element_type=F32)
             for i in range(0, t.shape[1], MXU_DIM)]
    return jnp.concatenate(parts, axis=1) * (1.0 / HEAD_DIM)


def _in_proj_body(x_ref, gmix_ref, win_ref, gq_ref, gk_ref, cos_ref, sa_ref, sb_ref, bd_ref, cw_ref, gconv_ref,
                  q_ref, k_ref, v_ref, conv_ref, ubuf):
    ts = x_ref.shape[0]
    sub = ts // ROW_SUBSTEPS
    bd = bd_ref[...]
    reps = ATTN_WIDTH // LANES

    @pl.when(pl.program_id(1) == 0)
    def _():
        ubuf[0:8, :] = jnp.zeros((8, CONV_WIDTH), F32)

    for r0 in range(0, ts, sub):
        rows = pl.ds(r0, sub)
        x = x_ref[rows, :]
        ms = jnp.mean(x * x, axis=-1, keepdims=True)
        xn = (x * lax.rsqrt(ms + EPS) * gmix_ref[...]).astype(BF16)

        def proj(c):
            return jnp.dot(xn, win_ref[:, c * ATTN_WIDTH:(c + 1) * ATTN_WIDTH], preferred_element_type=F32)

        cos = jnp.concatenate([cos_ref[rows, :]] * reps, axis=1)
        sa = jnp.concatenate([sa_ref[rows, :]] * reps, axis=1)
        sb = jnp.concatenate([sb_ref[rows, :]] * reps, axis=1)

        def norm_rope(t, g):
            tn = t * lax.rsqrt(_segment_mean_sq(t, bd) + EPS) * g
            up = pltpu.roll(tn, ATTN_WIDTH - ROPE_HALF, 1)
            dn = pltpu.roll(tn, ROPE_HALF, 1)
            return tn * cos + up * sa + dn * sb

        q_ref[rows, :] = norm_rope(proj(0), gq_ref[...]).astype(BF16)
        k_ref[rows, :] = norm_rope(proj(1), gk_ref[...]).astype(BF16)
        v_ref[rows, :] = proj(2).astype(BF16)

        cb = proj(3)
        u = proj(4) * proj(5)
        ubuf[8 + r0:8 + r0 + sub, :] = u
        u1 = ubuf[7 + r0:7 + r0 + sub, :]
        u2 = ubuf[6 + r0:6 + r0 + sub, :]
        y = cw_ref[0:1, :] * u2 + cw_ref[1:2, :] * u1 + cw_ref[2:3, :] * u
        conv = cb * y
        convn = conv * lax.rsqrt(_segment_mean_sq(conv, bd) + EPS) * gconv_ref[...]
        conv_ref[rows, :] = convn.astype(BF16)

    ubuf[0:8, :] = ubuf[ts:ts + 8, :]


def _in_proj(x, g_mix, w_in, gq, gk, cos_t, sa_t, sb_t, bd, cw, g_conv):
    B, S, _ = x.shape
    ts = IN_PROJ_TILE
    row = lambda b, j: (b, j, 0)
    const2 = lambda b, j: (0, 0)
    tab = lambda b, j: (j, 0)
    out_sds = jax.ShapeDtypeStruct((B, S, ATTN_WIDTH), BF16)
    return pl.pallas_call(
        _in_proj_body,
        grid=(B, S // ts),
        in_specs=[
            pl.BlockSpec((None, ts, D_MODEL), row),
            pl.BlockSpec((1, D_MODEL), const2),
            pl.BlockSpec((D_MODEL, 6 * ATTN_WIDTH), const2),
            pl.BlockSpec((1, ATTN_WIDTH), const2),
            pl.BlockSpec((1, ATTN_WIDTH), const2),
            pl.BlockSpec((ts, LANES), tab),
            pl.BlockSpec((ts, LANES), tab),
            pl.BlockSpec((ts, LANES), tab),
            pl.BlockSpec((MXU_DIM, MXU_DIM), const2),
            pl.BlockSpec((8, CONV_WIDTH), const2),
            pl.BlockSpec((1, CONV_WIDTH), const2),
        ],
        out_specs=[pl.BlockSpec((None, ts, ATTN_WIDTH), row)] * 4,
        out_shape=[out_sds] * 4,
        scratch_shapes=[pltpu.VMEM((ts + 8, CONV_WIDTH), F32)],
        compiler_params=pltpu.CompilerParams(
            dimension_semantics=("arbitrary", "arbitrary"), vmem_limit_bytes=VMEM_LIMIT),
        name="in_proj",
    )(x, g_mix, w_in, gq, gk, cos_t, sa_t, sb_t, bd, cw, g_conv)


def _attn_body(q_ref, kp_ref, kc_ref, vp_ref, vc_ref, gat_ref, bd_ref, bias_ref, o_ref,
               qf, kf, vf, acc_a, acc_b, m_a, m_b):
    tq = q_ref.shape[0]
    blk = ATTN_BLOCK
    first_tile = pl.program_id(2) == 0

    qf[...] = q_ref[...].astype(F32)
    kf[0:tq, :] = kp_ref[...].astype(F32)
    kf[tq:2 * tq, :] = kc_ref[...].astype(F32)
    vf[0:tq, :] = vp_ref[...].astype(F32)
    vf[tq:2 * tq, :] = vc_ref[...].astype(F32)

    lane = lax.broadcasted_iota(jnp.int32, (1, LANES), 1)
    sel_a = (lane < HEAD_DIM).astype(F32).astype(BF16)
    sel_b = (lane >= HEAD_DIM).astype(F32).astype(BF16)

    order = DILATIONS[::-1]
    for d in order:
        nblk = tq // (blk * d)
        shift = nblk.bit_length() - 1
        first_visit = d == order[0]

        def block(idx, carry, d=d, nblk=nblk, shift=shift, first_visit=first_visit):
            r = idx >> shift
            n = idx & (nblk - 1)
            rows_q = pl.ds(r + d * blk * n, blk, stride=d)
            rows_k = pl.ds(tq + d * blk * (n - 1) + r, 2 * blk, stride=d)
            qw = qf[rows_q, :].astype(BF16)
            kw = kf[rows_k, :].astype(BF16)
            vw = vf[rows_k, :].astype(BF16)
            bias = bias_ref[jnp.where(jnp.logical_and(first_tile, n == 0), 1, 0)]

            def one_head(sel_q, sel_one, acc, mst):
                s = lax.dot_general(qw * sel_q, kw, (((1,), (1,)), ((), ())), preferred_element_type=F32) + bias
                mb = jnp.max(s, axis=-1, keepdims=True)
                if first_visit:
                    m_new = jnp.broadcast_to(mb, (blk, LANES))
                else:
                    m_old = mst[rows_q, :]
                    m_new = jnp.maximum(m_old, mb)
                e = jnp.exp2(s - jnp.concatenate([m_new, m_new], axis=1))
                pv = jnp.dot(e.astype(BF16), vw * sel_q + sel_one, preferred_element_type=F32)
                if first_visit:
                    acc[rows_q, :] = pv
                else:
                    acc[rows_q, :] = acc[rows_q, :] * jnp.exp2(m_old - m_new) + pv
                mst[rows_q, :] = m_new

            one_head(sel_a, sel_b, acc_a, m_a)
            one_head(sel_b, sel_a, acc_b, m_b)
            return carry

        lax.fori_loop(0, tq // blk, block, 0, unroll=ATTN_UNROLL)

    head_a = lax.broadcasted_iota(jnp.int32, (tq, LANES), 1) < HEAD_DIM
    aa = acc_a[...]
    ab = acc_b[...]
    num = jnp.where(head_a, aa, ab)
    den = pltpu.roll(jnp.where(head_a, ab, aa), HEAD_DIM, 1)
    o = num / den
    o2 = (o * o).astype(BF16)
    msq = jnp.dot(o2, bd_ref[...], preferred_element_type=F32) * (1.0 / HEAD_DIM)
    o_ref[...] = (o * lax.rsqrt(msq + EPS) * gat_ref[...]).astype(BF16)


def _attention(q, k, v, g_attn, bd, bias):
    B, S, _ = q.shape
    tq = ATTN_TILE
    n_pairs = ATTN_WIDTH // LANES
    cur = lambda b, hp, j: (b, j, hp)
    prev = lambda b, hp, j: (b, jnp.maximum(j - 1, 0), hp)
    blk = (None, tq, LANES)
    return pl.pallas_call(
        _attn_body,
        grid=(B, n_pairs, S // tq),
        in_specs=[
            pl.BlockSpec(blk, cur),
            pl.BlockSpec(blk, prev),
            pl.BlockSpec(blk, cur),
            pl.BlockSpec(blk, prev),
            pl.BlockSpec(blk, cur),
            pl.BlockSpec((1, LANES), lambda b, hp, j: (0, hp)),
            pl.BlockSpec((LANES, LANES), lambda b, hp, j: (0, 0)),
            pl.BlockSpec((2, ATTN_BLOCK, 2 * ATTN_BLOCK), lambda b, hp, j: (0, 0, 0)),
        ],
        out_specs=pl.BlockSpec(blk, cur),
        out_shape=jax.ShapeDtypeStruct((B, S, ATTN_WIDTH), BF16),
        scratch_shapes=[
            pltpu.VMEM((tq, LANES), F32),
            pltpu.VMEM((2 * tq, LANES), F32), pltpu.VMEM((2 * tq, LANES), F32),
            pltpu.VMEM((tq, LANES), F32), pltpu.VMEM((tq, LANES), F32),
            pltpu.VMEM((tq, LANES), F32), pltpu.VMEM((tq, LANES), F32),
        ],
        compiler_params=pltpu.CompilerParams(
            dimension_semantics=("arbitrary", "arbitrary", "arbitrary"), vmem_limit_bytes=VMEM_LIMIT),
        name="attention",
    )(q, k, k, v, v, g_attn, bd, bias)


def _band_bias():
    qi = np.arange(ATTN_BLOCK)[:, None]
    ki = np.arange(2 * ATTN_BLOCK)[None, :]
    dist = qi + ATTN_BLOCK - ki
    band = (dist >= 0) & (dist <= N_BACK)
    no_prev = band & (ki >= ATTN_BLOCK)
    tab = np.stack([band, no_prev]).astype(np.float32)
    return jnp.asarray((1.0 - tab) * NEG_BIG, dtype=F32)


def _first_argmax(vals):
    best = vals[0]
    idx = jnp.zeros(best.shape, jnp.int32)
    for i in range(1, len(vals)):
        upd = vals[i] > best
        idx = jnp.where(upd, i, idx)
        best = jnp.where(upd, vals[i], best)
    return best, idx


def _out_route_body(x_ref, attn_ref, conv_ref, wo_ref, gffn_ref, wr_ref, br_ref, upper_ref,
                    h1_ref, rt_ref, rg_ref, cnt_ref, cnt):
    ts = x_ref.shape[0]
    h1 = (x_ref[...]
          + jnp.dot(attn_ref[...], wo_ref[0:ATTN_WIDTH, :], preferred_element_type=F32)
          + jnp.dot(conv_ref[...], wo_ref[ATTN_WIDTH:, :], preferred_element_type=F32))
    h1_ref[...] = h1
    ms = jnp.mean(h1 * h1, axis=-1, keepdims=True)
    xn = h1 * lax.rsqrt(ms + EPS) * gffn_ref[...]

    logits = jnp.dot(xn.astype(BF16), wr_ref[...], preferred_element_type=F32) + br_ref[...]
    lt = logits.T

    lg = [lt[ROUTER_GROUP_COL + i:ROUTER_GROUP_COL + i + 1, :] for i in range(N_GROUPS)]
    gbest, gi = _first_argmax(lg)
    sumexp = lg[0] * 0.0
    for i in range(N_GROUPS):
        sumexp = sumexp + jnp.exp(lg[i] - gbest)
    pg_top = 1.0 / sumexp

    sel = []
    for jx in range(EXPERTS_PER_GROUP):
        cand = [lt[ROUTER_EXPERT_COL + EXPERTS_PER_GROUP * g + jx:ROUTER_EXPERT_COL + EXPERTS_PER_GROUP * g + jx + 1, :]
                for g in range(N_GROUPS)]
        vj = cand[N_GROUPS - 1]
        for g in range(N_GROUPS - 2, -1, -1):
            vj = jnp.where(gi == g, cand[g], vj)
        sel.append(vj)
    b1, i1 = _first_argmax(sel)
    masked = [jnp.where(i1 == jx, -jnp.inf, sel[jx]) for jx in range(EXPERTS_PER_GROUP)]
    b2, i2 = _first_argmax(masked)
    e2 = jnp.exp(b2 - b1)
    t1 = 1.0 / (1.0 + e2)
    ga = pg_top * t1
    gb = pg_top * (e2 * t1)
    lo = jnp.minimum(i1, i2)
    hi = jnp.maximum(i1, i2)
    g_lo = jnp.where(i1 < i2, ga, gb)
    g_hi = jnp.where(i1 < i2, gb, ga)
    pair = hi - lo - 1
    for a in range(1, EXPERTS_PER_GROUP - 1):
        pair = pair + jnp.where(lo >= a, EXPERT_PAIRS.index((a, a + 1)) - EXPERT_PAIRS.index((a - 1, a)), 0)
    cls = gi * len(EXPERT_PAIRS) + pair

    r8 = lax.broadcasted_iota(jnp.int32, (8, ts), 0)
    rg_ref[...] = jnp.where(r8 == 0, g_lo, jnp.where(r8 == 1, g_hi, 0.0))

    @pl.when(jnp.logical_and(pl.program_id(0) == 0, pl.program_id(1) == 0))
    def _():
        cnt[...] = jnp.zeros(cnt.shape, F32)

    crow = lax.broadcasted_iota(jnp.int32, (CLASS_ROWS, ts), 0)
    oh = (crow == cls).astype(F32)
    before = jnp.dot(oh.astype(BF16), upper_ref[...], preferred_element_type=F32)
    rank = jnp.sum(oh * (before + cnt[:, 0:1]), axis=0, keepdims=True)
    cnt[...] = cnt[...] + jnp.sum(oh, axis=1, keepdims=True)
    cnt_ref[...] = cnt[...]
    rt_ref[...] = jnp.where(r8 == 0, cls, jnp.where(r8 == 1, rank.astype(jnp.int32), 0))


def _out_route(x, attn, conv, w_out, g_ffn, wr, b_r):
    B, S, _ = x.shape
    ts = OUT_ROUTE_TILE
    row = lambda b, j: (b, j, 0)
    const2 = lambda b, j: (0, 0)
    tok = np.arange(ts)
    upper = jnp.asarray((tok[:, None] < tok[None, :]).astype(np.float32), dtype=BF16)
    return pl.pallas_call(
        _out_route_body,
        grid=(B, S // ts),
        in_specs=[
            pl.BlockSpec((None, ts, D_MODEL), row),
            pl.BlockSpec((None, ts, ATTN_WIDTH), row),
            pl.BlockSpec((None, ts, CONV_WIDTH), row),
            pl.BlockSpec((D_MODEL, D_MODEL), const2),
            pl.BlockSpec((1, D_MODEL), const2),
            pl.BlockSpec((D_MODEL, GATE_LANES), const2),
            pl.BlockSpec((1, GATE_LANES), const2),
            pl.BlockSpec((ts, ts), const2),
        ],
        out_specs=[
            pl.BlockSpec((None, ts, D_MODEL), row),
            pl.BlockSpec((None, None, 8, ts), lambda b, j: (b, j, 0, 0)),
            pl.BlockSpec((None, None, 8, ts), lambda b, j: (b, j, 0, 0)),
            pl.BlockSpec((CLASS_ROWS, LANES), const2),
        ],
        out_shape=[
            jax.ShapeDtypeStruct((B, S, D_MODEL), F32),
            jax.ShapeDtypeStruct((B, S // ts, 8, ts), jnp.int32),
            jax.ShapeDtypeStruct((B, S // ts, 8, ts), F32),
            jax.ShapeDtypeStruct((CLASS_ROWS, LANES), F32),
        ],
        scratch_shapes=[pltpu.VMEM((CLASS_ROWS, LANES), F32)],
        compiler_params=pltpu.CompilerParams(
            dimension_semantics=("arbitrary", "arbitrary"), vmem_limit_bytes=VMEM_LIMIT),
        name="out_route",
    )(x, attn, conv, w_out, g_ffn, wr, b_r, upper)


def _route_plan_body(cnt_ref, rt_ref, elo_ref, ehi_ref, pos_ref, sched_ref, cinfo_ref):
    n_tiles_tok, _, ts = pos_ref.shape
    shift = MOE_TILE.bit_length() - 1
    ntile = (cnt_ref[...].astype(jnp.int32) + (MOE_TILE - 1)) >> shift
    crow = lax.broadcasted_iota(jnp.int32, (CLASS_ROWS, LANES), 0)
    incl = ntile
    step = 1
    while step < CLASS_ROWS:
        incl = incl + jnp.where(crow >= step, pltpu.roll(incl, step, 0), 0)
        step *= 2
    tstart = incl - ntile
    total = incl[CLASS_ROWS - 1:CLASS_ROWS, 0:1]

    row_base = jnp.broadcast_to(tstart[:, 0:1] * MOE_TILE, (CLASS_ROWS, ts))
    crow_t = lax.broadcasted_iota(jnp.int32, (CLASS_ROWS, ts), 0)

    def token_tile(i, carry):
        cls = rt_ref[i, 0:1, :]
        rank = rt_ref[i, 1:2, :]
        base = jnp.sum(jnp.where(crow_t == cls, row_base, 0), axis=0, keepdims=True)
        pos_ref[i] = base + rank
        return carry

    lax.fori_loop(0, n_tiles_tok, token_tile, 0)

    lane_j = lax.broadcasted_iota(jnp.int32, (CLASS_ROWS, SCHED_LANES), 1)
    start_b = jnp.broadcast_to(tstart[:, 0:1], (CLASS_ROWS, SCHED_LANES))
    ntile_b = jnp.broadcast_to(ntile[:, 0:1], (CLASS_ROWS, SCHED_LANES))
    member = jnp.logical_and(lane_j >= start_b, lane_j < start_b + ntile_b)
    pick = lambda tab: jnp.sum(jnp.where(member, jnp.broadcast_to(tab[:, 0:1], member.shape), 0),
                               axis=0, keepdims=True)
    valid = jnp.sum(member.astype(jnp.int32), axis=0, keepdims=True)
    elo = pick(elo_ref[...])
    ehi = pick(ehi_ref[...])
    j1 = lane_j[0:1, :]
    last = total - 1
    at_last = j1 == last
    elo_last = jnp.sum(jnp.where(at_last, elo, 0), axis=1, keepdims=True)
    ehi_last = jnp.sum(jnp.where(at_last, ehi, 0), axis=1, keepdims=True)
    in_use = valid > 0
    elo = jnp.where(in_use, elo, elo_last)
    ehi = jnp.where(in_use, ehi, ehi_last)
    blk = jnp.minimum(j1, last)
    r8 = lax.broadcasted_iota(jnp.int32, (8, SCHED_LANES), 0)
    sched_ref[...] = jnp.where(r8 == 0, elo, jnp.where(r8 == 1, ehi, jnp.where(r8 == 2, valid,
                               jnp.where(r8 == 3, blk, 0))))
    lane_c = lax.broadcasted_iota(jnp.int32, (CLASS_ROWS, LANES), 1)
    cinfo_ref[...] = jnp.where(lane_c == 0, tstart + ntile - 1,
                               jnp.where(lane_c == 1, (ntile > 0).astype(jnp.int32), total))


def _route_plan(cnt, rt):
    n_tiles_tok, _, ts = rt.shape
    elo_tab = np.zeros((CLASS_ROWS, LANES), np.int32)
    ehi_tab = np.zeros((CLASS_ROWS, LANES), np.int32)
    for g in range(N_GROUPS):
        for p, (a, b) in enumerate(EXPERT_PAIRS):
            elo_tab[g * len(EXPERT_PAIRS) + p, :] = g * EXPERTS_PER_GROUP + a
            ehi_tab[g * len(EXPERT_PAIRS) + p, :] = g * EXPERTS_PER_GROUP + b
    full = lambda shape: pl.BlockSpec(shape, lambda i: (0,) * len(shape))
    return pl.pallas_call(
        _route_plan_body,
        grid=(1,),
        in_specs=[full((CLASS_ROWS, LANES)), full(rt.shape), full((CLASS_ROWS, LANES)), full((CLASS_ROWS, LANES))],
        out_specs=[full((n_tiles_tok, 1, ts)), full((8, SCHED_LANES)), full((CLASS_ROWS, LANES))],
        out_shape=[
            jax.ShapeDtypeStruct((n_tiles_tok, 1, ts), jnp.int32),
            jax.ShapeDtypeStruct((8, SCHED_LANES), jnp.int32),
            jax.ShapeDtypeStruct((CLASS_ROWS, LANES), jnp.int32),
        ],
        compiler_params=pltpu.CompilerParams(dimension_semantics=("arbitrary",)),
        name="route_plan",
    )(cnt, rt, jnp.asarray(elo_tab), jnp.asarray(ehi_tab))


DMA_UNROLL = 8


def _start_rows(n_rows, make_copy):
    for t in range(n_rows):
        make_copy(t).start(priority=t % 2)


def _wait_rows(n_rows, make_copy):
    def trip(t8, carry):
        for _ in range(DMA_UNROLL):
            make_copy(0).wait()
        return carry
    lax.fori_loop(0, n_rows // DMA_UNROLL, trip, 0)


def _scatter_rows_body(last_ref, has_ref, total_ref, pos_ref, h1_ref, gffn_ref, rg_ref, xs_hbm,
                       rows, zbuf, sem, zsem):
    i = pl.program_id(0)
    n = pl.num_programs(0)
    ts = h1_ref.shape[0]
    slot = i % 2
    tile_rows = MOE_TILE * SLAB_ROWS

    n_dst_tiles = xs_hbm.shape[0] // tile_rows

    def pad_copies():
        for c in range(N_CLASSES):
            yield has_ref[c] != 0, last_ref[c]
            yield total_ref[0] + c < n_dst_tiles, total_ref[0] + c

    def tile_copy(tile):
        return pltpu.make_async_copy(zbuf, xs_hbm.at[pl.ds(tile * tile_rows, tile_rows)], zsem)

    @pl.when(i == 0)
    def _():
        rows[...] = jnp.zeros(rows.shape, rows.dtype)
        zbuf[...] = jnp.zeros(zbuf.shape, zbuf.dtype)
        for cond, tile in pad_copies():
            @pl.when(cond)
            def _():
                tile_copy(tile).start()
        for cond, tile in pad_copies():
            @pl.when(cond)
            def _():
                tile_copy(tile).wait()

    h1 = h1_ref[...]
    ms = jnp.mean(h1 * h1, axis=-1, keepdims=True)
    xn = h1 * lax.rsqrt(ms + EPS) * gffn_ref[...]
    for k in range(ROW_SUBTILES):
        rows[slot, pl.ds(k, ts, stride=SLAB_ROWS), :] = xn[:, k * LANES:(k + 1) * LANES]
    erow = lax.broadcasted_iota(jnp.int32, (GATE_LANES, ts), 0)
    gt = (jnp.where(erow == GATE_LO_LANE, rg_ref[0:1, :], 0.0)
          + jnp.where(erow == GATE_HI_LANE, rg_ref[1:2, :], 0.0))
    rows[slot, pl.ds(ROW_SUBTILES, ts, stride=SLAB_ROWS), :] = gt.T

    def row_copy(sl):
        return lambda t: pltpu.make_async_copy(
            rows.at[sl, pl.ds(t * SLAB_ROWS, SLAB_ROWS)],
            xs_hbm.at[pl.ds(pl.multiple_of(pos_ref[0, t] * SLAB_ROWS, SLAB_ROWS), SLAB_ROWS)], sem.at[sl])

    _start_rows(ts, row_copy(slot))

    @pl.when(i > 0)
    def _():
        _wait_rows(ts, row_copy(1 - slot))

    @pl.when(i == n - 1)
    def _():
        _wait_rows(ts, row_copy(slot))


def _scatter_rows(cinfo, pos, h1, g_ffn, rg, n_dst_rows):
    n_tiles_tok, _, ts = pos.shape
    return pl.pallas_call(
        _scatter_rows_body,
        grid_spec=pltpu.PrefetchScalarGridSpec(
            num_scalar_prefetch=3,
            grid=(n_tiles_tok,),
            in_specs=[
                pl.BlockSpec((None, 1, ts), lambda i, *_: (i, 0, 0), memory_space=pltpu.SMEM),
                pl.BlockSpec((ts, D_MODEL), lambda i, *_: (i, 0)),
                pl.BlockSpec((1, D_MODEL), lambda i, *_: (0, 0)),
                pl.BlockSpec((None, 8, ts), lambda i, *_: (i, 0, 0)),
            ],
            out_specs=pl.BlockSpec(memory_space=pl.ANY),
            scratch_shapes=[
                pltpu.VMEM((2, ts * SLAB_ROWS, LANES), F32),
                pltpu.VMEM((MOE_TILE * SLAB_ROWS, LANES), F32),
                pltpu.SemaphoreType.DMA((2,)),
                pltpu.SemaphoreType.DMA(()),
            ],
        ),
        out_shape=jax.ShapeDtypeStruct((n_dst_rows * SLAB_ROWS, LANES), F32),
        compiler_params=pltpu.CompilerParams(dimension_semantics=("arbitrary",), vmem_limit_bytes=VMEM_LIMIT),
        name="scatter_rows",
    )(cinfo[:, 0], cinfo[:, 1], cinfo[0:1, 2], pos, h1, g_ffn, rg)


def _moe_body(elo_ref, ehi_ref, valid_ref, blk_ref, xs_ref, w1l_ref, w3l_ref, w2l_ref, w1h_ref, w3h_ref, w2h_ref,
              ys_ref, w1l, w3l, w2l, w1h, w3h, w2h):
    j = pl.program_id(0)
    prev = jnp.maximum(j - 1, 0)
    for ids, pairs in ((elo_ref, ((w1l_ref, w1l), (w3l_ref, w3l), (w2l_ref, w2l))),
                       (ehi_ref, ((w1h_ref, w1h), (w3h_ref, w3h), (w2h_ref, w2h)))):
        @pl.when(jnp.logical_or(j == 0, ids[j] != ids[prev]))
        def _():
            for src, dst in pairs:
                dst[...] = src[...]

    @pl.when(valid_ref[pl.program_id(0)] != 0)
    def _():
        slab_row = lambda k: xs_ref[pl.ds(k, MOE_TILE, stride=SLAB_ROWS), :]
        x = jnp.concatenate([slab_row(k).astype(BF16) for k in range(ROW_SUBTILES)], axis=1)
        gates = slab_row(ROW_SUBTILES)

        def expert(w1_ref, w3_ref, w2_ref):
            a = jnp.dot(x, w1_ref[...], preferred_element_type=F32)
            b = jnp.dot(x, w3_ref[...], preferred_element_type=F32)
            hdn = (a * (1.0 / (1.0 + jnp.exp(-a))) * b).astype(BF16)
            return jnp.dot(hdn, w2_ref[...], preferred_element_type=F32)

        g_lo = gates[:, GATE_LO_LANE:GATE_LO_LANE + 1]
        g_hi = gates[:, GATE_HI_LANE:GATE_HI_LANE + 1]
        y = g_lo * expert(w1l, w3l, w2l) + g_hi * expert(w1h, w3h, w2h)
        for k in range(ROW_SUBTILES):
            ys_ref[pl.ds(k, MOE_TILE, stride=ROW_SUBTILES), :] = y[:, k * LANES:(k + 1) * LANES]

    @pl.when(valid_ref[pl.program_id(0)] == 0)
    def _():
        ys_ref[...] = jnp.zeros(ys_ref.shape, F32)


def _moe(sched, xs, w1, w3, w2):
    n_tiles = xs.shape[0] // (MOE_TILE * SLAB_ROWS)
    rows = lambda j, elo, ehi, valid, blk: (blk[j], 0)
    w_lo = lambda j, elo, ehi, valid, blk: (elo[j], 0, 0)
    w_hi = lambda j, elo, ehi, valid, blk: (ehi[j], 0, 0)
    up = (None, D_MODEL, D_FF)
    down = (None, D_FF, D_MODEL)
    return pl.pallas_call(
        _moe_body,
        grid_spec=pltpu.PrefetchScalarGridSpec(
            num_scalar_prefetch=4,
            grid=(n_tiles,),
            in_specs=[
                pl.BlockSpec((MOE_TILE * SLAB_ROWS, LANES), rows),
                pl.BlockSpec(up, w_lo), pl.BlockSpec(up, w_lo), pl.BlockSpec(down, w_lo),
                pl.BlockSpec(up, w_hi), pl.BlockSpec(up, w_hi), pl.BlockSpec(down, w_hi),
            ],
            out_specs=pl.BlockSpec((MOE_TILE * ROW_SUBTILES, LANES), lambda j, *_: (j, 0)),
            scratch_shapes=[pltpu.VMEM(up[1:], BF16), pltpu.VMEM(up[1:], BF16), pltpu.VMEM(down[1:], BF16)] * 2,
        ),
        out_shape=jax.ShapeDtypeStruct((n_tiles * MOE_TILE * ROW_SUBTILES, LANES), F32),
        compiler_params=pltpu.CompilerParams(dimension_semantics=("arbitrary",), vmem_limit_bytes=VMEM_LIMIT),
        name="moe",
    )(sched[0, :n_tiles], sched[1, :n_tiles], sched[2, :n_tiles], sched[3, :n_tiles], xs, w1, w3, w2, w1, w3, w2)


def _ple_body(pos_ref, posn_ref, h1_ref, ys_hbm, p_ref, gple_ref, wg_ref, wp_ref, gpost_ref, o_ref, ybuf, sem):
    i = pl.program_id(0)
    n = pl.num_programs(0)
    ts = h1_ref.shape[0]
    slot = i % 2

    def row_copy(pref, sl):
        return lambda t: pltpu.make_async_copy(
            ys_hbm.at[pl.ds(pl.multiple_of(pref[0, t] * ROW_SUBTILES, ROW_SUBTILES), ROW_SUBTILES)],
            ybuf.at[sl, pl.ds(t * ROW_SUBTILES, ROW_SUBTILES)], sem.at[sl])

    @pl.when(i == 0)
    def _():
        _start_rows(ts, row_copy(pos_ref, 0))

    _wait_rows(ts, row_copy(pos_ref, slot))
    _start_rows(ts, row_copy(posn_ref, 1 - slot))

    y = jnp.concatenate([ybuf[slot, pl.ds(k, ts, stride=ROW_SUBTILES), :] for k in range(ROW_SUBTILES)], axis=1)
    h2 = h1_ref[...] + y
    ms = jnp.mean(h2 * h2, axis=-1, keepdims=True)
    hn = (h2 * lax.rsqrt(ms + EPS) * gple_ref[...]).astype(BF16)
    z = jnp.dot(hn, wg_ref[...], preferred_element_type=F32)
    gate = 1.0 / (1.0 + jnp.exp(-z))
    pp = jnp.dot(p_ref[...].astype(BF16), wp_ref[...], preferred_element_type=F32)
    pms = jnp.mean(pp * pp, axis=-1, keepdims=True)
    pn = pp * lax.rsqrt(pms + EPS) * gpost_ref[...]
    o_ref[...] = h2 + gate * pn

    @pl.when(i == n - 1)
    def _():
        _wait_rows(ts, row_copy(posn_ref, 1 - slot))


def _ple(pos, h1, ys, p, g_ple, wg, wp, g_post):
    T = h1.shape[0]
    n_tiles_tok, _, ts = pos.shape
    row = lambda i: (i, 0)
    const = lambda i: (0, 0)
    return pl.pallas_call(
        _ple_body,
        grid=(n_tiles_tok,),
        in_specs=[
            pl.BlockSpec((None, 1, ts), lambda i: (i, 0, 0), memory_space=pltpu.SMEM),
            pl.BlockSpec((None, 1, ts), lambda i: (jnp.minimum(i + 1, n_tiles_tok - 1), 0, 0),
                         memory_space=pltpu.SMEM),
            pl.BlockSpec((ts, D_MODEL), row),
            pl.BlockSpec(memory_space=pl.ANY),
            pl.BlockSpec((ts, PLE_DIM), row),
            pl.BlockSpec((1, D_MODEL), const),
            pl.BlockSpec((D_MODEL, D_MODEL), const),
            pl.BlockSpec((PLE_DIM, D_MODEL), const),
            pl.BlockSpec((1, D_MODEL), const),
        ],
        out_specs=pl.BlockSpec((ts, D_MODEL), row),
        out_shape=jax.ShapeDtypeStruct((T, D_MODEL), F32),
        scratch_shapes=[pltpu.VMEM((2, ts * ROW_SUBTILES, LANES), F32), pltpu.SemaphoreType.DMA((2,))],
        compiler_params=pltpu.CompilerParams(
            dimension_semantics=("arbitrary",), vmem_limit_bytes=VMEM_LIMIT),
        name="ple",
    )(pos, pos, h1, ys, p, g_ple, wg, wp, g_post)


def _rope_tables(S):
    pos = jnp.arange(S)
    inv = ROPE_THETA ** (-jnp.arange(0, ROPE_DIM, 2, dtype=F32) / ROPE_DIM)
    ang = inv[:, None] * pos.astype(F32)[None, :]
    cos, sin = jnp.cos(ang), jnp.sin(ang)
    lane = np.arange(LANES) % HEAD_DIM
    freq = np.arange(ROPE_HALF)[:, None]
    first = (lane[None, :] == freq).astype(np.float32)
    second = (lane[None, :] == freq + ROPE_HALF).astype(np.float32)
    rest = (lane >= ROPE_DIM).astype(np.float32)[None, :]
    spread = lambda t, m: lax.dot_general(t, jnp.asarray(m), (((0,), (0,)), ((), ())),
                                          precision=lax.Precision.HIGHEST)
    return spread(cos, first + second) + rest, spread(sin, -first), spread(sin, second)


def _layer(h, p_i, g_mix, w_in, q_norm, k_norm, conv_w, g_attn_out, g_conv_out, w_out, g_ffn,
           w_rg, b_rg, w_re, b_re, w1, w3, w2, g_ple, w_ple_gate, w_ple_proj, g_ple_post):
    B, S, _ = h.shape
    T = B * S
    row = lambda g: g.reshape(1, -1).astype(F32)

    cos_t, sa_t, sb_t = _rope_tables(S)
    bd256 = _block_diag_ones(MXU_DIM, HEAD_DIM)
    bd128 = _block_diag_ones(LANES, HEAD_DIM)
    gq = row(jnp.tile(q_norm, N_HEADS) * (HEAD_DIM ** -0.5 * LOG2_E))
    gk = row(jnp.tile(k_norm, N_HEADS))
    cw = jnp.zeros((8, CONV_WIDTH), F32).at[0:CONV_K].set(conv_w)

    q, k, v, convn = _in_proj(h, row(g_mix), w_in.astype(BF16), gq, gk, cos_t, sa_t, sb_t, bd256, cw,
                              row(g_conv_out))
    attn = _attention(q, k, v, row(g_attn_out), bd128, _band_bias())

    w_r = jnp.zeros((D_MODEL, GATE_LANES), F32)
    w_r = w_r.at[:, ROUTER_GROUP_COL:ROUTER_GROUP_COL + N_GROUPS].set(w_rg)
    w_r = w_r.at[:, ROUTER_EXPERT_COL:ROUTER_EXPERT_COL + N_EXPERTS].set(w_re)
    b_r = jnp.zeros((1, GATE_LANES), F32)
    b_r = b_r.at[0, ROUTER_GROUP_COL:ROUTER_GROUP_COL + N_GROUPS].set(b_rg)
    b_r = b_r.at[0, ROUTER_EXPERT_COL:ROUTER_EXPERT_COL + N_EXPERTS].set(b_re)
    wr = w_r.astype(BF16)

    h1, rt, rg, cnt = _out_route(h, attn, convn, w_out.astype(BF16), row(g_ffn), wr, b_r)
    n_route_tiles = T // OUT_ROUTE_TILE
    n_tok_tiles = T // ROW_TILE
    per_route = OUT_ROUTE_TILE // ROW_TILE
    n_sorted_tiles = T // MOE_TILE + N_CLASSES
    assert n_sorted_tiles <= SCHED_LANES
    pos, sched, cinfo = _route_plan(cnt, rt.reshape(n_route_tiles, 8, OUT_ROUTE_TILE))
    pos = pos.reshape(n_tok_tiles, 1, ROW_TILE)
    rg = rg.reshape(n_route_tiles, 8, per_route, ROW_TILE).transpose(0, 2, 1, 3).reshape(n_tok_tiles, 8, ROW_TILE)
    h1 = h1.reshape(T, D_MODEL)
    xs = _scatter_rows(cinfo, pos, h1, row(g_ffn), rg, n_sorted_tiles * MOE_TILE)
    ys = _moe(sched, xs, w1.astype(BF16), w3.astype(BF16), w2.astype(BF16))
    out = _ple(pos, h1, ys, p_i.reshape(T, PLE_DIM), row(g_ple),
               w_ple_gate.astype(BF16), w_ple_proj.astype(BF16), row(g_ple_post))
    return out.reshape(B, S, D_MODEL)


def kernel(x, p, g_mix, w_in, q_norm, k_norm, conv_w, g_attn_out, g_conv_out, w_out, g_ffn, w_router_group, b_router_group, w_router_expert, b_router_expert, w1, w3, w2, g_ple, w_ple_gate, w_ple_proj, g_ple_post):
    h = x
    for i in range(p.shape[0]):
        h = _layer(h, p[i], g_mix[i], w_in[i], q_norm[i], k_norm[i], conv_w[i], g_attn_out[i], g_conv_out[i],
                   w_out[i], g_ffn[i], w_router_group[i], b_router_group[i], w_router_expert[i],
                   b_router_expert[i], w1[i], w3[i], w2[i], g_ple[i], w_ple_gate[i], w_ple_proj[i],
                   g_ple_post[i])
    return h
```

```python
import functools

import numpy as np
import jax
import jax.numpy as jnp
from jax import lax
from jax.experimental import pallas as pl
from jax.experimental.pallas import tpu as pltpu

F32 = jnp.float32
BF16 = jnp.bfloat16

D_MODEL = 1024
PLE_DIM = 256
HEAD_DIM = 64
N_HEADS = 8
ATTN_WIDTH = N_HEADS * HEAD_DIM
CONV_WIDTH = D_MODEL - ATTN_WIDTH
CONV_K = 3
DILATIONS = (1, 4, 16)
N_BACK = 128
ATTN_BLOCK = 128
ROPE_THETA = 500000.0
ROPE_DIM = HEAD_DIM // 4
ROPE_HALF = ROPE_DIM // 2
N_GROUPS = 4
EXPERTS_PER_GROUP = 4
N_EXPERTS = N_GROUPS * EXPERTS_PER_GROUP
D_FF = 512
EPS = 1e-6

LANES = 128
MXU_DIM = 256
NEG_BIG = -1e30
LOG2_E = 1.4426950408889634

ROW_TILE = 512
OUT_ROUTE_TILE = 1024
IN_PROJ_TILE = 1024
ROW_SUBSTEPS = 1
ATTN_TILE = 2048
ATTN_UNROLL = 16
MOE_TILE = 256
ROW_SUBTILES = D_MODEL // LANES
GATE_LANES = LANES
SLAB_ROWS = 16
GATE_LO_LANE = 0
GATE_HI_LANE = 1
EXPERT_PAIRS = tuple((a, b) for a in range(EXPERTS_PER_GROUP) for b in range(a + 1, EXPERTS_PER_GROUP))
N_CLASSES = N_GROUPS * len(EXPERT_PAIRS)
CLASS_ROWS = 32
SCHED_LANES = 256
ROUTER_GROUP_COL = 0
ROUTER_EXPERT_COL = 8
VMEM_LIMIT = 48 * 1024 * 1024


def _block_diag_ones(n, seg):
    idx = np.arange(n) // seg
    return jnp.asarray((idx[:, None] == idx[None, :]).astype(np.float32), dtype=BF16)


def _segment_mean_sq(t, bd):
    t2 = (t * t).astype(BF16)
    parts = [jnp.dot(t2[:, i:i + MXU_DIM], bd, preferred_element_type=F32)
             for i in range(0, t.shape[1], MXU_DIM)]
    return jnp.concatenate(parts, axis=1) * (1.0 / HEAD_DIM)


def _in_proj_body(x_ref, gmix_ref, win_ref, gq_ref, gk_ref, cos_ref, sa_ref, sb_ref, bd_ref, cw_ref, gconv_ref,
                  q_ref, k_ref, v_ref, conv_ref, ubuf):
    ts = x_ref.shape[0]
    sub = ts // ROW_SUBSTEPS
    bd = bd_ref[...]
    reps = ATTN_WIDTH // LANES

    @pl.when(pl.program_id(1) == 0)
    def _():
        ubuf[0:8, :] = jnp.zeros((8, CONV_WIDTH), F32)

    for r0 in range(0, ts, sub):
        rows = pl.ds(r0, sub)
        x = x_ref[rows, :]
        ms = jnp.mean(x * x, axis=-1, keepdims=True)
        xn = (x * lax.rsqrt(ms + EPS) * gmix_ref[...]).astype(BF16)

        def proj(c):
            return jnp.dot(xn, win_ref[:, c * ATTN_WIDTH:(c + 1) * ATTN_WIDTH], preferred_element_type=F32)

        cos = jnp.concatenate([cos_ref[rows, :]] * reps, axis=1)
        sa = jnp.concatenate([sa_ref[rows, :]] * reps, axis=1)
        sb = jnp.concatenate([sb_ref[rows, :]] * reps, axis=1)

        def norm_rope(t, g):
            tn = t * lax.rsqrt(_segment_mean_sq(t, bd) + EPS) * g
            up = pltpu.roll(tn, ATTN_WIDTH - ROPE_HALF, 1)
            dn = pltpu.roll(tn, ROPE_HALF, 1)
            return tn * cos + up * sa + dn * sb

        q_ref[rows, :] = norm_rope(proj(0), gq_ref[...]).astype(BF16)
        k_ref[rows, :] = norm_rope(proj(1), gk_ref[...]).astype(BF16)
        v_ref[rows, :] = proj(2).astype(BF16)

        cb = proj(3)
        u = proj(4) * proj(5)
        ubuf[8 + r0:8 + r0 + sub, :] = u
        u1 = ubuf[7 + r0:7 + r0 + sub, :]
        u2 = ubuf[6 + r0:6 + r0 + sub, :]
        y = cw_ref[0:1, :] * u2 + cw_ref[1:2, :] * u1 + cw_ref[2:3, :] * u
        conv = cb * y
        convn = conv * lax.rsqrt(_segment_mean_sq(conv, bd) + EPS) * gconv_ref[...]
        conv_ref[rows, :] = convn.astype(BF16)

    ubuf[0:8, :] = ubuf[ts:ts + 8, :]


def _in_proj(x, g_mix, w_in, gq, gk, cos_t, sa_t, sb_t, bd, cw, g_conv):
    B, S, _ = x.shape
    ts = IN_PROJ_TILE
    row = lambda b, j: (b, j, 0)
    const2 = lambda b, j: (0, 0)
    tab = lambda b, j: (j, 0)
    out_sds = jax.ShapeDtypeStruct((B, S, ATTN_WIDTH), BF16)
    return pl.pallas_call(
        _in_proj_body,
        grid=(B, S // ts),
        in_specs=[
            pl.BlockSpec((None, ts, D_MODEL), row),
            pl.BlockSpec((1, D_MODEL), const2),
            pl.BlockSpec((D_MODEL, 6 * ATTN_WIDTH), const2),
            pl.BlockSpec((1, ATTN_WIDTH), const2),
            pl.BlockSpec((1, ATTN_WIDTH), const2),
            pl.BlockSpec((ts, LANES), tab),
            pl.BlockSpec((ts, LANES), tab),
            pl.BlockSpec((ts, LANES), tab),
            pl.BlockSpec((MXU_DIM, MXU_DIM), const2),
            pl.BlockSpec((8, CONV_WIDTH), const2),
            pl.BlockSpec((1, CONV_WIDTH), const2),
        ],
        out_specs=[pl.BlockSpec((None, ts, ATTN_WIDTH), row)] * 4,
        out_shape=[out_sds] * 4,
        scratch_shapes=[pltpu.VMEM((ts + 8, CONV_WIDTH), F32)],
        compiler_params=pltpu.CompilerParams(
            dimension_semantics=("arbitrary", "arbitrary"), vmem_limit_bytes=VMEM_LIMIT),
        name="in_proj",
    )(x, g_mix, w_in, gq, gk, cos_t, sa_t, sb_t, bd, cw, g_conv)


def _attn_body(q_ref, kp_ref, kc_ref, vp_ref, vc_ref, gat_ref, bd_ref, bias_ref, o_ref,
               qf, kf, vf, acc_a, acc_b, m_a, m_b):
    tq = q_ref.shape[0]
    blk = ATTN_BLOCK
    first_tile = pl.program_id(2) == 0

    qf[...] = q_ref[...].astype(F32)
    kf[0:tq, :] = kp_ref[...].astype(F32)
    kf[tq:2 * tq, :] = kc_ref[...].astype(F32)
    vf[0:tq, :] = vp_ref[...].astype(F32)
    vf[tq:2 * tq, :] = vc_ref[...].astype(F32)

    lane = lax.broadcasted_iota(jnp.int32, (1, LANES), 1)
    sel_a = (lane < HEAD_DIM).astype(F32).astype(BF16)
    sel_b = (lane >= HEAD_DIM).astype(F32).astype(BF16)

    order = DILATIONS[::-1]
    for d in order:
        nblk = tq // (blk * d)
        shift = nblk.bit_length() - 1
        first_visit = d == order[0]

        def block(idx, carry, d=d, nblk=nblk, shift=shift, first_visit=first_visit):
            r = idx >> shift
            n = idx & (nblk - 1)
            rows_q = pl.ds(r + d * blk * n, blk, stride=d)
            rows_k = pl.ds(tq + d * blk * (n - 1) + r, 2 * blk, stride=d)
            qw = qf[rows_q, :].astype(BF16)
            kw = kf[rows_k, :].astype(BF16)
            vw = vf[rows_k, :].astype(BF16)
            bias = bias_ref[jnp.where(jnp.logical_and(first_tile, n == 0), 1, 0)]

            def one_head(sel_q, sel_one, acc, mst):
                s = lax.dot_general(qw * sel_q, kw, (((1,), (1,)), ((), ())), preferred_element_type=F32) + bias
                mb = jnp.max(s, axis=-1, keepdims=True)
                if first_visit:
                    m_new = jnp.broadcast_to(mb, (blk, LANES))
                else:
                    m_old = mst[rows_q, :]
                    m_new = jnp.maximum(m_old, mb)
                e = jnp.exp2(s - jnp.concatenate([m_new, m_new], axis=1))
                pv = jnp.dot(e.astype(BF16), vw * sel_q + sel_one, preferred_element_type=F32)
                if first_visit:
                    acc[rows_q, :] = pv
                else:
                    acc[rows_q, :] = acc[rows_q, :] * jnp.exp2(m_old - m_new) + pv
                mst[rows_q, :] = m_new

            one_head(sel_a, sel_b, acc_a, m_a)
            one_head(sel_b, sel_a, acc_b, m_b)
            return carry

        lax.fori_loop(0, tq // blk, block, 0, unroll=ATTN_UNROLL)

    head_a = lax.broadcasted_iota(jnp.int32, (tq, LANES), 1) < HEAD_DIM
    aa = acc_a[...]
    ab = acc_b[...]
    num = jnp.where(head_a, aa, ab)
    den = pltpu.roll(jnp.where(head_a, ab, aa), HEAD_DIM, 1)
    o = num / den
    o2 = (o * o).astype(BF16)
    msq = jnp.dot(o2, bd_ref[...], preferred_element_type=F32) * (1.0 / HEAD_DIM)
    o_ref[...] = (o * lax.rsqrt(msq + EPS) * gat_ref[...]).astype(BF16)


def _attention(q, k, v, g_attn, bd, bias):
    B, S, _ = q.shape
    tq = ATTN_TILE
    n_pairs = ATTN_WIDTH // LANES
    cur = lambda b, hp, j: (b, j, hp)
    prev = lambda b, hp, j: (b, jnp.maximum(j - 1, 0), hp)
    blk = (None, tq, LANES)
    return pl.pallas_call(
        _attn_body,
        grid=(B, n_pairs, S // tq),
        in_specs=[
            pl.BlockSpec(blk, cur),
            pl.BlockSpec(blk, prev),
            pl.BlockSpec(blk, cur),
            pl.BlockSpec(blk, prev),
            pl.BlockSpec(blk, cur),
            pl.BlockSpec((1, LANES), lambda b, hp, j: (0, hp)),
            pl.BlockSpec((LANES, LANES), lambda b, hp, j: (0, 0)),
            pl.BlockSpec((2, ATTN_BLOCK, 2 * ATTN_BLOCK), lambda b, hp, j: (0, 0, 0)),
        ],
        out_specs=pl.BlockSpec(blk, cur),
        out_shape=jax.ShapeDtypeStruct((B, S, ATTN_WIDTH), BF16),
        scratch_shapes=[
            pltpu.VMEM((tq, LANES), F32),
            pltpu.VMEM((2 * tq, LANES), F32), pltpu.VMEM((2 * tq, LANES), F32),
            pltpu.VMEM((tq, LANES), F32), pltpu.VMEM((tq, LANES), F32),
            pltpu.VMEM((tq, LANES), F32), pltpu.VMEM((tq, LANES), F32),
        ],
        compiler_params=pltpu.CompilerParams(
            dimension_semantics=("arbitrary", "arbitrary", "arbitrary"), vmem_limit_bytes=VMEM_LIMIT),
        name="attention",
    )(q, k, k, v, v, g_attn, bd, bias)


def _band_bias():
    qi = np.arange(ATTN_BLOCK)[:, None]
    ki = np.arange(2 * ATTN_BLOCK)[None, :]
    dist = qi + ATTN_BLOCK - ki
    band = (dist >= 0) & (dist <= N_BACK)
    no_prev = band & (ki >= ATTN_BLOCK)
    tab = np.stack([band, no_prev]).astype(np.float32)
    return jnp.asarray((1.0 - tab) * NEG_BIG, dtype=F32)


def _first_argmax(vals):
    best = vals[0]
    idx = jnp.zeros(best.shape, jnp.int32)
    for i in range(1, len(vals)):
        upd = vals[i] > best
        idx = jnp.where(upd, i, idx)
        best = jnp.where(upd, vals[i], best)
    return best, idx


def _out_route_body(x_ref, attn_ref, conv_ref, wo_ref, gffn_ref, wr_ref, br_ref, upper_ref,
                    h1_ref, rt_ref, rg_ref, cnt_ref, cnt):
    ts = x_ref.shape[0]
    h1 = (x_ref[...]
          + jnp.dot(attn_ref[...], wo_ref[0:ATTN_WIDTH, :], preferred_element_type=F32)
          + jnp.dot(conv_ref[...], wo_ref[ATTN_WIDTH:, :], preferred_element_type=F32))
    h1_ref[...] = h1
    ms = jnp.mean(h1 * h1, axis=-1, keepdims=True)
    xn = h1 * lax.rsqrt(ms + EPS) * gffn_ref[...]

    logits = jnp.dot(xn.astype(BF16), wr_ref[...], preferred_element_type=F32) + br_ref[...]
    lt = logits.T

    lg = [lt[ROUTER_GROUP_COL + i:ROUTER_GROUP_COL + i + 1, :] for i in range(N_GROUPS)]
    gbest, gi = _first_argmax(lg)
    sumexp = lg[0] * 0.0
    for i in range(N_GROUPS):
        sumexp = sumexp + jnp.exp(lg[i] - gbest)
    pg_top = 1.0 / sumexp

    sel = []
    for jx in range(EXPERTS_PER_GROUP):
        cand = [lt[ROUTER_EXPERT_COL + EXPERTS_PER_GROUP * g + jx:ROUTER_EXPERT_COL + EXPERTS_PER_GROUP * g + jx + 1, :]
                for g in range(N_GROUPS)]
        vj = cand[N_GROUPS - 1]
        for g in range(N_GROUPS - 2, -1, -1):
            vj = jnp.where(gi == g, cand[g], vj)
        sel.append(vj)
    b1, i1 = _first_argmax(sel)
    masked = [jnp.where(i1 == jx, -jnp.inf, sel[jx]) for jx in range(EXPERTS_PER_GROUP)]
    b2, i2 = _first_argmax(masked)
    e2 = jnp.exp(b2 - b1)
    t1 = 1.0 / (1.0 + e2)
    ga = pg_top * t1
    gb = pg_top * (e2 * t1)
    lo = jnp.minimum(i1, i2)
    hi = jnp.maximum(i1, i2)
    g_lo = jnp.where(i1 < i2, ga, gb)
    g_hi = jnp.where(i1 < i2, gb, ga)
    pair = hi - lo - 1
    for a in range(1, EXPERTS_PER_GROUP - 1):
        pair = pair + jnp.where(lo >= a, EXPERT_PAIRS.index((a, a + 1)) - EXPERT_PAIRS.index((a - 1, a)), 0)
    cls = gi * len(EXPERT_PAIRS) + pair

    r8 = lax.broadcasted_iota(jnp.int32, (8, ts), 0)
    rg_ref[...] = jnp.where(r8 == 0, g_lo, jnp.where(r8 == 1, g_hi, 0.0))

    @pl.when(jnp.logical_and(pl.program_id(0) == 0, pl.program_id(1) == 0))
    def _():
        cnt[...] = jnp.zeros(cnt.shape, F32)

    crow = lax.broadcasted_iota(jnp.int32, (CLASS_ROWS, ts), 0)
    oh = (crow == cls).astype(F32)
    before = jnp.dot(oh.astype(BF16), upper_ref[...], preferred_element_type=F32)
    rank = jnp.sum(oh * (before + cnt[:, 0:1]), axis=0, keepdims=True)
    cnt[...] = cnt[...] + jnp.sum(oh, axis=1, keepdims=True)
    cnt_ref[...] = cnt[...]
    rt_ref[...] = jnp.where(r8 == 0, cls, jnp.where(r8 == 1, rank.astype(jnp.int32), 0))


def _out_route(x, attn, conv, w_out, g_ffn, wr, b_r):
    B, S, _ = x.shape
    ts = OUT_ROUTE_TILE
    row = lambda b, j: (b, j, 0)
    const2 = lambda b, j: (0, 0)
    tok = np.arange(ts)
    upper = jnp.asarray((tok[:, None] < tok[None, :]).astype(np.float32), dtype=BF16)
    return pl.pallas_call(
        _out_route_body,
        grid=(B, S // ts),
        in_specs=[
            pl.BlockSpec((None, ts, D_MODEL), row),
            pl.BlockSpec((None, ts, ATTN_WIDTH), row),
            pl.BlockSpec((None, ts, CONV_WIDTH), row),
            pl.BlockSpec((D_MODEL, D_MODEL), const2),
            pl.BlockSpec((1, D_MODEL), const2),
            pl.BlockSpec((D_MODEL, GATE_LANES), const2),
            pl.BlockSpec((1, GATE_LANES), const2),
            pl.BlockSpec((ts, ts), const2),
        ],
        out_specs=[
            pl.BlockSpec((None, ts, D_MODEL), row),
            pl.BlockSpec((None, None, 8, ts), lambda b, j: (b, j, 0, 0)),
            pl.BlockSpec((None, None, 8, ts), lambda b, j: (b, j, 0, 0)),
            pl.BlockSpec((CLASS_ROWS, LANES), const2),
        ],
        out_shape=[
            jax.ShapeDtypeStruct((B, S, D_MODEL), F32),
            jax.ShapeDtypeStruct((B, S // ts, 8, ts), jnp.int32),
            jax.ShapeDtypeStruct((B, S // ts, 8, ts), F32),
            jax.ShapeDtypeStruct((CLASS_ROWS, LANES), F32),
        ],
        scratch_shapes=[pltpu.VMEM((CLASS_ROWS, LANES), F32)],
        compiler_params=pltpu.CompilerParams(
            dimension_semantics=("arbitrary", "arbitrary"), vmem_limit_bytes=VMEM_LIMIT),
        name="out_route",
    )(x, attn, conv, w_out, g_ffn, wr, b_r, upper)


def _route_plan_body(cnt_ref, rt_ref, elo_ref, ehi_ref, pos_ref, sched_ref, cinfo_ref):
    n_tiles_tok, _, ts = pos_ref.shape
    shift = MOE_TILE.bit_length() - 1
    ntile = (cnt_ref[...].astype(jnp.int32) + (MOE_TILE - 1)) >> shift
    crow = lax.broadcasted_iota(jnp.int32, (CLASS_ROWS, LANES), 0)
    incl = ntile
    step = 1
    while step < CLASS_ROWS:
        incl = incl + jnp.where(crow >= step, pltpu.roll(incl, step, 0), 0)
        step *= 2
    tstart = incl - ntile
    total = incl[CLASS_ROWS - 1:CLASS_ROWS, 0:1]

    row_base = jnp.broadcast_to(tstart[:, 0:1] * MOE_TILE, (CLASS_ROWS, ts))
    crow_t = lax.broadcasted_iota(jnp.int32, (CLASS_ROWS, ts), 0)

    def token_tile(i, carry):
        cls = rt_ref[i, 0:1, :]
        rank = rt_ref[i, 1:2, :]
        base = jnp.sum(jnp.where(crow_t == cls, row_base, 0), axis=0, keepdims=True)
        pos_ref[i] = base + rank
        return carry

    lax.fori_loop(0, n_tiles_tok, token_tile, 0)

    lane_j = lax.broadcasted_iota(jnp.int32, (CLASS_ROWS, SCHED_LANES), 1)
    start_b = jnp.broadcast_to(tstart[:, 0:1], (CLASS_ROWS, SCHED_LANES))
    ntile_b = jnp.broadcast_to(ntile[:, 0:1], (CLASS_ROWS, SCHED_LANES))
    member = jnp.logical_and(lane_j >= start_b, lane_j < start_b + ntile_b)
    pick = lambda tab: jnp.sum(jnp.where(member, jnp.broadcast_to(tab[:, 0:1], member.shape), 0),
                               axis=0, keepdims=True)
    valid = jnp.sum(member.astype(jnp.int32), axis=0, keepdims=True)
    elo = pick(elo_ref[...])
    ehi = pick(ehi_ref[...])
    j1 = lane_j[0:1, :]
    last = total - 1
    at_last = j1 == last
    elo_last = jnp.sum(jnp.where(at_last, elo, 0), axis=1, keepdims=True)
    ehi_last = jnp.sum(jnp.where(at_last, ehi, 0), axis=1, keepdims=True)
    in_use = valid > 0
    elo = jnp.where(in_use, elo, elo_last)
    ehi = jnp.where(in_use, ehi, ehi_last)
    blk = jnp.minimum(j1, last)
    r8 = lax.broadcasted_iota(jnp.int32, (8, SCHED_LANES), 0)
    sched_ref[...] = jnp.where(r8 == 0, elo, jnp.where(r8 == 1, ehi, jnp.where(r8 == 2, valid,
                               jnp.where(r8 == 3, blk, 0))))
    lane_c = lax.broadcasted_iota(jnp.int32, (CLASS_ROWS, LANES), 1)
    cinfo_ref[...] = jnp.where(lane_c == 0, tstart + ntile - 1,
                               jnp.where(lane_c == 1, (ntile > 0).astype(jnp.int32), total))


def _route_plan(cnt, rt):
    n_tiles_tok, _, ts = rt.shape
    elo_tab = np.zeros((CLASS_ROWS, LANES), np.int32)
    ehi_tab = np.zeros((CLASS_ROWS, LANES), np.int32)
    for g in range(N_GROUPS):
        for p, (a, b) in enumerate(EXPERT_PAIRS):
            elo_tab[g * len(EXPERT_PAIRS) + p, :] = g * EXPERTS_PER_GROUP + a
            ehi_tab[g * len(EXPERT_PAIRS) + p, :] = g * EXPERTS_PER_GROUP + b
    full = lambda shape: pl.BlockSpec(shape, lambda i: (0,) * len(shape))
    return pl.pallas_call(
        _route_plan_body,
        grid=(1,),
        in_specs=[full((CLASS_ROWS, LANES)), full(rt.shape), full((CLASS_ROWS, LANES)), full((CLASS_ROWS, LANES))],
        out_specs=[full((n_tiles_tok, 1, ts)), full((8, SCHED_LANES)), full((CLASS_ROWS, LANES))],
        out_shape=[
            jax.ShapeDtypeStruct((n_tiles_tok, 1, ts), jnp.int32),
            jax.ShapeDtypeStruct((8, SCHED_LANES), jnp.int32),
            jax.ShapeDtypeStruct((CLASS_ROWS, LANES), jnp.int32),
        ],
        compiler_params=pltpu.CompilerParams(dimension_semantics=("arbitrary",)),
        name="route_plan",
    )(cnt, rt, jnp.asarray(elo_tab), jnp.asarray(ehi_tab))


DMA_UNROLL = 8


def _start_rows(n_rows, make_copy):
    for t in range(n_rows):
        make_copy(t).start(priority=t % 2)


def _wait_rows(n_rows, make_copy):
    def trip(t8, carry):
        for _ in range(DMA_UNROLL):
            make_copy(0).wait()
        return carry
    lax.fori_loop(0, n_rows // DMA_UNROLL, trip, 0)


def _scatter_rows_body(last_ref, has_ref, total_ref, pos_ref, h1_ref, gffn_ref, rg_ref, xs_hbm,
                       rows, zbuf, sem, zsem):
    i = pl.program_id(0)
    n = pl.num_programs(0)
    ts = h1_ref.shape[0]
    slot = i % 2
    tile_rows = MOE_TILE * SLAB_ROWS

    n_dst_tiles = xs_hbm.shape[0] // tile_rows

    def pad_copies():
        for c in range(N_CLASSES):
            yield has_ref[c] != 0, last_ref[c]
            yield total_ref[0] + c < n_dst_tiles, total_ref[0] + c

    def tile_copy(tile):
        return pltpu.make_async_copy(zbuf, xs_hbm.at[pl.ds(tile * tile_rows, tile_rows)], zsem)

    @pl.when(i == 0)
    def _():
        rows[...] = jnp.zeros(rows.shape, rows.dtype)
        zbuf[...] = jnp.zeros(zbuf.shape, zbuf.dtype)
        for cond, tile in pad_copies():
            @pl.when(cond)
            def _():
                tile_copy(tile).start()
        for cond, tile in pad_copies():
            @pl.when(cond)
            def _():
                tile_copy(tile).wait()

    h1 = h1_ref[...]
    ms = jnp.mean(h1 * h1, axis=-1, keepdims=True)
    xn = h1 * lax.rsqrt(ms + EPS) * gffn_ref[...]
    for k in range(ROW_SUBTILES):
        rows[slot, pl.ds(k, ts, stride=SLAB_ROWS), :] = xn[:, k * LANES:(k + 1) * LANES]
    erow = lax.broadcasted_iota(jnp.int32, (GATE_LANES, ts), 0)
    gt = (jnp.where(erow == GATE_LO_LANE, rg_ref[0:1, :], 0.0)
          + jnp.where(erow == GATE_HI_LANE, rg_ref[1:2, :], 0.0))
    rows[slot, pl.ds(ROW_SUBTILES, ts, stride=SLAB_ROWS), :] = gt.T

    def row_copy(sl):
        return lambda t: pltpu.make_async_copy(
            rows.at[sl, pl.ds(t * SLAB_ROWS, SLAB_ROWS)],
            xs_hbm.at[pl.ds(pl.multiple_of(pos_ref[0, t] * SLAB_ROWS, SLAB_ROWS), SLAB_ROWS)], sem.at[sl])

    _start_rows(ts, row_copy(slot))

    @pl.when(i > 0)
    def _():
        _wait_rows(ts, row_copy(1 - slot))

    @pl.when(i == n - 1)
    def _():
        _wait_rows(ts, row_copy(slot))


def _scatter_rows(cinfo, pos, h1, g_ffn, rg, n_dst_rows):
    n_tiles_tok, _, ts = pos.shape
    return pl.pallas_call(
        _scatter_rows_body,
        grid_spec=pltpu.PrefetchScalarGridSpec(
            num_scalar_prefetch=3,
            grid=(n_tiles_tok,),
            in_specs=[
                pl.BlockSpec((None, 1, ts), lambda i, *_: (i, 0, 0), memory_space=pltpu.SMEM),
                pl.BlockSpec((ts, D_MODEL), lambda i, *_: (i, 0)),
                pl.BlockSpec((1, D_MODEL), lambda i, *_: (0, 0)),
                pl.BlockSpec((None, 8, ts), lambda i, *_: (i, 0, 0)),
            ],
            out_specs=pl.BlockSpec(memory_space=pl.ANY),
            scratch_shapes=[
                pltpu.VMEM((2, ts * SLAB_ROWS, LANES), F32),
                pltpu.VMEM((MOE_TILE * SLAB_ROWS, LANES), F32),
                pltpu.SemaphoreType.DMA((2,)),
                pltpu.SemaphoreType.DMA(()),
            ],
        ),
        out_shape=jax.ShapeDtypeStruct((n_dst_rows * SLAB_ROWS, LANES), F32),
        compiler_params=pltpu.CompilerParams(dimension_semantics=("arbitrary",), vmem_limit_bytes=VMEM_LIMIT),
        name="scatter_rows",
    )(cinfo[:, 0], cinfo[:, 1], cinfo[0:1, 2], pos, h1, g_ffn, rg)


def _moe_body(elo_ref, ehi_ref, valid_ref, blk_ref, xs_ref, w1l_ref, w3l_ref, w2l_ref, w1h_ref, w3h_ref, w2h_ref,
              ys_ref):
    @pl.when(valid_ref[pl.program_id(0)] != 0)
    def _():
        slab_row = lambda k: xs_ref[pl.ds(k, MOE_TILE, stride=SLAB_ROWS), :]
        x = jnp.concatenate([slab_row(k).astype(BF16) for k in range(ROW_SUBTILES)], axis=1)
        gates = slab_row(ROW_SUBTILES)

        def expert(w1_ref, w3_ref, w2_ref):
            a = jnp.dot(x, w1_ref[...], preferred_element_type=F32)
            b = jnp.dot(x, w3_ref[...], preferred_element_type=F32)
            hdn = (a * (1.0 / (1.0 + jnp.exp(-a))) * b).astype(BF16)
            return jnp.dot(hdn, w2_ref[...], preferred_element_type=F32)

        g_lo = gates[:, GATE_LO_LANE:GATE_LO_LANE + 1]
        g_hi = gates[:, GATE_HI_LANE:GATE_HI_LANE + 1]
        y = g_lo * expert(w1l_ref, w3l_ref, w2l_ref) + g_hi * expert(w1h_ref, w3h_ref, w2h_ref)
        for k in range(ROW_SUBTILES):
            ys_ref[pl.ds(k, MOE_TILE, stride=ROW_SUBTILES), :] = y[:, k * LANES:(k + 1) * LANES]

    @pl.when(valid_ref[pl.program_id(0)] == 0)
    def _():
        ys_ref[...] = jnp.zeros(ys_ref.shape, F32)


def _moe(sched, xs, w1, w3, w2):
    n_tiles = xs.shape[0] // (MOE_TILE * SLAB_ROWS)
    rows = lambda j, elo, ehi, valid, blk: (blk[j], 0)
    w_lo = lambda j, elo, ehi, valid, blk: (elo[j], 0, 0)
    w_hi = lambda j, elo, ehi, valid, blk: (ehi[j], 0, 0)
    up = (None, D_MODEL, D_FF)
    down = (None, D_FF, D_MODEL)
    return pl.pallas_call(
        _moe_body,
        grid_spec=pltpu.PrefetchScalarGridSpec(
            num_scalar_prefetch=4,
            grid=(n_tiles,),
            in_specs=[
                pl.BlockSpec((MOE_TILE * SLAB_ROWS, LANES), rows),
                pl.BlockSpec(up, w_lo), pl.BlockSpec(up, w_lo), pl.BlockSpec(down, w_lo),
                pl.BlockSpec(up, w_hi), pl.BlockSpec(up, w_hi), pl.BlockSpec(down, w_hi),
            ],
            out_specs=pl.BlockSpec((MOE_TILE * ROW_SUBTILES, LANES), lambda j, *_: (j, 0)),
        ),
        out_shape=jax.ShapeDtypeStruct((n_tiles * MOE_TILE * ROW_SUBTILES, LANES), F32),
        compiler_params=pltpu.CompilerParams(dimension_semantics=("arbitrary",), vmem_limit_bytes=VMEM_LIMIT),
        name="moe",
    )(sched[0, :n_tiles], sched[1, :n_tiles], sched[2, :n_tiles], sched[3, :n_tiles], xs, w1, w3, w2, w1, w3, w2)


def _ple_body(pos_ref, posn_ref, h1_ref, ys_hbm, p_ref, gple_ref, wg_ref, wp_ref, gpost_ref, o_ref, ybuf, sem):
    i = pl.program_id(0)
    n = pl.num_programs(0)
    ts = h1_ref.shape[0]
    slot = i % 2

    def row_copy(pref, sl):
        return lambda t: pltpu.make_async_copy(
            ys_hbm.at[pl.ds(pl.multiple_of(pref[0, t] * ROW_SUBTILES, ROW_SUBTILES), ROW_SUBTILES)],
            ybuf.at[sl, pl.ds(t * ROW_SUBTILES, ROW_SUBTILES)], sem.at[sl])

    @pl.when(i == 0)
    def _():
        _start_rows(ts, row_copy(pos_ref, 0))

    _wait_rows(ts, row_copy(pos_ref, slot))
    _start_rows(ts, row_copy(posn_ref, 1 - slot))

    y = jnp.concatenate([ybuf[slot, pl.ds(k, ts, stride=ROW_SUBTILES), :] for k in range(ROW_SUBTILES)], axis=1)
    h2 = h1_ref[...] + y
    ms = jnp.mean(h2 * h2, axis=-1, keepdims=True)
    hn = (h2 * lax.rsqrt(ms + EPS) * gple_ref[...]).astype(BF16)
    z = jnp.dot(hn, wg_ref[...], preferred_element_type=F32)
    gate = 1.0 / (1.0 + jnp.exp(-z))
    pp = jnp.dot(p_ref[...].astype(BF16), wp_ref[...], preferred_element_type=F32)
    pms = jnp.mean(pp * pp, axis=-1, keepdims=True)
    pn = pp * lax.rsqrt(pms + EPS) * gpost_ref[...]
    o_ref[...] = h2 + gate * pn

    @pl.when(i == n - 1)
    def _():
        _wait_rows(ts, row_copy(posn_ref, 1 - slot))


def _ple(pos, h1, ys, p, g_ple, wg, wp, g_post):
    T = h1.shape[0]
    n_tiles_tok, _, ts = pos.shape
    row = lambda i: (i, 0)
    const = lambda i: (0, 0)
    return pl.pallas_call(
        _ple_body,
        grid=(n_tiles_tok,),
        in_specs=[
            pl.BlockSpec((None, 1, ts), lambda i: (i, 0, 0), memory_space=pltpu.SMEM),
            pl.BlockSpec((None, 1, ts), lambda i: (jnp.minimum(i + 1, n_tiles_tok - 1), 0, 0),
                         memory_space=pltpu.SMEM),
            pl.BlockSpec((ts, D_MODEL), row),
            pl.BlockSpec(memory_space=pl.ANY),
            pl.BlockSpec((ts, PLE_DIM), row),
            pl.BlockSpec((1, D_MODEL), const),
            pl.BlockSpec((D_MODEL, D_MODEL), const),
            pl.BlockSpec((PLE_DIM, D_MODEL), const),
            pl.BlockSpec((1, D_MODEL), const),
        ],
        out_specs=pl.BlockSpec((ts, D_MODEL), row),
        out_shape=jax.ShapeDtypeStruct((T, D_MODEL), F32),
        scratch_shapes=[pltpu.VMEM((2, ts * ROW_SUBTILES, LANES), F32), pltpu.SemaphoreType.DMA((2,))],
        compiler_params=pltpu.CompilerParams(
            dimension_semantics=("arbitrary",), vmem_limit_bytes=VMEM_LIMIT),
        name="ple",
    )(pos, pos, h1, ys, p, g_ple, wg, wp, g_post)


def _rope_tables(S):
    pos = jnp.arange(S)
    inv = ROPE_THETA ** (-jnp.arange(0, ROPE_DIM, 2, dtype=F32) / ROPE_DIM)
    ang = inv[:, None] * pos.astype(F32)[None, :]
    cos, sin = jnp.cos(ang), jnp.sin(ang)
    lane = np.arange(LANES) % HEAD_DIM
    freq = np.arange(ROPE_HALF)[:, None]
    first = (lane[None, :] == freq).astype(np.float32)
    second = (lane[None, :] == freq + ROPE_HALF).astype(np.float32)
    rest = (lane >= ROPE_DIM).astype(np.float32)[None, :]
    spread = lambda t, m: lax.dot_general(t, jnp.asarray(m), (((0,), (0,)), ((), ())),
                                          precision=lax.Precision.HIGHEST)
    return spread(cos, first + second) + rest, spread(sin, -first), spread(sin, second)


def _layer(h, p_i, g_mix, w_in, q_norm, k_norm, conv_w, g_attn_out, g_conv_out, w_out, g_ffn,
           w_rg, b_rg, w_re, b_re, w1, w3, w2, g_ple, w_ple_gate, w_ple_proj, g_ple_post):
    B, S, _ = h.shape
    T = B * S
    row = lambda g: g.reshape(1, -1).astype(F32)

    cos_t, sa_t, sb_t = _rope_tables(S)
    bd256 = _block_diag_ones(MXU_DIM, HEAD_DIM)
    bd128 = _block_diag_ones(LANES, HEAD_DIM)
    gq = row(jnp.tile(q_norm, N_HEADS) * (HEAD_DIM ** -0.5 * LOG2_E))
    gk = row(jnp.tile(k_norm, N_HEADS))
    cw = jnp.zeros((8, CONV_WIDTH), F32).at[0:CONV_K].set(conv_w)

    q, k, v, convn = _in_proj(h, row(g_mix), w_in.astype(BF16), gq, gk, cos_t, sa_t, sb_t, bd256, cw,
                              row(g_conv_out))
    attn = _attention(q, k, v, row(g_attn_out), bd128, _band_bias())

    w_r = jnp.zeros((D_MODEL, GATE_LANES), F32)
    w_r = w_r.at[:, ROUTER_GROUP_COL:ROUTER_GROUP_COL + N_GROUPS].set(w_rg)
    w_r = w_r.at[:, ROUTER_EXPERT_COL:ROUTER_EXPERT_COL + N_EXPERTS].set(w_re)
    b_r = jnp.zeros((1, GATE_LANES), F32)
    b_r = b_r.at[0, ROUTER_GROUP_COL:ROUTER_GROUP_COL + N_GROUPS].set(b_rg)
    b_r = b_r.at[0, ROUTER_EXPERT_COL:ROUTER_EXPERT_COL + N_EXPERTS].set(b_re)
    wr = w_r.astype(BF16)

    h1, rt, rg, cnt = _out_route(h, attn, convn, w_out.astype(BF16), row(g_ffn), wr, b_r)
    n_route_tiles = T // OUT_ROUTE_TILE
    n_tok_tiles = T // ROW_TILE
    per_route = OUT_ROUTE_TILE // ROW_TILE
    n_sorted_tiles = T // MOE_TILE + N_CLASSES
    assert n_sorted_tiles <= SCHED_LANES
    pos, sched, cinfo = _route_plan(cnt, rt.reshape(n_route_tiles, 8, OUT_ROUTE_TILE))
    pos = pos.reshape(n_tok_tiles, 1, ROW_TILE)
    rg = rg.reshape(n_route_tiles, 8, per_route, ROW_TILE).transpose(0, 2, 1, 3).reshape(n_tok_tiles, 8, ROW_TILE)
    h1 = h1.reshape(T, D_MODEL)
    xs = _scatter_rows(cinfo, pos, h1, row(g_ffn), rg, n_sorted_tiles * MOE_TILE)
    ys = _moe(sched, xs, w1.astype(BF16), w3.astype(BF16), w2.astype(BF16))
    out = _ple(pos, h1, ys, p_i.reshape(T, PLE_DIM), row(g_ple),
               w_ple_gate.astype(BF16), w_ple_proj.astype(BF16), row(g_ple_post))
    return out.reshape(B, S, D_MODEL)


def kernel(x, p, g_mix, w_in, q_norm, k_norm, conv_w, g_attn_out, g_conv_out, w_out, g_ffn, w_router_group, b_router_group, w_router_expert, b_router_expert, w1, w3, w2, g_ple, w_ple_gate, w_ple_proj, g_ple_post):
    h = x
    for i in range(p.shape[0]):
        h = _layer(h, p[i], g_mix[i], w_in[i], q_norm[i], k_norm[i], conv_w[i], g_attn_out[i], g_conv_out[i],
                   w_out[i], g_ffn[i], w_router_group[i], b_router_group[i], w_router_expert[i],
                   b_router_expert[i], w1[i], w3[i], w2[i], g_ple[i], w_ple_gate[i], w_ple_proj[i],
                   g_ple_post[i])
    return h
```
